```python
import jax, jax.numpy as jnp
from jax import lax
import numpy as np

D_MODEL = 2048
BATCH = 4
SEQ = 4096
DEPTH = 1
DEC_BATCH = 16
DEC_SEQ = 16
PAST_LEN = 1024

CHUNK = 64
D_MIX = D_MODEL
D_A = D_MIX // 2
D_B = D_MIX - D_A
GMLP_CHUNK = 128
N_HEADS_A = 8
HEAD_DIM_A = D_A // N_HEADS_A
POOL_WINDOWS = (2, 4, 8, 16)
N_POOL_GROUPS = len(POOL_WINDOWS)
POOL_GROUP_DIM = D_B // N_POOL_GROUPS
POOL_MAX = 16
POOL_STATE = POOL_MAX - 1
N_EXPERTS = 32
TOP_K = 4
D_FF = D_MODEL
SWIGLU_LIMIT = 7.0
SWIGLU_ALPHA = 1.702
EXPERT_BLOCK = 128
EPS = 1e-5

kernel_name = 'hymba_gmlp_pool_moe_stream_step'


def rms_norm(x, g):
    xf = x.astype(jnp.float32)
    y = xf * lax.rsqrt(jnp.mean(xf * xf, axis=-1, keepdims=True) + EPS)
    return (y * g.astype(jnp.float32)).astype(x.dtype)


def layer_norm(x, g, b):
    xf = x.astype(jnp.float32)
    mu = jnp.mean(xf, axis=-1, keepdims=True)
    xc = xf - mu
    y = xc * lax.rsqrt(jnp.mean(xc * xc, axis=-1, keepdims=True) + EPS)
    return (y * g.astype(jnp.float32) + b.astype(jnp.float32)).astype(x.dtype)


def split_mixer_inputs(h, w_in):
    z = jnp.einsum('btd,de->bte', h, w_in)
    u = jax.nn.gelu(z[..., :D_A], approximate=False)
    v = z[..., D_A:2 * D_A]
    p = z[..., 2 * D_A:]
    return u, v, p


def spatial_gate_prompt(u, v_n, w_s, b_s):
    bsz, s, _ = u.shape
    nc = s // GMLP_CHUNK
    mask = jnp.tril(jnp.ones((GMLP_CHUNK, GMLP_CHUNK), dtype=bool))
    ws = jnp.where(mask[None], w_s, jnp.zeros_like(w_s))
    vc = v_n.reshape(bsz, nc, GMLP_CHUNK, N_HEADS_A, HEAD_DIM_A)
    mixed = jnp.einsum('hts,bcshd->bcthd', ws, vc) + b_s.T[None, None, :, :, None]
    return u * mixed.reshape(bsz, s, D_A)


def spatial_gate_sample(u, v_n, w_s, b_s):
    bsz, t, _ = u.shape
    mask = jnp.tril(jnp.ones((t, t), dtype=bool))
    ws = w_s[:, :t, :t]
    ws = jnp.where(mask[None], ws, jnp.zeros_like(ws))
    vc = v_n.reshape(bsz, t, N_HEADS_A, HEAD_DIM_A)
    mixed = jnp.einsum('hts,bshd->bthd', ws, vc) + b_s[:, :t].T[None, :, :, None]
    return u * mixed.reshape(bsz, t, D_A)


def pool_mixer(p_ext, n_prev, first_pos, w_pool, pool_scale):
    bsz, length, _ = p_ext.shape
    t = length - n_prev
    pf = p_ext.astype(jnp.float32)
    cs = jnp.cumsum(jnp.pad(pf, ((0, 0), (POOL_MAX, 0), (0, 0))), axis=1)
    e0 = POOL_MAX + n_prev
    pos = first_pos + jnp.arange(t, dtype=jnp.int32)
    means = []
    for g, w in enumerate(POOL_WINDOWS):
        c0 = g * POOL_GROUP_DIM
        c1 = c0 + POOL_GROUP_DIM
        wsum = cs[:, e0:e0 + t, c0:c1] - cs[:, e0 - w:e0 - w + t, c0:c1]
        cnt = jnp.minimum(pos + 1, w).astype(jnp.float32)[None, :, None]
        means.append(wsum / cnt)
    diff = jnp.concatenate(means, axis=-1) - pf[:, n_prev:]
    diff = diff.reshape(bsz, t, N_POOL_GROUPS, POOL_GROUP_DIM).astype(p_ext.dtype)
    out = jnp.einsum('btgc,gce->btge', diff, w_pool).reshape(bsz, t, D_B)
    return out * pool_scale


def merge_groups(a, b, g_a, g_b, w_out):
    cat = jnp.concatenate([rms_norm(a, g_a), rms_norm(b, g_b)], axis=-1)
    return jnp.einsum('bte,ed->btd', cat, w_out)


def expert_ffn(xb, w_g, b_g, w_u, b_u, w_d, b_d):
    gate = jnp.minimum(xb @ w_g + b_g, SWIGLU_LIMIT)
    up = jnp.clip(xb @ w_u + b_u, -SWIGLU_LIMIT, SWIGLU_LIMIT)
    glu = gate * jax.nn.sigmoid(SWIGLU_ALPHA * gate)
    return ((up + 1.0) * glu) @ w_d + b_d


def moe(h, w_router, b_router, w_gate, b_gate, w_up, b_up, w_down, b_down):
    n, d = h.shape
    logits = h.astype(jnp.float32) @ w_router.astype(jnp.float32) + b_router.astype(jnp.float32)
    top_vals, top_idx = lax.top_k(logits, TOP_K)
    gates = jax.nn.softmax(top_vals, axis=-1)
    nk = n * TOP_K
    e_flat = top_idx.reshape(-1).astype(jnp.int32)
    tok_flat = jnp.repeat(jnp.arange(n, dtype=jnp.int32), TOP_K)
    g_flat = gates.reshape(-1)
    n_blocks = -(-(nk + N_EXPERTS * (EXPERT_BLOCK - 1)) // EXPERT_BLOCK)
    order = jnp.argsort(e_flat)
    e_sorted = e_flat[order]
    counts = jnp.bincount(e_flat, length=N_EXPERTS).astype(jnp.int32)
    starts = jnp.cumsum(counts) - counts
    padded = (counts + EXPERT_BLOCK - 1) // EXPERT_BLOCK * EXPERT_BLOCK
    pad_ends = jnp.cumsum(padded)
    pad_starts = pad_ends - padded
    dest = pad_starts[e_sorted] + jnp.arange(nk, dtype=jnp.int32) - starts[e_sorted]
    n_slots = n_blocks * EXPERT_BLOCK
    slot_tok = jnp.full((n_slots,), n, dtype=jnp.int32).at[dest].set(tok_flat[order])
    slot_gate = jnp.zeros((n_slots,), dtype=jnp.float32).at[dest].set(g_flat[order])
    block_start = jnp.arange(n_blocks, dtype=jnp.int32) * EXPERT_BLOCK
    block_expert = jnp.minimum(jnp.searchsorted(pad_ends, block_start, side='right'), N_EXPERTS - 1).astype(jnp.int32)
    h_pad = jnp.concatenate([h, jnp.zeros((1, d), h.dtype)], axis=0)
    xb = h_pad[slot_tok].reshape(n_blocks, EXPERT_BLOCK, d)

    def run_block(args):
        xblk, e = args
        return expert_ffn(xblk, w_gate[e], b_gate[e], w_up[e], b_up[e], w_down[e], b_down[e])

    yb = lax.map(run_block, (xb, block_expert)).reshape(n_slots, d)
    yb = yb * slot_gate[:, None].astype(yb.dtype)
    return jax.ops.segment_sum(yb, slot_tok, num_segments=n + 1)[:n]


def setup_inputs(seed: int = 0) -> dict:
    key = jax.random.key(seed)
    ks = jax.random.split(key, 24)
    f32 = jnp.float32

    def nrm(k, shape, scale):
        return jax.random.normal(k, shape, f32) * scale

    def gain(k, shape):
        return 1.0 + 0.02 * jax.random.normal(k, shape, f32)

    L = DEPTH
    return {
        'x_prompt': nrm(ks[0], (BATCH, SEQ, D_MODEL), 1.0),
        'x_sample': nrm(ks[1], (DEC_BATCH, DEC_SEQ, D_MODEL), 1.0),
        'cache_pool': nrm(ks[2], (L, DEC_BATCH, POOL_STATE, D_B), 1.0),
        'norm_mix_g': gain(ks[3], (L, D_MODEL)),
        'w_in': nrm(ks[4], (L, D_MODEL, 2 * D_A + D_B), D_MODEL ** -0.5),
        'ln_v_g': gain(ks[5], (L, D_A)),
        'ln_v_b': nrm(ks[6], (L, D_A), 0.02),
        'w_spatial': nrm(ks[7], (L, N_HEADS_A, GMLP_CHUNK, GMLP_CHUNK), GMLP_CHUNK ** -0.5),
        'b_spatial': gain(ks[8], (L, N_HEADS_A, GMLP_CHUNK)),
        'w_pool': nrm(ks[9], (L, N_POOL_GROUPS, POOL_GROUP_DIM, POOL_GROUP_DIM), POOL_GROUP_DIM ** -0.5),
        'pool_scale': gain(ks[10], (L, D_B)),
        'out_norm_a_g': gain(ks[11], (L, D_A)),
        'out_norm_b_g': gain(ks[12], (L, D_B)),
        'w_out': nrm(ks[13], (L, D_MIX, D_MODEL), D_MIX ** -0.5),
        'norm_ffn_g': gain(ks[14], (L, D_MODEL)),
        'w_router': nrm(ks[15], (L, D_MODEL, N_EXPERTS), D_MODEL ** -0.5),
        'b_router': nrm(ks[16], (L, N_EXPERTS), 0.01),
        'w_gate': nrm(ks[17], (L, N_EXPERTS, D_MODEL, D_FF), D_MODEL ** -0.5),
        'b_gate': nrm(ks[18], (L, N_EXPERTS, D_FF), 0.02),
        'w_up': nrm(ks[19], (L, N_EXPERTS, D_MODEL, D_FF), D_MODEL ** -0.5),
        'b_up': nrm(ks[20], (L, N_EXPERTS, D_FF), 0.02),
        'w_down': nrm(ks[21], (L, N_EXPERTS, D_FF, D_MODEL), D_FF ** -0.5),
        'b_down': nrm(ks[22], (L, N_EXPERTS, D_MODEL), 0.02),
        'final_norm_g': gain(ks[23], (D_MODEL,)),
    }


def reference(x_prompt, x_sample, cache_pool, norm_mix_g, w_in, ln_v_g, ln_v_b, w_spatial, b_spatial,
              w_pool, pool_scale, out_norm_a_g, out_norm_b_g, w_out, norm_ffn_g, w_router, b_router,
              w_gate, b_gate, w_up, b_up, w_down, b_down, final_norm_g):
    assert x_sample.shape[1] <= CHUNK
    xp, xs = x_prompt, x_sample
    n_p = xp.shape[0] * xp.shape[1]
    pool_p_list, pool_s_list, v_s_list = [], [], []
    for l in range(DEPTH):
        hp = rms_norm(xp, norm_mix_g[l])
        hs = rms_norm(xs, norm_mix_g[l])
        u_p, v_p, p_p = split_mixer_inputs(hp, w_in[l])
        u_s, v_s, p_s = split_mixer_inputs(hs, w_in[l])
        vn_p = layer_norm(v_p, ln_v_g[l], ln_v_b[l])
        vn_s = layer_norm(v_s, ln_v_g[l], ln_v_b[l])
        a_p = spatial_gate_prompt(u_p, vn_p, w_spatial[l], b_spatial[l])
        a_s = spatial_gate_sample(u_s, vn_s, w_spatial[l], b_spatial[l])
        b_p = pool_mixer(p_p, 0, 0, w_pool[l], pool_scale[l])
        p_s_ext = jnp.concatenate([cache_pool[l].astype(p_s.dtype), p_s], axis=1)
        b_s = pool_mixer(p_s_ext, POOL_STATE, PAST_LEN, w_pool[l], pool_scale[l])
        xp = xp + merge_groups(a_p, b_p, out_norm_a_g[l], out_norm_b_g[l], w_out[l])
        xs = xs + merge_groups(a_s, b_s, out_norm_a_g[l], out_norm_b_g[l], w_out[l])
        pool_p_list.append(p_p[:, -POOL_STATE:])
        pool_s_list.append(p_s_ext[:, -POOL_STATE:])
        v_s_list.append(vn_s)
        h2 = jnp.concatenate([rms_norm(xp, norm_ffn_g[l]).reshape(-1, D_MODEL),
                              rms_norm(xs, norm_ffn_g[l]).reshape(-1, D_MODEL)], axis=0)
        y2 = moe(h2, w_router[l], b_router[l], w_gate[l], b_gate[l], w_up[l], b_up[l], w_down[l], b_down[l])
        xp = xp + y2[:n_p].reshape(xp.shape)
        xs = xs + y2[n_p:].reshape(xs.shape)
    y_prompt = rms_norm(xp, final_norm_g)
    y_sample = rms_norm(xs, final_norm_g)
    state_pool_prompt = jnp.stack(pool_p_list, axis=0)
    state_pool_sample = jnp.stack(pool_s_list, axis=0)
    state_chunk_v_sample = jnp.stack(v_s_list, axis=0)
    return (y_prompt, y_sample, state_pool_prompt, state_pool_sample, state_chunk_v_sample)
```

```python
import functools

import jax
import jax.numpy as jnp
from jax import lax
from jax.experimental import pallas as pl
from jax.experimental.pallas import tpu as pltpu

D_MODEL = 2048
D_A = 1024
D_B = 1024
GMLP_CHUNK = 128
N_HEADS_A = 8
HEAD_DIM_A = D_A // N_HEADS_A
POOL_WINDOWS = (2, 4, 8, 16)
N_POOL_GROUPS = len(POOL_WINDOWS)
POOL_GROUP_DIM = D_B // N_POOL_GROUPS
POOL_MAX = 16
POOL_STATE = POOL_MAX - 1
PAST_LEN = 1024
N_EXPERTS = 32
TOP_K = 4
D_FF = 2048
SWIGLU_LIMIT = 7.0
SWIGLU_ALPHA = 1.702
EPS = 1e-5

LANES = 128
TM_PROMPT = 256
SLOT_BLOCK = 256
FF_TILE = 1024
OUT_TILE = 1024
TM_COMBINE = 128
VMEM_LIMIT = 56 * 1024 * 1024

BF16 = jnp.bfloat16
F32 = jnp.float32


def _dot(a, b):
    return jnp.dot(a, b, preferred_element_type=F32)


def _rms_norm(x, g):
    return x * lax.rsqrt(jnp.mean(x * x, axis=-1, keepdims=True) + EPS) * g


def _layer_norm(x, g, b):
    mu = jnp.mean(x, axis=-1, keepdims=True)
    xc = x - mu
    return xc * lax.rsqrt(jnp.mean(xc * xc, axis=-1, keepdims=True) + EPS) * g + b


def _gelu(x):
    return 0.5 * x * (1.0 + lax.erf(x * (2.0 ** -0.5)))


def _mixer_front(x, g_mix, w_in, ln_g, ln_b):
    h = _rms_norm(x, g_mix[...]).astype(BF16)
    u = _gelu(_dot(h, w_in[:, 0:D_A]))
    vn = _layer_norm(_dot(h, w_in[:, D_A:2 * D_A]), ln_g[...], ln_b[...])
    p = _dot(h, w_in[:, 2 * D_A:2 * D_A + D_B])
    return u, vn, p


def _spatial_gate(u, vn, ws, bs_t, chunk, a_ref):
    rows = u.shape[0]
    t = lax.broadcasted_iota(jnp.int32, (GMLP_CHUNK, GMLP_CHUNK), 0)
    s = lax.broadcasted_iota(jnp.int32, (GMLP_CHUNK, GMLP_CHUNK), 1)
    mask = (s <= t) & ((t // chunk) == (s // chunk))
    vb = vn.astype(BF16)
    for h in range(N_HEADS_A):
        w = jnp.where(mask, ws[h], 0.0).astype(BF16)
        bias = bs_t[:, h:h + 1]
        c0 = h * HEAD_DIM_A
        for r0 in range(0, rows, GMLP_CHUNK):
            mixed = _dot(w, vb[r0:r0 + GMLP_CHUNK, c0:c0 + HEAD_DIM_A]) + bias
            a_ref[r0:r0 + GMLP_CHUNK, c0:c0 + HEAD_DIM_A] = u[r0:r0 + GMLP_CHUNK, c0:c0 + HEAD_DIM_A] * mixed


def _window_sums(e):
    parts = []
    s = e
    for g, w in enumerate(POOL_WINDOWS):
        s = s + pltpu.roll(s, w // 2, 0)
        parts.append(s[:, 0:POOL_GROUP_DIM])
        if g + 1 < N_POOL_GROUPS:
            s = s[:, POOL_GROUP_DIM:]
    return parts


def _pool_project(diffs, w_pool, pool_scale, b_ref):
    for g in range(N_POOL_GROUPS):
        c0 = g * POOL_GROUP_DIM
        out = _dot(diffs[g].astype(BF16), w_pool[g])
        b_ref[:, c0:c0 + POOL_GROUP_DIM] = out * pool_scale[:, c0:c0 + POOL_GROUP_DIM]


def _mixer_back(x, a_ref, b_ref, g_a, g_b, w_out, g_ffn):
    na = _rms_norm(a_ref[...], g_a[...]).astype(BF16)
    nb = _rms_norm(b_ref[...], g_b[...]).astype(BF16)
    x1 = x + _dot(na, w_out[0:D_A, :]) + _dot(nb, w_out[D_A:D_A + D_B, :])
    hb = _rms_norm(x1, g_ffn[...]).astype(BF16)
    return x1, hb


def _pack_bf16_pairs(hb):
    half = D_MODEL // 2
    lo = lax.bitcast_convert_type(hb[:, 0:half].astype(F32), jnp.int32)
    hi = lax.bitcast_convert_type(hb[:, half:D_MODEL].astype(F32), jnp.int32)
    return lax.shift_right_logical(lo, 16) | hi


def _route(hb, w_router, b_router, run_ref):
    rows = hb.shape[0]
    logits = _dot(hb, w_router[...]) + b_router[...]
    lane = lax.broadcasted_iota(jnp.int32, (rows, N_EXPERTS), 1)
    vals, idxs = [], []
    l = logits
    for _ in range(TOP_K):
        m = jnp.max(l, axis=-1, keepdims=True)
        i = jnp.min(jnp.where(l == m, lane, N_EXPERTS), axis=-1, keepdims=True)
        vals.append(m)
        idxs.append(i)
        l = jnp.where(lane == i, -jnp.inf, l)
    exps = [jnp.exp(v - vals[0]) for v in vals]
    denom = exps[0] + exps[1] + exps[2] + exps[3]
    gates = [e / denom for e in exps]

    onehot = jnp.zeros((rows, N_EXPERTS), F32)
    for i in idxs:
        onehot = jnp.where(lane == i, 1.0, onehot)
    r = lax.broadcasted_iota(jnp.int32, (rows, rows), 0)
    c = lax.broadcasted_iota(jnp.int32, (rows, rows), 1)
    before = jnp.where(c < r, 1.0, 0.0).astype(BF16)
    base = _dot(before, onehot.astype(BF16)) + run_ref[0:1, 0:N_EXPERTS]
    ranks = [jnp.sum(jnp.where(lane == i, base, 0.0), axis=-1, keepdims=True) for i in idxs]
    run_ref[0:1, 0:N_EXPERTS] = run_ref[0:1, 0:N_EXPERTS] + jnp.sum(onehot, axis=0, keepdims=True)

    lane_m = lax.broadcasted_iota(jnp.int32, (rows, LANES), 1)
    meta = jnp.zeros((rows, LANES), F32)
    cols = [i.astype(F32) for i in idxs] + gates + ranks
    for j, col in enumerate(cols):
        meta = jnp.where(lane_m == j, col, meta)
    return meta


def _mixer_prompt_kernel(x_ref, g_mix, w_in, ln_g, ln_b, ws, bs_t, w_pool, pool_scale, g_a, g_b, w_out,
                         g_ffn, w_router, b_router,
                         x1_ref, hp_ref, meta_ref, ptail_ref, cnt_ref,
                         halo_ref, run_ref, a_ref, b_ref):
    tm = x_ref.shape[1]
    j = pl.program_id(1)

    @pl.when((pl.program_id(0) == 0) & (j == 0))
    def _():
        run_ref[...] = jnp.zeros_like(run_ref)

    @pl.when(j == 0)
    def _():
        halo_ref[...] = jnp.zeros_like(halo_ref)

    x = x_ref[0]
    u, vn, p = _mixer_front(x, g_mix, w_in, ln_g, ln_b)
    _spatial_gate(u, vn, ws, bs_t, GMLP_CHUNK, a_ref)

    e = jnp.concatenate([halo_ref[...], p], axis=0)
    halo_ref[...] = p[tm - POOL_MAX:tm]
    ptail_ref[0] = p[tm - POOL_MAX:tm]
    pos = j * tm + lax.broadcasted_iota(jnp.int32, (tm, 1), 0)
    sums = _window_sums(e)
    diffs = []
    for g, w in enumerate(POOL_WINDOWS):
        inv_cnt = 1.0 / jnp.minimum(pos + 1, w).astype(F32)
        c0 = g * POOL_GROUP_DIM
        diffs.append(sums[g][POOL_MAX:] * inv_cnt - p[:, c0:c0 + POOL_GROUP_DIM])
    _pool_project(diffs, w_pool, pool_scale, b_ref)

    x1, hb = _mixer_back(x, a_ref, b_ref, g_a, g_b, w_out, g_ffn)
    x1_ref[0] = x1
    hp_ref[...] = _pack_bf16_pairs(hb)
    meta_ref[...] = _route(hb, w_router, b_router, run_ref)
    cnt_ref[...] = run_ref[...]


def _mixer_sample_kernel(x_ref, hist_ref, cnt_in_ref, g_mix, w_in, ln_g, ln_b, ws, bs_t, w_pool, pool_scale,
                         g_a, g_b, w_out, g_ffn, w_router, b_router,
                         x1_ref, hp_ref, meta_ref, vn_ref, p_ref, cnt_ref,
                         run_ref, a_ref, b_ref, ext_ref):
    n_streams, hist_rows, _ = hist_ref.shape
    t_new = x_ref.shape[0] // n_streams
    group = hist_rows + t_new
    run_ref[...] = cnt_in_ref[...]

    x = x_ref[...]
    u, vn, p = _mixer_front(x, g_mix, w_in, ln_g, ln_b)
    vn_ref[...] = vn
    p_ref[...] = p
    _spatial_gate(u, vn, ws, bs_t, t_new, a_ref)

    for b in range(n_streams):
        ext_ref[b * group:b * group + hist_rows, :] = hist_ref[b]
        ext_ref[b * group + hist_rows:(b + 1) * group, :] = p[b * t_new:(b + 1) * t_new]
    e = ext_ref[...]
    sums = _window_sums(e)
    diffs = []
    for g, w in enumerate(POOL_WINDOWS):
        c0 = g * POOL_GROUP_DIM
        d = sums[g] * (1.0 / w) - e[:, c0:c0 + POOL_GROUP_DIM]
        diffs.append(jnp.concatenate(
            [d[b * group + hist_rows:(b + 1) * group] for b in range(n_streams)], axis=0))
    _pool_project(diffs, w_pool, pool_scale, b_ref)

    x1, hb = _mixer_back(x, a_ref, b_ref, g_a, g_b, w_out, g_ffn)
    x1_ref[...] = x1
    hp_ref[...] = _pack_bf16_pairs(hb)
    meta_ref[...] = _route(hb, w_router, b_router, run_ref)
    cnt_ref[...] = run_ref[...]


def _const_spec(shape):
    zeros = (0,) * len(shape)
    return pl.BlockSpec(shape, lambda *_: zeros, pipeline_mode=pl.Buffered(1))


def _mixer_weight_specs():
    return [
        _const_spec((1, D_MODEL)),
        _const_spec((D_MODEL, 2 * D_A + D_B)),
        _const_spec((1, D_A)), _const_spec((1, D_A)),
        _const_spec((N_HEADS_A, GMLP_CHUNK, GMLP_CHUNK)),
        _const_spec((GMLP_CHUNK, N_HEADS_A)),
        _const_spec((N_POOL_GROUPS, POOL_GROUP_DIM, POOL_GROUP_DIM)),
        _const_spec((1, D_B)),
        _const_spec((1, D_A)), _const_spec((1, D_B)),
        _const_spec((D_A + D_B, D_MODEL)),
        _const_spec((1, D_MODEL)),
        _const_spec((D_MODEL, N_EXPERTS)),
        _const_spec((1, N_EXPERTS)),
    ]


def _mixer_prompt(x, weights):
    bsz, seq, _ = x.shape
    tm = TM_PROMPT
    nt = seq // tm
    return pl.pallas_call(
        _mixer_prompt_kernel,
        name="mixer_prompt",
        grid=(bsz, nt),
        in_specs=[pl.BlockSpec((1, tm, D_MODEL), lambda b, j: (b, j, 0))] + _mixer_weight_specs(),
        out_specs=[
            pl.BlockSpec((1, tm, D_MODEL), lambda b, j: (b, j, 0)),
            pl.BlockSpec((tm, D_MODEL // 2), lambda b, j: (b * nt + j, 0)),
            pl.BlockSpec((tm, LANES), lambda b, j: (b * nt + j, 0)),
            pl.BlockSpec((1, POOL_MAX, D_B), lambda b, j: (b, 0, 0)),
            pl.BlockSpec((8, LANES), lambda b, j: (0, 0)),
        ],
        out_shape=[
            jax.ShapeDtypeStruct((bsz, seq, D_MODEL), F32),
            jax.ShapeDtypeStruct((bsz * seq, D_MODEL // 2), jnp.int32),
            jax.ShapeDtypeStruct((bsz * seq, LANES), F32),
            jax.ShapeDtypeStruct((bsz, POOL_MAX, D_B), F32),
            jax.ShapeDtypeStruct((8, LANES), F32),
        ],
        scratch_shapes=[
            pltpu.VMEM((POOL_MAX, D_B), F32),
            pltpu.VMEM((8, LANES), F32),
            pltpu.VMEM((tm, D_A), F32),
            pltpu.VMEM((tm, D_B), F32),
        ],
        compiler_params=pltpu.CompilerParams(
            dimension_semantics=("arbitrary", "arbitrary"), vmem_limit_bytes=VMEM_LIMIT),
    )(x, *weights)


def _mixer_sample(x2d, hist, cnt_in, weights):
    rows = x2d.shape[0]
    n_streams, hist_rows, _ = hist.shape
    ext_rows = rows + n_streams * hist_rows
    full = lambda shape: pl.BlockSpec(shape, lambda i: (0,) * len(shape))
    return pl.pallas_call(
        _mixer_sample_kernel,
        name="mixer_sample",
        grid=(1,),
        in_specs=[full((rows, D_MODEL)), full(hist.shape), full((8, LANES))] + _mixer_weight_specs(),
        out_specs=[
            full((rows, D_MODEL)), full((rows, D_MODEL // 2)), full((rows, LANES)),
            full((rows, D_A)), full((rows, D_B)), full((8, LANES)),
        ],
        out_shape=[
            jax.ShapeDtypeStruct((rows, D_MODEL), F32),
            jax.ShapeDtypeStruct((rows, D_MODEL // 2), jnp.int32),
            jax.ShapeDtypeStruct((rows, LANES), F32),
            jax.ShapeDtypeStruct((rows, D_A), F32),
            jax.ShapeDtypeStruct((rows, D_B), F32),
            jax.ShapeDtypeStruct((8, LANES), F32),
        ],
        scratch_shapes=[
            pltpu.VMEM((8, LANES), F32),
            pltpu.VMEM((rows, D_A), F32),
            pltpu.VMEM((rows, D_B), F32),
            pltpu.VMEM((ext_rows, D_B), F32),
        ],
        compiler_params=pltpu.CompilerParams(
            dimension_semantics=("arbitrary",), vmem_limit_bytes=VMEM_LIMIT),
    )(x2d, hist, cnt_in, *weights)


def _dispatch_kernel(tok_ref, nused_ref, hp_hbm, xs_ref, raw, sem):
    i = pl.program_id(0)
    nb = pl.num_programs(0)
    n_used = nused_ref[0]

    def gather(block, slot):
        def body(r, carry):
            t = tok_ref[block * SLOT_BLOCK + r]
            pltpu.make_async_copy(hp_hbm.at[pl.ds(t, 1)], raw.at[slot, pl.ds(r, 1)], sem.at[slot]).start()
            return carry
        lax.fori_loop(0, SLOT_BLOCK, body, 0)

    @pl.when((i == 0) & (n_used > 0))
    def _():
        gather(0, 0)

    @pl.when((i + 1 < nb) & (i + 1 < n_used))
    def _():
        gather(i + 1, (i + 1) % 2)

    @pl.when(i < n_used)
    def _():
        slot = i % 2
        pltpu.make_async_copy(hp_hbm.at[pl.ds(0, SLOT_BLOCK)], raw.at[slot], sem.at[slot]).wait()
        w = raw[slot]
        half = D_MODEL // 2
        xs_ref[:, 0:half] = lax.bitcast_convert_type(lax.shift_left(w, 16), F32).astype(BF16)
        xs_ref[:, half:D_MODEL] = lax.bitcast_convert_type(w & jnp.int32(-65536), F32).astype(BF16)

    @pl.when(i >= n_used)
    def _():
        xs_ref[...] = jnp.zeros_like(xs_ref)


def _dispatch(slot_tok, n_used, hp, n_blocks):
    return pl.pallas_call(
        _dispatch_kernel,
        name="dispatch",
        grid_spec=pltpu.PrefetchScalarGridSpec(
            num_scalar_prefetch=2,
            grid=(n_blocks,),
            in_specs=[pl.BlockSpec(memory_space=pl.ANY)],
            out_specs=pl.BlockSpec((SLOT_BLOCK, D_MODEL), lambda i, tok, nu: (i, 0)),
            scratch_shapes=[
                pltpu.VMEM((2, SLOT_BLOCK, D_MODEL // 2), jnp.int32),
                pltpu.SemaphoreType.DMA((2,)),
            ],
        ),
        out_shape=jax.ShapeDtypeStruct((n_blocks * SLOT_BLOCK, D_MODEL), BF16),
        compiler_params=pltpu.CompilerParams(dimension_semantics=("arbitrary",)),
    )(slot_tok, n_used, hp)


def _ffn_up_kernel(be_ref, first_ref, nused_ref, x_ref, wg_ref, bg_ref, wu_ref, bu_ref, act_ref, wgb, wub):
    rb = pl.program_id(1)

    @pl.when(first_ref[rb] == 1)
    def _():
        wgb[...] = wg_ref[0].astype(BF16)
        wub[...] = wu_ref[0].astype(BF16)

    @pl.when(rb < nused_ref[0])
    def _():
        x = x_ref[...]
        gate = jnp.minimum(_dot(x, wgb[...]) + bg_ref[0], SWIGLU_LIMIT)
        up = jnp.clip(_dot(x, wub[...]) + bu_ref[0], -SWIGLU_LIMIT, SWIGLU_LIMIT)
        glu = gate * jax.nn.sigmoid(SWIGLU_ALPHA * gate)
        act_ref[...] = ((up + 1.0) * glu).astype(BF16)

    @pl.when(rb >= nused_ref[0])
    def _():
        act_ref[...] = jnp.zeros_like(act_ref)


def _ffn_up(block_expert, first, n_used, xs, w_gate, b_gate, w_up, b_up, n_blocks):
    nf = D_FF // FF_TILE
    w_spec = pl.BlockSpec((1, D_MODEL, FF_TILE), lambda f, rb, be, fi, nu: (be[rb], 0, f))
    b_spec = pl.BlockSpec((1, 1, FF_TILE), lambda f, rb, be, fi, nu: (be[rb], 0, f))
    return pl.pallas_call(
        _ffn_up_kernel,
        name="ffn_up",
        grid_spec=pltpu.PrefetchScalarGridSpec(
            num_scalar_prefetch=3,
            grid=(nf, n_blocks),
            in_specs=[
                pl.BlockSpec((SLOT_BLOCK, D_MODEL), lambda f, rb, be, fi, nu: (rb, 0)),
                w_spec, b_spec, w_spec, b_spec,
            ],
            out_specs=pl.BlockSpec((SLOT_BLOCK, FF_TILE), lambda f, rb, be, fi, nu: (rb, f)),
            scratch_shapes=[pltpu.VMEM((D_MODEL, FF_TILE), BF16), pltpu.VMEM((D_MODEL, FF_TILE), BF16)],
        ),
        out_shape=jax.ShapeDtypeStruct((n_blocks * SLOT_BLOCK, D_FF), BF16),
        compiler_params=pltpu.CompilerParams(
            dimension_semantics=("arbitrary", "arbitrary"), vmem_limit_bytes=VMEM_LIMIT),
    )(block_expert, first, n_used, xs, w_gate, b_gate, w_up, b_up)


def _ffn_down_kernel(be_ref, first_ref, nused_ref, act_ref, wd_ref, bd_ref, y_ref, wdb):
    rb = pl.program_id(1)

    @pl.when(first_ref[rb] == 1)
    def _():
        wdb[...] = wd_ref[0].astype(BF16)

    @pl.when(rb < nused_ref[0])
    def _():
        y_ref[...] = _dot(act_ref[...], wdb[...]) + bd_ref[0]

    @pl.when(rb >= nused_ref[0])
    def _():
        y_ref[...] = jnp.zeros_like(y_ref)


def _ffn_down(block_expert, first, n_used, act, w_down, b_down, n_blocks):
    nt = D_MODEL // OUT_TILE
    return pl.pallas_call(
        _ffn_down_kernel,
        name="ffn_down",
        grid_spec=pltpu.PrefetchScalarGridSpec(
            num_scalar_prefetch=3,
            grid=(nt, n_blocks),
            in_specs=[
                pl.BlockSpec((SLOT_BLOCK, D_FF), lambda n, rb, be, fi, nu: (rb, 0)),
                pl.BlockSpec((1, D_FF, OUT_TILE), lambda n, rb, be, fi, nu: (be[rb], 0, n)),
                pl.BlockSpec((1, 1, OUT_TILE), lambda n, rb, be, fi, nu: (be[rb], 0, n)),
            ],
            out_specs=pl.BlockSpec((SLOT_BLOCK, OUT_TILE), lambda n, rb, be, fi, nu: (rb, n)),
            scratch_shapes=[pltpu.VMEM((D_FF, OUT_TILE), BF16)],
        ),
        out_shape=jax.ShapeDtypeStruct((n_blocks * SLOT_BLOCK, D_MODEL), F32),
        compiler_params=pltpu.CompilerParams(
            dimension_semantics=("arbitrary", "arbitrary"), vmem_limit_bytes=VMEM_LIMIT),
    )(block_expert, first, n_used, act, w_down, b_down)


def _combine_kernel(dest_ref, x1_ref, meta_ref, g_ref, ys_hbm, out_ref, ybuf, sem):
    i = pl.program_id(0)
    nt = pl.num_programs(0)
    tm = x1_ref.shape[0]

    def gather(tile, slot):
        def body(t, carry):
            for k in range(TOP_K):
                d = dest_ref[(tile * tm + t) * TOP_K + k]
                pltpu.make_async_copy(
                    ys_hbm.at[pl.ds(d, 1)], ybuf.at[slot, pl.ds(k * tm + t, 1)], sem.at[slot]).start()
            return carry
        lax.fori_loop(0, tm, body, 0)

    @pl.when(i == 0)
    def _():
        gather(0, 0)

    @pl.when(i + 1 < nt)
    def _():
        gather(i + 1, (i + 1) % 2)

    slot = i % 2
    pltpu.make_async_copy(ys_hbm.at[pl.ds(0, TOP_K * tm)], ybuf.at[slot], sem.at[slot]).wait()
    acc = x1_ref[...]
    for k in range(TOP_K):
        acc = acc + ybuf[slot, k * tm:(k + 1) * tm, :] * meta_ref[:, TOP_K + k:TOP_K + k + 1]
    out_ref[...] = _rms_norm(acc, g_ref[...])


def _combine(dest_flat, x1, meta, g_final, ys):
    n = x1.shape[0]
    tm = TM_COMBINE
    return pl.pallas_call(
        _combine_kernel,
        name="combine",
        grid_spec=pltpu.PrefetchScalarGridSpec(
            num_scalar_prefetch=1,
            grid=(n // tm,),
            in_specs=[
                pl.BlockSpec((tm, D_MODEL), lambda i, d: (i, 0)),
                pl.BlockSpec((tm, LANES), lambda i, d: (i, 0)),
                pl.BlockSpec((1, D_MODEL), lambda i, d: (0, 0)),
                pl.BlockSpec(memory_space=pl.ANY),
            ],
            out_specs=pl.BlockSpec((tm, D_MODEL), lambda i, d: (i, 0)),
            scratch_shapes=[
                pltpu.VMEM((2, TOP_K * tm, D_MODEL), F32),
                pltpu.SemaphoreType.DMA((2,)),
            ],
        ),
        out_shape=jax.ShapeDtypeStruct((n, D_MODEL), F32),
        compiler_params=pltpu.CompilerParams(
            dimension_semantics=("arbitrary",), vmem_limit_bytes=VMEM_LIMIT),
    )(dest_flat, x1, meta, g_final, ys)


def kernel(x_prompt, x_sample, cache_pool, norm_mix_g, w_in, ln_v_g, ln_v_b, w_spatial, b_spatial, w_pool,
           pool_scale, out_norm_a_g, out_norm_b_g, w_out, norm_ffn_g, w_router, b_router, w_gate, b_gate,
           w_up, b_up, w_down, b_down, final_norm_g):
    depth = norm_mix_g.shape[0]
    assert depth == 1
    bsz, seq, _ = x_prompt.shape
    dec_b, dec_t, _ = x_sample.shape
    assert seq % TM_PROMPT == 0 and TM_PROMPT % GMLP_CHUNK == 0
    assert GMLP_CHUNK % dec_t == 0 and (dec_b * dec_t) % GMLP_CHUNK == 0 and PAST_LEN % GMLP_CHUNK == 0
    assert dec_t >= POOL_STATE and PAST_LEN + 1 >= POOL_MAX
    n_p = bsz * seq
    n_s = dec_b * dec_t
    n = n_p + n_s
    l = 0

    row = lambda v: v.reshape(1, -1)
    shared = [
        row(norm_mix_g[l]), w_in[l].astype(BF16), row(ln_v_g[l]), row(ln_v_b[l]),
    ]
    tail = [
        w_pool[l].astype(BF16), row(pool_scale[l]), row(out_norm_a_g[l]), row(out_norm_b_g[l]),
        w_out[l].astype(BF16), row(norm_ffn_g[l]), w_router[l].astype(BF16), row(b_router[l]),
    ]
    weights_p = shared + [w_spatial[l], b_spatial[l].T] + tail
    reps = GMLP_CHUNK // dec_t
    weights_s = shared + [
        jnp.tile(w_spatial[l][:, :dec_t, :dec_t], (1, reps, reps)),
        jnp.tile(b_spatial[l][:, :dec_t].T, (reps, 1)),
    ] + tail

    x1_p, hp_p, meta_p, ptail, cnt_p = _mixer_prompt(x_prompt, weights_p)
    hist = jnp.pad(cache_pool[l], ((0, 0), (POOL_MAX - POOL_STATE, 0), (0, 0)))
    x1_s, hp_s, meta_s, vn_s, p_s, cnt = _mixer_sample(
        x_sample.reshape(n_s, D_MODEL), hist, cnt_p, weights_s)

    hp = jnp.concatenate([hp_p, hp_s], axis=0)
    meta = jnp.concatenate([meta_p, meta_s], axis=0)
    idx = meta[:, 0:TOP_K].astype(jnp.int32)
    rank = meta[:, 2 * TOP_K:3 * TOP_K].astype(jnp.int32)
    counts = cnt[0, :N_EXPERTS].astype(jnp.int32)
    n_blocks = -(-(n * TOP_K + N_EXPERTS * (SLOT_BLOCK - 1)) // SLOT_BLOCK)
    blocks_e = (counts + SLOT_BLOCK - 1) // SLOT_BLOCK
    blk_end = jnp.cumsum(blocks_e)
    pad_start = (blk_end - blocks_e) * SLOT_BLOCK
    n_used = blk_end[-1]
    dest = (pad_start[idx] + rank).reshape(-1)
    tok = jnp.repeat(jnp.arange(n, dtype=jnp.int32), TOP_K)
    slot_tok = jnp.zeros((n_blocks * SLOT_BLOCK,), jnp.int32).at[dest].set(tok)
    blk = jnp.minimum(jnp.arange(n_blocks, dtype=jnp.int32), n_used - 1)
    block_expert = jnp.minimum(
        jnp.searchsorted(blk_end, blk, side='right'), N_EXPERTS - 1).astype(jnp.int32)
    prev = jnp.concatenate([jnp.full((1,), -1, jnp.int32), block_expert[:-1]])
    first = ((block_expert != prev) & (jnp.arange(n_blocks) < n_used)).astype(jnp.int32)
    n_used_arr = n_used.reshape(1).astype(jnp.int32)

    xs = _dispatch(slot_tok, n_used_arr, hp, n_blocks)
    act = _ffn_up(block_expert, first, n_used_arr, xs, w_gate[l], b_gate[l].reshape(N_EXPERTS, 1, D_FF),
                  w_up[l], b_up[l].reshape(N_EXPERTS, 1, D_FF), n_blocks)
    ys = _ffn_down(block_expert, first, n_used_arr, act, w_down[l], b_down[l].reshape(N_EXPERTS, 1, D_MODEL),
                   n_blocks)

    g_final = row(final_norm_g)
    y_p = _combine(dest[:n_p * TOP_K], x1_p.reshape(n_p, D_MODEL), meta_p, g_final, ys)
    y_s = _combine(dest[n_p * TOP_K:], x1_s, meta_s, g_final, ys)

    y_prompt = y_p.reshape(bsz, seq, D_MODEL)
    y_sample = y_s.reshape(dec_b, dec_t, D_MODEL)
    state_pool_prompt = ptail[:, POOL_MAX - POOL_STATE:][None]
    p_s3 = p_s.reshape(dec_b, dec_t, D_B)
    state_pool_sample = p_s3[:, dec_t - POOL_STATE:][None]
    state_chunk_v_sample = vn_s.reshape(dec_b, dec_t, D_A)[None]
    return (y_prompt, y_sample, state_pool_prompt, state_pool_sample, state_chunk_v_sample)
```

```python
import functools

import jax
import jax.numpy as jnp
from jax import lax
from jax.experimental import pallas as pl
from jax.experimental.pallas import tpu as pltpu

D_MODEL = 2048
D_A = 1024
D_B = 1024
GMLP_CHUNK = 128
N_HEADS_A = 8
HEAD_DIM_A = D_A // N_HEADS_A
POOL_WINDOWS = (2, 4, 8, 16)
N_POOL_GROUPS = len(POOL_WINDOWS)
POOL_GROUP_DIM = D_B // N_POOL_GROUPS
POOL_MAX = 16
POOL_STATE = POOL_MAX - 1
PAST_LEN = 1024
N_EXPERTS = 32
TOP_K = 4
D_FF = 2048
SWIGLU_LIMIT = 7.0
SWIGLU_ALPHA = 1.702
EPS = 1e-5

LANES = 128
TM_PROMPT = 256
SLOT_BLOCK = 512
SUB_BLOCK = 256
TM_DISPATCH = 256
FF_TILE = 1024
OUT_TILE = 1024
TM_COMBINE = 128
VMEM_LIMIT = 56 * 1024 * 1024

BF16 = jnp.bfloat16
F32 = jnp.float32


def _dot(a, b):
    return jnp.dot(a, b, preferred_element_type=F32)


def _rms_norm(x, g):
    return x * lax.rsqrt(jnp.mean(x * x, axis=-1, keepdims=True) + EPS) * g


def _layer_norm(x, g, b):
    mu = jnp.mean(x, axis=-1, keepdims=True)
    xc = x - mu
    return xc * lax.rsqrt(jnp.mean(xc * xc, axis=-1, keepdims=True) + EPS) * g + b


def _gelu(x):
    return 0.5 * x * (1.0 + lax.erf(x * (2.0 ** -0.5)))


def _mixer_front(x, g_mix, w_in, ln_g, ln_b):
    h = _rms_norm(x, g_mix[...]).astype(BF16)
    u = _gelu(_dot(h, w_in[:, 0:D_A]))
    vn = _layer_norm(_dot(h, w_in[:, D_A:2 * D_A]), ln_g[...], ln_b[...])
    p = _dot(h, w_in[:, 2 * D_A:2 * D_A + D_B])
    return u, vn, p


def _spatial_gate(u, vn, ws, bs_t, chunk, a_ref):
    rows = u.shape[0]
    t = lax.broadcasted_iota(jnp.int32, (GMLP_CHUNK, GMLP_CHUNK), 0)
    s = lax.broadcasted_iota(jnp.int32, (GMLP_CHUNK, GMLP_CHUNK), 1)
    mask = (s <= t) & ((t // chunk) == (s // chunk))
    vb = vn.astype(BF16)
    for h in range(N_HEADS_A):
        w = jnp.where(mask, ws[h], 0.0).astype(BF16)
        bias = bs_t[:, h:h + 1]
        c0 = h * HEAD_DIM_A
        for r0 in range(0, rows, GMLP_CHUNK):
            mixed = _dot(w, vb[r0:r0 + GMLP_CHUNK, c0:c0 + HEAD_DIM_A]) + bias
            a_ref[r0:r0 + GMLP_CHUNK, c0:c0 + HEAD_DIM_A] = u[r0:r0 + GMLP_CHUNK, c0:c0 + HEAD_DIM_A] * mixed


def _window_sums(e):
    parts = []
    s = e
    for g, w in enumerate(POOL_WINDOWS):
        s = s + pltpu.roll(s, w // 2, 0)
        parts.append(s[:, 0:POOL_GROUP_DIM])
        if g + 1 < N_POOL_GROUPS:
            s = s[:, POOL_GROUP_DIM:]
    return parts


def _pool_project(diffs, w_pool, pool_scale, b_ref):
    for g in range(N_POOL_GROUPS):
        c0 = g * POOL_GROUP_DIM
        out = _dot(diffs[g].astype(BF16), w_pool[g])
        b_ref[:, c0:c0 + POOL_GROUP_DIM] = out * pool_scale[:, c0:c0 + POOL_GROUP_DIM]


def _mixer_back(x, a_ref, b_ref, g_a, g_b, w_out, g_ffn):
    na = _rms_norm(a_ref[...], g_a[...]).astype(BF16)
    nb = _rms_norm(b_ref[...], g_b[...]).astype(BF16)
    x1 = x + _dot(na, w_out[0:D_A, :]) + _dot(nb, w_out[D_A:D_A + D_B, :])
    hb = _rms_norm(x1, g_ffn[...]).astype(BF16)
    return x1, hb


def _pack_bf16_pairs(hb):
    half = D_MODEL // 2
    lo = lax.bitcast_convert_type(hb[:, 0:half].astype(F32), jnp.int32)
    hi = lax.bitcast_convert_type(hb[:, half:D_MODEL].astype(F32), jnp.int32)
    return lax.shift_right_logical(lo, 16) | hi


def _route(hb, w_router, b_router, run_ref):
    rows = hb.shape[0]
    logits = _dot(hb, w_router[...]) + b_router[...]
    lane = lax.broadcasted_iota(jnp.int32, (rows, N_EXPERTS), 1)
    vals, idxs = [], []
    l = logits
    for _ in range(TOP_K):
        m = jnp.max(l, axis=-1, keepdims=True)
        i = jnp.min(jnp.where(l == m, lane, N_EXPERTS), axis=-1, keepdims=True)
        vals.append(m)
        idxs.append(i)
        l = jnp.where(lane == i, -jnp.inf, l)
    exps = [jnp.exp(v - vals[0]) for v in vals]
    denom = exps[0] + exps[1] + exps[2] + exps[3]
    gates = [e / denom for e in exps]

    onehot = jnp.zeros((rows, N_EXPERTS), F32)
    for i in idxs:
        onehot = jnp.where(lane == i, 1.0, onehot)
    r = lax.broadcasted_iota(jnp.int32, (rows, rows), 0)
    c = lax.broadcasted_iota(jnp.int32, (rows, rows), 1)
    before = jnp.where(c < r, 1.0, 0.0).astype(BF16)
    base = _dot(before, onehot.astype(BF16)) + run_ref[0:1, 0:N_EXPERTS]
    ranks = [jnp.sum(jnp.where(lane == i, base, 0.0), axis=-1, keepdims=True) for i in idxs]
    run_ref[0:1, 0:N_EXPERTS] = run_ref[0:1, 0:N_EXPERTS] + jnp.sum(onehot, axis=0, keepdims=True)

    lane_m = lax.broadcasted_iota(jnp.int32, (rows, LANES), 1)
    meta = jnp.zeros((rows, LANES), F32)
    cols = [i.astype(F32) for i in idxs] + gates + ranks
    for j, col in enumerate(cols):
        meta = jnp.where(lane_m == j, col, meta)
    return meta


def _mixer_prompt_kernel(x_ref, g_mix, w_in, ln_g, ln_b, ws, bs_t, w_pool, pool_scale, g_a, g_b, w_out,
                         g_ffn, w_router, b_router,
                         x1_ref, hp_ref, meta_ref, ptail_ref, cnt_ref,
                         halo_ref, run_ref, a_ref, b_ref):
    tm = x_ref.shape[1]
    j = pl.program_id(1)

    @pl.when((pl.program_id(0) == 0) & (j == 0))
    def _():
        run_ref[...] = jnp.zeros_like(run_ref)

    @pl.when(j == 0)
    def _():
        halo_ref[...] = jnp.zeros_like(halo_ref)

    x = x_ref[0]
    u, vn, p = _mixer_front(x, g_mix, w_in, ln_g, ln_b)
    _spatial_gate(u, vn, ws, bs_t, GMLP_CHUNK, a_ref)

    e = jnp.concatenate([halo_ref[...], p], axis=0)
    halo_ref[...] = p[tm - POOL_MAX:tm]
    ptail_ref[0] = p[tm - POOL_MAX:tm]
    pos = j * tm + lax.broadcasted_iota(jnp.int32, (tm, 1), 0)
    sums = _window_sums(e)
    diffs = []
    for g, w in enumerate(POOL_WINDOWS):
        inv_cnt = 1.0 / jnp.minimum(pos + 1, w).astype(F32)
        c0 = g * POOL_GROUP_DIM
        diffs.append(sums[g][POOL_MAX:] * inv_cnt - p[:, c0:c0 + POOL_GROUP_DIM])
    _pool_project(diffs, w_pool, pool_scale, b_ref)

    x1, hb = _mixer_back(x, a_ref, b_ref, g_a, g_b, w_out, g_ffn)
    x1_ref[0] = x1
    hp_ref[...] = _pack_bf16_pairs(hb)
    meta_ref[...] = _route(hb, w_router, b_router, run_ref)
    cnt_ref[...] = run_ref[...]


def _mixer_sample_kernel(x_ref, hist_ref, cnt_in_ref, g_mix, w_in, ln_g, ln_b, ws, bs_t, w_pool, pool_scale,
                         g_a, g_b, w_out, g_ffn, w_router, b_router,
                         x1_ref, hp_ref, meta_ref, vn_ref, p_ref, cnt_ref,
                         run_ref, a_ref, b_ref, ext_ref):
    n_streams, hist_rows, _ = hist_ref.shape
    t_new = x_ref.shape[0] // n_streams
    group = hist_rows + t_new
    run_ref[...] = cnt_in_ref[...]

    x = x_ref[...]
    u, vn, p = _mixer_front(x, g_mix, w_in, ln_g, ln_b)
    vn_ref[...] = vn
    p_ref[...] = p
    _spatial_gate(u, vn, ws, bs_t, t_new, a_ref)

    for b in range(n_streams):
        ext_ref[b * group:b * group + hist_rows, :] = hist_ref[b]
        ext_ref[b * group + hist_rows:(b + 1) * group, :] = p[b * t_new:(b + 1) * t_new]
    e = ext_ref[...]
    sums = _window_sums(e)
    diffs = []
    for g, w in enumerate(POOL_WINDOWS):
        c0 = g * POOL_GROUP_DIM
        d = sums[g] * (1.0 / w) - e[:, c0:c0 + POOL_GROUP_DIM]
        diffs.append(jnp.concatenate(
            [d[b * group + hist_rows:(b + 1) * group] for b in range(n_streams)], axis=0))
    _pool_project(diffs, w_pool, pool_scale, b_ref)

    x1, hb = _mixer_back(x, a_ref, b_ref, g_a, g_b, w_out, g_ffn)
    x1_ref[...] = x1
    hp_ref[...] = _pack_bf16_pairs(hb)
    meta_ref[...] = _route(hb, w_router, b_router, run_ref)
    cnt_ref[...] = run_ref[...]


def _const_spec(shape):
    zeros = (0,) * len(shape)
    return pl.BlockSpec(shape, lambda *_: zeros, pipeline_mode=pl.Buffered(1))


def _mixer_weight_specs():
    return [
        _const_spec((1, D_MODEL)),
        _const_spec((D_MODEL, 2 * D_A + D_B)),
        _const_spec((1, D_A)), _const_spec((1, D_A)),
        _const_spec((N_HEADS_A, GMLP_CHUNK, GMLP_CHUNK)),
        _const_spec((GMLP_CHUNK, N_HEADS_A)),
        _const_spec((N_POOL_GROUPS, POOL_GROUP_DIM, POOL_GROUP_DIM)),
        _const_spec((1, D_B)),
        _const_spec((1, D_A)), _const_spec((1, D_B)),
        _const_spec((D_A + D_B, D_MODEL)),
        _const_spec((1, D_MODEL)),
        _const_spec((D_MODEL, N_EXPERTS)),
        _const_spec((1, N_EXPERTS)),
    ]


def _mixer_prompt(x, weights):
    bsz, seq, _ = x.shape
    tm = TM_PROMPT
    nt = seq // tm
    return pl.pallas_call(
        _mixer_prompt_kernel,
        name="mixer_prompt",
        grid=(bsz, nt),
        in_specs=[pl.BlockSpec((1, tm, D_MODEL), lambda b, j: (b, j, 0))] + _mixer_weight_specs(),
        out_specs=[
            pl.BlockSpec((1, tm, D_MODEL), lambda b, j: (b, j, 0)),
            pl.BlockSpec((tm, D_MODEL // 2), lambda b, j: (b * nt + j, 0)),
            pl.BlockSpec((tm, LANES), lambda b, j: (b * nt + j, 0)),
            pl.BlockSpec((1, POOL_MAX, D_B), lambda b, j: (b, 0, 0)),
            pl.BlockSpec((8, LANES), lambda b, j: (0, 0)),
        ],
        out_shape=[
            jax.ShapeDtypeStruct((bsz, seq, D_MODEL), F32),
            jax.ShapeDtypeStruct((bsz * seq, D_MODEL // 2), jnp.int32),
            jax.ShapeDtypeStruct((bsz * seq, LANES), F32),
            jax.ShapeDtypeStruct((bsz, POOL_MAX, D_B), F32),
            jax.ShapeDtypeStruct((8, LANES), F32),
        ],
        scratch_shapes=[
            pltpu.VMEM((POOL_MAX, D_B), F32),
            pltpu.VMEM((8, LANES), F32),
            pltpu.VMEM((tm, D_A), F32),
            pltpu.VMEM((tm, D_B), F32),
        ],
        compiler_params=pltpu.CompilerParams(
            dimension_semantics=("arbitrary", "arbitrary"), vmem_limit_bytes=VMEM_LIMIT),
    )(x, *weights)


def _mixer_sample(x2d, hist, cnt_in, weights):
    rows = x2d.shape[0]
    n_streams, hist_rows, _ = hist.shape
    ext_rows = rows + n_streams * hist_rows
    full = lambda shape: pl.BlockSpec(shape, lambda i: (0,) * len(shape))
    return pl.pallas_call(
        _mixer_sample_kernel,
        name="mixer_sample",
        grid=(1,),
        in_specs=[full((rows, D_MODEL)), full(hist.shape), full((8, LANES))] + _mixer_weight_specs(),
        out_specs=[
            full((rows, D_MODEL)), full((rows, D_MODEL // 2)), full((rows, LANES)),
            full((rows, D_A)), full((rows, D_B)), full((8, LANES)),
        ],
        out_shape=[
            jax.ShapeDtypeStruct((rows, D_MODEL), F32),
            jax.ShapeDtypeStruct((rows, D_MODEL // 2), jnp.int32),
            jax.ShapeDtypeStruct((rows, LANES), F32),
            jax.ShapeDtypeStruct((rows, D_A), F32),
            jax.ShapeDtypeStruct((rows, D_B), F32),
            jax.ShapeDtypeStruct((8, LANES), F32),
        ],
        scratch_shapes=[
            pltpu.VMEM((8, LANES), F32),
            pltpu.VMEM((rows, D_A), F32),
            pltpu.VMEM((rows, D_B), F32),
            pltpu.VMEM((ext_rows, D_B), F32),
        ],
        compiler_params=pltpu.CompilerParams(
            dimension_semantics=("arbitrary",), vmem_limit_bytes=VMEM_LIMIT),
    )(x2d, hist, cnt_in, *weights)


def _dispatch_kernel(n_prompt_tiles, dest_ref, cnt_ref, region_ref, nused_ref,
                     hp_p_ref, hp_s_ref, xs_hbm, zbuf, sem, zsem):
    i = pl.program_id(0)
    tm = hp_p_ref.shape[0]
    n_blocks = xs_hbm.shape[0] // SLOT_BLOCK

    def pad_rows(e):
        cnt = cnt_ref[e]
        nblk = (cnt + SLOT_BLOCK - 1) // SLOT_BLOCK
        lo = region_ref[e] + cnt - (nblk - 1) * SLOT_BLOCK
        hi = region_ref[e] + SLOT_BLOCK
        return lo, jnp.where(nblk > 0, hi, lo)

    def zero_row_copy(r):
        return pltpu.make_async_copy(zbuf.at[pl.ds(0, 1)], xs_hbm.at[pl.ds(r, 1)], zsem.at[0])

    def zero_block_copy(rb):
        return pltpu.make_async_copy(zbuf, xs_hbm.at[pl.ds(rb * SLOT_BLOCK, SLOT_BLOCK)], zsem.at[1])

    def for_each_fill(row_fn, block_fn):
        def per_expert(e, carry):
            lo, hi = pad_rows(e)
            lax.fori_loop(lo, hi, lambda r, c: (row_fn(r), c)[1], 0)
            return carry
        lax.fori_loop(0, N_EXPERTS, per_expert, 0)
        lax.fori_loop(nused_ref[0], n_blocks, lambda rb, c: (block_fn(rb), c)[1], 0)

    @pl.when(i == 0)
    def _():
        zbuf[...] = jnp.zeros_like(zbuf)
        for_each_fill(lambda r: zero_row_copy(r).start(), lambda rb: zero_block_copy(rb).start())

    def scatter(src_ref):
        def body(t, carry):
            base = (i * tm + t) * TOP_K
            for k in range(TOP_K):
                d = dest_ref[base + k]
                pltpu.make_async_copy(src_ref.at[pl.ds(t, 1)], xs_hbm.at[pl.ds(d, 1)], sem).start()
            return carry
        lax.fori_loop(0, tm, body, 0, unroll=8)

    @pl.when(i < n_prompt_tiles)
    def _():
        scatter(hp_p_ref)

    @pl.when(i >= n_prompt_tiles)
    def _():
        scatter(hp_s_ref)

    for _ in range(TOP_K):
        pltpu.make_async_copy(hp_p_ref, xs_hbm.at[pl.ds(0, tm)], sem).wait()

    @pl.when(i == 0)
    def _():
        for_each_fill(lambda r: zero_row_copy(r).wait(), lambda rb: zero_block_copy(rb).wait())


def _dispatch(dest, counts, region, n_used, hp_p, hp_s, n_blocks):
    tm = TM_DISPATCH
    ntp = hp_p.shape[0] // tm
    nts = hp_s.shape[0] // tm
    half = D_MODEL // 2
    return pl.pallas_call(
        functools.partial(_dispatch_kernel, ntp),
        name="dispatch",
        grid_spec=pltpu.PrefetchScalarGridSpec(
            num_scalar_prefetch=4,
            grid=(ntp + nts,),
            in_specs=[
                pl.BlockSpec((tm, half), lambda i, *_: (jnp.minimum(i, ntp - 1), 0)),
                pl.BlockSpec((tm, half), lambda i, *_: (jnp.maximum(i - ntp, 0), 0)),
            ],
            out_specs=pl.BlockSpec(memory_space=pl.ANY),
            scratch_shapes=[
                pltpu.VMEM((SLOT_BLOCK, half), jnp.int32),
                pltpu.SemaphoreType.DMA,
                pltpu.SemaphoreType.DMA((2,)),
            ],
        ),
        out_shape=jax.ShapeDtypeStruct((n_blocks * SLOT_BLOCK, half), jnp.int32),
        compiler_params=pltpu.CompilerParams(dimension_semantics=("arbitrary",)),
    )(dest, counts, region, n_used, hp_p, hp_s)


def _unpack_bf16_pairs(w):
    lo = lax.bitcast_convert_type(lax.shift_left(w, 16), F32).astype(BF16)
    hi = lax.bitcast_convert_type(w & jnp.int32(-65536), F32).astype(BF16)
    return jnp.concatenate([lo, hi], axis=1)


def _for_each_sub_block(nsub, compute, out_ref):
    for s in range(SLOT_BLOCK // SUB_BLOCK):
        rows = pl.ds(s * SUB_BLOCK, SUB_BLOCK)

        @pl.when(s < nsub)
        def _(rows=rows):
            compute(rows)

        @pl.when(s >= nsub)
        def _(rows=rows):
            out_ref[rows, :] = jnp.zeros((SUB_BLOCK, out_ref.shape[1]), out_ref.dtype)


def _ffn_up_kernel(be_ref, first_ref, nsub_ref, x_ref, wg_ref, bg_ref, wu_ref, bu_ref, act_ref, wgb, wub):
    rb = pl.program_id(1)

    @pl.when(first_ref[rb] == 1)
    def _():
        wgb[...] = wg_ref[0].astype(BF16)
        wub[...] = wu_ref[0].astype(BF16)

    def compute(rows):
        x = _unpack_bf16_pairs(x_ref[rows, :])
        gate = jnp.minimum(_dot(x, wgb[...]) + bg_ref[0], SWIGLU_LIMIT)
        up = jnp.clip(_dot(x, wub[...]) + bu_ref[0], -SWIGLU_LIMIT, SWIGLU_LIMIT)
        glu = gate * jax.nn.sigmoid(SWIGLU_ALPHA * gate)
        act_ref[rows, :] = ((up + 1.0) * glu).astype(BF16)

    _for_each_sub_block(nsub_ref[rb], compute, act_ref)


def _ffn_up(block_expert, first, nsub, xs, w_gate, b_gate, w_up, b_up, n_blocks):
    nf = D_FF // FF_TILE
    w_spec = pl.BlockSpec((1, D_MODEL, FF_TILE), lambda f, rb, be, *_: (be[rb], 0, f))
    b_spec = pl.BlockSpec((1, 1, FF_TILE), lambda f, rb, be, *_: (be[rb], 0, f))
    return pl.pallas_call(
        _ffn_up_kernel,
        name="ffn_up",
        grid_spec=pltpu.PrefetchScalarGridSpec(
            num_scalar_prefetch=3,
            grid=(nf, n_blocks),
            in_specs=[
                pl.BlockSpec((SLOT_BLOCK, D_MODEL // 2), lambda f, rb, *_: (rb, 0)),
                w_spec, b_spec, w_spec, b_spec,
            ],
            out_specs=pl.BlockSpec((SLOT_BLOCK, FF_TILE), lambda f, rb, *_: (rb, f)),
            scratch_shapes=[pltpu.VMEM((D_MODEL, FF_TILE), BF16), pltpu.VMEM((D_MODEL, FF_TILE), BF16)],
        ),
        out_shape=jax.ShapeDtypeStruct((n_blocks * SLOT_BLOCK, D_FF), BF16),
        compiler_params=pltpu.CompilerParams(
            dimension_semantics=("arbitrary", "arbitrary"), vmem_limit_bytes=VMEM_LIMIT),
    )(block_expert, first, nsub, xs, w_gate, b_gate, w_up, b_up)


def _ffn_down_kernel(be_ref, first_ref, nsub_ref, act_ref, wd_ref, bd_ref, y_ref, wdb):
    rb = pl.program_id(1)

    @pl.when(first_ref[rb] == 1)
    def _():
        wdb[...] = wd_ref[0].astype(BF16)

    def compute(rows):
        y_ref[rows, :] = _dot(act_ref[rows, :], wdb[...]) + bd_ref[0]

    _for_each_sub_block(nsub_ref[rb], compute, y_ref)


def _ffn_down(block_expert, first, nsub, act, w_down, b_down, n_blocks):
    nt = D_MODEL // OUT_TILE
    return pl.pallas_call(
        _ffn_down_kernel,
        name="ffn_down",
        grid_spec=pltpu.PrefetchScalarGridSpec(
            num_scalar_prefetch=3,
            grid=(nt, n_blocks),
            in_specs=[
                pl.BlockSpec((SLOT_BLOCK, D_FF), lambda n, rb, *_: (rb, 0)),
                pl.BlockSpec((1, D_FF, OUT_TILE), lambda n, rb, be, *_: (be[rb], 0, n)),
                pl.BlockSpec((1, 1, OUT_TILE), lambda n, rb, be, *_: (be[rb], 0, n)),
            ],
            out_specs=pl.BlockSpec((SLOT_BLOCK, OUT_TILE), lambda n, rb, *_: (rb, n)),
            scratch_shapes=[pltpu.VMEM((D_FF, OUT_TILE), BF16)],
        ),
        out_shape=jax.ShapeDtypeStruct((n_blocks * SLOT_BLOCK, D_MODEL), F32),
        compiler_params=pltpu.CompilerParams(
            dimension_semantics=("arbitrary", "arbitrary"), vmem_limit_bytes=VMEM_LIMIT),
    )(block_expert, first, nsub, act, w_down, b_down)


def _combine_kernel(dest_ref, x1_ref, meta_ref, g_ref, ys_hbm, out_ref, ybuf, sem):
    i = pl.program_id(0)
    nt = pl.num_programs(0)
    tm = x1_ref.shape[0]

    def gather(tile, slot):
        def body(t, carry):
            for k in range(TOP_K):
                d = dest_ref[(tile * tm + t) * TOP_K + k]
                pltpu.make_async_copy(
                    ys_hbm.at[pl.ds(d, 1)], ybuf.at[slot, pl.ds(k * tm + t, 1)], sem.at[slot]).start()
            return carry
        lax.fori_loop(0, tm, body, 0, unroll=8)

    @pl.when(i == 0)
    def _():
        gather(0, 0)

    @pl.when(i + 1 < nt)
    def _():
        gather(i + 1, (i + 1) % 2)

    slot = i % 2
    pltpu.make_async_copy(ys_hbm.at[pl.ds(0, TOP_K * tm)], ybuf.at[slot], sem.at[slot]).wait()
    acc = x1_ref[...]
    for k in range(TOP_K):
        acc = acc + ybuf[slot, k * tm:(k + 1) * tm, :] * meta_ref[:, TOP_K + k:TOP_K + k + 1]
    out_ref[...] = _rms_norm(acc, g_ref[...])


def _combine(dest_flat, x1, meta, g_final, ys):
    n = x1.shape[0]
    tm = TM_COMBINE
    return pl.pallas_call(
        _combine_kernel,
        name="combine",
        grid_spec=pltpu.PrefetchScalarGridSpec(
            num_scalar_prefetch=1,
            grid=(n // tm,),
            in_specs=[
                pl.BlockSpec((tm, D_MODEL), lambda i, d: (i, 0)),
                pl.BlockSpec((tm, LANES), lambda i, d: (i, 0)),
                pl.BlockSpec((1, D_MODEL), lambda i, d: (0, 0)),
                pl.BlockSpec(memory_space=pl.ANY),
            ],
            out_specs=pl.BlockSpec((tm, D_MODEL), lambda i, d: (i, 0)),
            scratch_shapes=[
                pltpu.VMEM((2, TOP_K * tm, D_MODEL), F32),
                pltpu.SemaphoreType.DMA((2,)),
            ],
        ),
        out_shape=jax.ShapeDtypeStruct((n, D_MODEL), F32),
        compiler_params=pltpu.CompilerParams(
            dimension_semantics=("arbitrary",), vmem_limit_bytes=VMEM_LIMIT),
    )(dest_flat, x1, meta, g_final, ys)


def kernel(x_prompt, x_sample, cache_pool, norm_mix_g, w_in, ln_v_g, ln_v_b, w_spatial, b_spatial, w_pool,
           pool_scale, out_norm_a_g, out_norm_b_g, w_out, norm_ffn_g, w_router, b_router, w_gate, b_gate,
           w_up, b_up, w_down, b_down, final_norm_g):
    depth = norm_mix_g.shape[0]
    assert depth == 1
    bsz, seq, _ = x_prompt.shape
    dec_b, dec_t, _ = x_sample.shape
    assert seq % TM_PROMPT == 0 and TM_PROMPT % GMLP_CHUNK == 0
    assert GMLP_CHUNK % dec_t == 0 and (dec_b * dec_t) % GMLP_CHUNK == 0 and PAST_LEN % GMLP_CHUNK == 0
    assert dec_t >= POOL_STATE and PAST_LEN + 1 >= POOL_MAX
    n_p = bsz * seq
    n_s = dec_b * dec_t
    n = n_p + n_s
    assert n_p % TM_DISPATCH == 0 and n_s % TM_DISPATCH == 0
    assert n_p % TM_COMBINE == 0 and n_s % TM_COMBINE == 0
    l = 0

    row = lambda v: v.reshape(1, -1)
    shared = [
        row(norm_mix_g[l]), w_in[l].astype(BF16), row(ln_v_g[l]), row(ln_v_b[l]),
    ]
    tail = [
        w_pool[l].astype(BF16), row(pool_scale[l]), row(out_norm_a_g[l]), row(out_norm_b_g[l]),
        w_out[l].astype(BF16), row(norm_ffn_g[l]), w_router[l].astype(BF16), row(b_router[l]),
    ]
    weights_p = shared + [w_spatial[l], b_spatial[l].T] + tail
    reps = GMLP_CHUNK // dec_t
    weights_s = shared + [
        jnp.tile(w_spatial[l][:, :dec_t, :dec_t], (1, reps, reps)),
        jnp.tile(b_spatial[l][:, :dec_t].T, (reps, 1)),
    ] + tail

    x1_p, hp_p, meta_p, ptail, cnt_p = _mixer_prompt(x_prompt, weights_p)
    hist = jnp.pad(cache_pool[l], ((0, 0), (POOL_MAX - POOL_STATE, 0), (0, 0)))
    x1_s, hp_s, meta_s, vn_s, p_s, cnt = _mixer_sample(
        x_sample.reshape(n_s, D_MODEL), hist, cnt_p, weights_s)

    i32 = jnp.int32
    counts = cnt[0, :N_EXPERTS].astype(i32)
    n_blocks = -(-(n * TOP_K + N_EXPERTS * (SLOT_BLOCK - 1)) // SLOT_BLOCK)
    blocks_e = (counts + SLOT_BLOCK - 1) // SLOT_BLOCK
    blk_end = jnp.cumsum(blocks_e)
    blk_start = blk_end - blocks_e
    region = blk_start * SLOT_BLOCK
    rows_first = counts - (blocks_e - 1) * SLOT_BLOCK
    n_used = blk_end[-1]
    experts = jnp.arange(N_EXPERTS, dtype=i32)

    def slot_index(meta_part):
        idx = meta_part[:, 0:TOP_K].astype(i32)
        rank = meta_part[:, 2 * TOP_K:3 * TOP_K].astype(i32)
        sel = idx[:, :, None] == experts
        reg = jnp.sum(jnp.where(sel, region, 0), axis=-1)
        rf = jnp.sum(jnp.where(sel, rows_first, 0), axis=-1)
        return (reg + jnp.where(rank < rf, rank, SLOT_BLOCK + rank - rf)).reshape(-1)

    dest_p = slot_index(meta_p)
    dest_s = slot_index(meta_s)
    blk = jnp.arange(n_blocks, dtype=i32)
    blk_c = jnp.minimum(blk, n_used - 1)
    block_expert = jnp.minimum(
        jnp.sum((blk_c[:, None] >= blk_end[None, :]).astype(i32), axis=-1), N_EXPERTS - 1)
    is_first = blk_c == blk_start[block_expert]
    used = blk < n_used
    first = (is_first & used).astype(i32)
    sub_first = (rows_first[block_expert] + SUB_BLOCK - 1) // SUB_BLOCK
    nsub = jnp.where(used, jnp.where(is_first, sub_first, SLOT_BLOCK // SUB_BLOCK), 0).astype(i32)

    xs = _dispatch(jnp.concatenate([dest_p, dest_s]), counts, region, n_used.reshape(1), hp_p, hp_s, n_blocks)
    act = _ffn_up(block_expert, first, nsub, xs, w_gate[l], b_gate[l].reshape(N_EXPERTS, 1, D_FF),
                  w_up[l], b_up[l].reshape(N_EXPERTS, 1, D_FF), n_blocks)
    ys = _ffn_down(block_expert, first, nsub, act, w_down[l], b_down[l].reshape(N_EXPERTS, 1, D_MODEL),
                   n_blocks)

    g_final = row(final_norm_g)
    y_p = _combine(dest_p, x1_p.reshape(n_p, D_MODEL), meta_p, g_final, ys)
    y_s = _combine(dest_s, x1_s, meta_s, g_final, ys)

    y_prompt = y_p.reshape(bsz, seq, D_MODEL)
    y_sample = y_s.reshape(dec_b, dec_t, D_MODEL)
    state_pool_prompt = ptail[:, POOL_MAX - POOL_STATE:][None]
    p_s3 = p_s.reshape(dec_b, dec_t, D_B)
    state_pool_sample = p_s3[:, dec_t - POOL_STATE:][None]
    state_chunk_v_sample = vn_s.reshape(dec_b, dec_t, D_A)[None]
    return (y_prompt, y_sample, state_pool_prompt, state_pool_sample, state_chunk_v_sample)
```

```python
import functools

import jax
import jax.numpy as jnp
from jax import lax
from jax.experimental import pallas as pl
from jax.experimental.pallas import tpu as pltpu

D_MODEL = 2048
D_A = 1024
D_B = 1024
GMLP_CHUNK = 128
N_HEADS_A = 8
HEAD_DIM_A = D_A // N_HEADS_A
POOL_WINDOWS = (2, 4, 8, 16)
N_POOL_GROUPS = len(POOL_WINDOWS)
POOL_GROUP_DIM = D_B // N_POOL_GROUPS
POOL_MAX = 16
POOL_STATE = POOL_MAX - 1
PAST_LEN = 1024
N_EXPERTS = 32
TOP_K = 4
D_FF = 2048
SWIGLU_LIMIT = 7.0
SWIGLU_ALPHA = 1.702
EPS = 1e-5

LANES = 128
TM_PROMPT = 256
SLOT_BLOCK = 512
SUB_BLOCK = 256
TM_DISPATCH = 256
FF_TILE = 1024
OUT_TILE = 1024
TM_COMBINE = 128
VMEM_LIMIT = 56 * 1024 * 1024

BF16 = jnp.bfloat16
F32 = jnp.float32


def _dot(a, b):
    return jnp.dot(a, b, preferred_element_type=F32)


def _rms_norm(x, g):
    return x * lax.rsqrt(jnp.mean(x * x, axis=-1, keepdims=True) + EPS) * g


def _layer_norm(x, g, b):
    mu = jnp.mean(x, axis=-1, keepdims=True)
    xc = x - mu
    return xc * lax.rsqrt(jnp.mean(xc * xc, axis=-1, keepdims=True) + EPS) * g + b


def _gelu(x):
    return 0.5 * x * (1.0 + lax.erf(x * (2.0 ** -0.5)))


def _mixer_front(x, g_mix, w_in, ln_g, ln_b):
    h = _rms_norm(x, g_mix[...]).astype(BF16)
    u = _gelu(_dot(h, w_in[:, 0:D_A]))
    vn = _layer_norm(_dot(h, w_in[:, D_A:2 * D_A]), ln_g[...], ln_b[...])
    p = _dot(h, w_in[:, 2 * D_A:2 * D_A + D_B])
    return u, vn, p


def _spatial_gate(u, vn, ws, bs_t, chunk, a_ref):
    rows = u.shape[0]
    t = lax.broadcasted_iota(jnp.int32, (GMLP_CHUNK, GMLP_CHUNK), 0)
    s = lax.broadcasted_iota(jnp.int32, (GMLP_CHUNK, GMLP_CHUNK), 1)
    mask = (s <= t) & ((t // chunk) == (s // chunk))
    vb = vn.astype(BF16)
    for h in range(N_HEADS_A):
        w = jnp.where(mask, ws[h], 0.0).astype(BF16)
        bias = bs_t[:, h:h + 1]
        c0 = h * HEAD_DIM_A
        for r0 in range(0, rows, GMLP_CHUNK):
            mixed = _dot(w, vb[r0:r0 + GMLP_CHUNK, c0:c0 + HEAD_DIM_A]) + bias
            a_ref[r0:r0 + GMLP_CHUNK, c0:c0 + HEAD_DIM_A] = u[r0:r0 + GMLP_CHUNK, c0:c0 + HEAD_DIM_A] * mixed


def _window_sums(e):
    parts = []
    s = e
    for g, w in enumerate(POOL_WINDOWS):
        s = s + pltpu.roll(s, w // 2, 0)
        parts.append(s[:, 0:POOL_GROUP_DIM])
        if g + 1 < N_POOL_GROUPS:
            s = s[:, POOL_GROUP_DIM:]
    return parts


def _pool_project(diffs, w_pool, pool_scale, b_ref):
    for g in range(N_POOL_GROUPS):
        c0 = g * POOL_GROUP_DIM
        out = _dot(diffs[g].astype(BF16), w_pool[g])
        b_ref[:, c0:c0 + POOL_GROUP_DIM] = out * pool_scale[:, c0:c0 + POOL_GROUP_DIM]


def _mixer_back(x, a_ref, b_ref, g_a, g_b, w_out, g_ffn):
    na = _rms_norm(a_ref[...], g_a[...]).astype(BF16)
    nb = _rms_norm(b_ref[...], g_b[...]).astype(BF16)
    x1 = x + _dot(na, w_out[0:D_A, :]) + _dot(nb, w_out[D_A:D_A + D_B, :])
    hb = _rms_norm(x1, g_ffn[...]).astype(BF16)
    return x1, hb


def _pack_bf16_pairs(hb):
    half = D_MODEL // 2
    lo = lax.bitcast_convert_type(hb[:, 0:half].astype(F32), jnp.int32)
    hi = lax.bitcast_convert_type(hb[:, half:D_MODEL].astype(F32), jnp.int32)
    return lax.shift_right_logical(lo, 16) | hi


def _route(hb, w_router, b_router, run_ref):
    rows = hb.shape[0]
    logits = _dot(hb, w_router[...]) + b_router[...]
    lane = lax.broadcasted_iota(jnp.int32, (rows, N_EXPERTS), 1)
    vals, idxs = [], []
    l = logits
    for _ in range(TOP_K):
        m = jnp.max(l, axis=-1, keepdims=True)
        i = jnp.min(jnp.where(l == m, lane, N_EXPERTS), axis=-1, keepdims=True)
        vals.append(m)
        idxs.append(i)
        l = jnp.where(lane == i, -jnp.inf, l)
    exps = [jnp.exp(v - vals[0]) for v in vals]
    denom = exps[0] + exps[1] + exps[2] + exps[3]
    gates = [e / denom for e in exps]

    onehot = jnp.zeros((rows, N_EXPERTS), F32)
    for i in idxs:
        onehot = jnp.where(lane == i, 1.0, onehot)
    r = lax.broadcasted_iota(jnp.int32, (rows, rows), 0)
    c = lax.broadcasted_iota(jnp.int32, (rows, rows), 1)
    before = jnp.where(c < r, 1.0, 0.0).astype(BF16)
    base = _dot(before, onehot.astype(BF16)) + run_ref[0:1, 0:N_EXPERTS]
    ranks = [jnp.sum(jnp.where(lane == i, base, 0.0), axis=-1, keepdims=True) for i in idxs]
    run_ref[0:1, 0:N_EXPERTS] = run_ref[0:1, 0:N_EXPERTS] + jnp.sum(onehot, axis=0, keepdims=True)

    lane_m = lax.broadcasted_iota(jnp.int32, (rows, LANES), 1)
    meta = jnp.zeros((rows, LANES), F32)
    cols = [i.astype(F32) for i in idxs] + gates + ranks
    for j, col in enumerate(cols):
        meta = jnp.where(lane_m == j, col, meta)
    return meta


def _mixer_prompt_kernel(x_ref, g_mix, w_in, ln_g, ln_b, ws, bs_t, w_pool, pool_scale, g_a, g_b, w_out,
                         g_ffn, w_router, b_router,
                         x1_ref, hp_ref, meta_ref, ptail_ref, cnt_ref,
                         halo_ref, run_ref, a_ref, b_ref):
    tm = x_ref.shape[1]
    j = pl.program_id(1)

    @pl.when((pl.program_id(0) == 0) & (j == 0))
    def _():
        run_ref[...] = jnp.zeros_like(run_ref)

    @pl.when(j == 0)
    def _():
        halo_ref[...] = jnp.zeros_like(halo_ref)

    x = x_ref[0]
    u, vn, p = _mixer_front(x, g_mix, w_in, ln_g, ln_b)
    _spatial_gate(u, vn, ws, bs_t, GMLP_CHUNK, a_ref)

    e = jnp.concatenate([halo_ref[...], p], axis=0)
    halo_ref[...] = p[tm - POOL_MAX:tm]
    ptail_ref[0] = p[tm - POOL_MAX:tm]
    pos = j * tm + lax.broadcasted_iota(jnp.int32, (tm, 1), 0)
    sums = _window_sums(e)
    diffs = []
    for g, w in enumerate(POOL_WINDOWS):
        inv_cnt = 1.0 / jnp.minimum(pos + 1, w).astype(F32)
        c0 = g * POOL_GROUP_DIM
        diffs.append(sums[g][POOL_MAX:] * inv_cnt - p[:, c0:c0 + POOL_GROUP_DIM])
    _pool_project(diffs, w_pool, pool_scale, b_ref)

    x1, hb = _mixer_back(x, a_ref, b_ref, g_a, g_b, w_out, g_ffn)
    x1_ref[0] = x1
    hp_ref[...] = _pack_bf16_pairs(hb)
    meta_ref[...] = _route(hb, w_router, b_router, run_ref)
    cnt_ref[...] = run_ref[...]


def _mixer_sample_kernel(x_ref, hist_ref, cnt_in_ref, g_mix, w_in, ln_g, ln_b, ws, bs_t, w_pool, pool_scale,
                         g_a, g_b, w_out, g_ffn, w_router, b_router,
                         x1_ref, hp_ref, meta_ref, vn_ref, p_ref, cnt_ref,
                         run_ref, a_ref, b_ref, ext_ref):
    n_streams, hist_rows, _ = hist_ref.shape
    t_new = x_ref.shape[0] // n_streams
    group = hist_rows + t_new
    run_ref[...] = cnt_in_ref[...]

    x = x_ref[...]
    u, vn, p = _mixer_front(x, g_mix, w_in, ln_g, ln_b)
    vn_ref[...] = vn
    p_ref[...] = p
    _spatial_gate(u, vn, ws, bs_t, t_new, a_ref)

    for b in range(n_streams):
        ext_ref[b * group:b * group + hist_rows, :] = hist_ref[b]
        ext_ref[b * group + hist_rows:(b + 1) * group, :] = p[b * t_new:(b + 1) * t_new]
    e = ext_ref[...]
    sums = _window_sums(e)
    diffs = []
    for g, w in enumerate(POOL_WINDOWS):
        c0 = g * POOL_GROUP_DIM
        d = sums[g] * (1.0 / w) - e[:, c0:c0 + POOL_GROUP_DIM]
        diffs.append(jnp.concatenate(
            [d[b * group + hist_rows:(b + 1) * group] for b in range(n_streams)], axis=0))
    _pool_project(diffs, w_pool, pool_scale, b_ref)

    x1, hb = _mixer_back(x, a_ref, b_ref, g_a, g_b, w_out, g_ffn)
    x1_ref[...] = x1
    hp_ref[...] = _pack_bf16_pairs(hb)
    meta_ref[...] = _route(hb, w_router, b_router, run_ref)
    cnt_ref[...] = run_ref[...]


def _const_spec(shape):
    zeros = (0,) * len(shape)
    return pl.BlockSpec(shape, lambda *_: zeros, pipeline_mode=pl.Buffered(1))


def _mixer_weight_specs():
    return [
        _const_spec((1, D_MODEL)),
        _const_spec((D_MODEL, 2 * D_A + D_B)),
        _const_spec((1, D_A)), _const_spec((1, D_A)),
        _const_spec((N_HEADS_A, GMLP_CHUNK, GMLP_CHUNK)),
        _const_spec((GMLP_CHUNK, N_HEADS_A)),
        _const_spec((N_POOL_GROUPS, POOL_GROUP_DIM, POOL_GROUP_DIM)),
        _const_spec((1, D_B)),
        _const_spec((1, D_A)), _const_spec((1, D_B)),
        _const_spec((D_A + D_B, D_MODEL)),
        _const_spec((1, D_MODEL)),
        _const_spec((D_MODEL, N_EXPERTS)),
        _const_spec((1, N_EXPERTS)),
    ]


def _mixer_prompt(x, weights):
    bsz, seq, _ = x.shape
    tm = TM_PROMPT
    nt = seq // tm
    return pl.pallas_call(
        _mixer_prompt_kernel,
        name="mixer_prompt",
        grid=(bsz, nt),
        in_specs=[pl.BlockSpec((1, tm, D_MODEL), lambda b, j: (b, j, 0))] + _mixer_weight_specs(),
        out_specs=[
            pl.BlockSpec((1, tm, D_MODEL), lambda b, j: (b, j, 0)),
            pl.BlockSpec((tm, D_MODEL // 2), lambda b, j: (b * nt + j, 0)),
            pl.BlockSpec((tm, LANES), lambda b, j: (b * nt + j, 0)),
            pl.BlockSpec((1, POOL_MAX, D_B), lambda b, j: (b, 0, 0)),
            pl.BlockSpec((8, LANES), lambda b, j: (0, 0)),
        ],
        out_shape=[
            jax.ShapeDtypeStruct((bsz, seq, D_MODEL), F32),
            jax.ShapeDtypeStruct((bsz * seq, D_MODEL // 2), jnp.int32),
            jax.ShapeDtypeStruct((bsz * seq, LANES), F32),
            jax.ShapeDtypeStruct((bsz, POOL_MAX, D_B), F32),
            jax.ShapeDtypeStruct((8, LANES), F32),
        ],
        scratch_shapes=[
            pltpu.VMEM((POOL_MAX, D_B), F32),
            pltpu.VMEM((8, LANES), F32),
            pltpu.VMEM((tm, D_A), F32),
            pltpu.VMEM((tm, D_B), F32),
        ],
        compiler_params=pltpu.CompilerParams(
            dimension_semantics=("arbitrary", "arbitrary"), vmem_limit_bytes=VMEM_LIMIT),
    )(x, *weights)


def _mixer_sample(x2d, hist, cnt_in, weights):
    rows = x2d.shape[0]
    n_streams, hist_rows, _ = hist.shape
    ext_rows = rows + n_streams * hist_rows
    full = lambda shape: pl.BlockSpec(shape, lambda i: (0,) * len(shape))
    return pl.pallas_call(
        _mixer_sample_kernel,
        name="mixer_sample",
        grid=(1,),
        in_specs=[full((rows, D_MODEL)), full(hist.shape), full((8, LANES))] + _mixer_weight_specs(),
        out_specs=[
            full((rows, D_MODEL)), full((rows, D_MODEL // 2)), full((rows, LANES)),
            full((rows, D_A)), full((rows, D_B)), full((8, LANES)),
        ],
        out_shape=[
            jax.ShapeDtypeStruct((rows, D_MODEL), F32),
            jax.ShapeDtypeStruct((rows, D_MODEL // 2), jnp.int32),
            jax.ShapeDtypeStruct((rows, LANES), F32),
            jax.ShapeDtypeStruct((rows, D_A), F32),
            jax.ShapeDtypeStruct((rows, D_B), F32),
            jax.ShapeDtypeStruct((8, LANES), F32),
        ],
        scratch_shapes=[
            pltpu.VMEM((8, LANES), F32),
            pltpu.VMEM((rows, D_A), F32),
            pltpu.VMEM((rows, D_B), F32),
            pltpu.VMEM((ext_rows, D_B), F32),
        ],
        compiler_params=pltpu.CompilerParams(
            dimension_semantics=("arbitrary",), vmem_limit_bytes=VMEM_LIMIT),
    )(x2d, hist, cnt_in, *weights)


def _dispatch_kernel(n_prompt_tiles, dest_ref, cnt_ref, region_ref, nused_ref,
                     hp_p_ref, hp_s_ref, xs_hbm, zbuf, sem, zsem):
    i = pl.program_id(0)
    tm = hp_p_ref.shape[0]
    n_blocks = xs_hbm.shape[0] // SLOT_BLOCK

    def pad_rows(e):
        cnt = cnt_ref[e]
        nblk = (cnt + SLOT_BLOCK - 1) // SLOT_BLOCK
        lo = region_ref[e] + cnt - (nblk - 1) * SLOT_BLOCK
        hi = region_ref[e] + SLOT_BLOCK
        return lo, jnp.where(nblk > 0, hi, lo)

    def zero_row_copy(r):
        return pltpu.make_async_copy(zbuf.at[pl.ds(0, 1)], xs_hbm.at[pl.ds(r, 1)], zsem.at[0])

    def zero_block_copy(rb):
        return pltpu.make_async_copy(zbuf, xs_hbm.at[pl.ds(rb * SLOT_BLOCK, SLOT_BLOCK)], zsem.at[1])

    def for_each_fill(row_fn, block_fn):
        def per_expert(e, carry):
            lo, hi = pad_rows(e)
            lax.fori_loop(lo, hi, lambda r, c: (row_fn(r), c)[1], 0)
            return carry
        lax.fori_loop(0, N_EXPERTS, per_expert, 0)
        lax.fori_loop(nused_ref[0], n_blocks, lambda rb, c: (block_fn(rb), c)[1], 0)

    @pl.when(i == 0)
    def _():
        zbuf[...] = jnp.zeros_like(zbuf)
        for_each_fill(lambda r: zero_row_copy(r).start(), lambda rb: zero_block_copy(rb).start())

    def scatter(src_ref):
        def body(t, carry):
            base = (i * tm + t) * TOP_K
            for k in range(TOP_K):
                d = dest_ref[base + k]
                pltpu.make_async_copy(
                    src_ref.at[pl.ds(t, 1)], xs_hbm.at[pl.ds(d, 1)], sem).start(priority=k % 2)
            return carry
        lax.fori_loop(0, tm, body, 0, unroll=8)

    @pl.when(i < n_prompt_tiles)
    def _():
        scatter(hp_p_ref)

    @pl.when(i >= n_prompt_tiles)
    def _():
        scatter(hp_s_ref)

    for _ in range(TOP_K):
        pltpu.make_async_copy(hp_p_ref, xs_hbm.at[pl.ds(0, tm)], sem).wait()

    @pl.when(i == 0)
    def _():
        for_each_fill(lambda r: zero_row_copy(r).wait(), lambda rb: zero_block_copy(rb).wait())


def _dispatch(dest, counts, region, n_used, hp_p, hp_s, n_blocks):
    tm = TM_DISPATCH
    ntp = hp_p.shape[0] // tm
    nts = hp_s.shape[0] // tm
    half = D_MODEL // 2
    return pl.pallas_call(
        functools.partial(_dispatch_kernel, ntp),
        name="dispatch",
        grid_spec=pltpu.PrefetchScalarGridSpec(
            num_scalar_prefetch=4,
            grid=(ntp + nts,),
            in_specs=[
                pl.BlockSpec((tm, half), lambda i, *_: (jnp.minimum(i, ntp - 1), 0)),
                pl.BlockSpec((tm, half), lambda i, *_: (jnp.maximum(i - ntp, 0), 0)),
            ],
            out_specs=pl.BlockSpec(memory_space=pl.ANY),
            scratch_shapes=[
                pltpu.VMEM((SLOT_BLOCK, half), jnp.int32),
                pltpu.SemaphoreType.DMA,
                pltpu.SemaphoreType.DMA((2,)),
            ],
        ),
        out_shape=jax.ShapeDtypeStruct((n_blocks * SLOT_BLOCK, half), jnp.int32),
        compiler_params=pltpu.CompilerParams(dimension_semantics=("arbitrary",)),
    )(dest, counts, region, n_used, hp_p, hp_s)


def _unpack_bf16_pairs(w):
    lo = lax.bitcast_convert_type(lax.shift_left(w, 16), F32).astype(BF16)
    hi = lax.bitcast_convert_type(w & jnp.int32(-65536), F32).astype(BF16)
    return jnp.concatenate([lo, hi], axis=1)


def _for_each_sub_block(nsub, compute, out_ref):
    for s in range(SLOT_BLOCK // SUB_BLOCK):
        rows = pl.ds(s * SUB_BLOCK, SUB_BLOCK)

        @pl.when(s < nsub)
        def _(rows=rows):
            compute(rows)

        @pl.when(s >= nsub)
        def _(rows=rows):
            out_ref[rows, :] = jnp.zeros((SUB_BLOCK, out_ref.shape[1]), out_ref.dtype)


def _grouped_call(name, step, grid, in_specs_fn, out_spec, operands, out_shape, scratch_shapes):
    n_in = len(operands)

    def outer(be_ref, first_ref, nsub_ref, *refs):
        ins, out, scratches = refs[:n_in], refs[n_in], refs[n_in + 1:]

        def body(*block_refs):
            step(first_ref, nsub_ref, *block_refs)

        pltpu.emit_pipeline(
            body, grid=grid, in_specs=in_specs_fn(be_ref), out_specs=[out_spec],
        )(*ins, out, scratches=scratches)

    any_spec = pl.BlockSpec(memory_space=pl.ANY)
    return lambda block_expert, first, nsub: pl.pallas_call(
        outer,
        name=name,
        grid_spec=pltpu.PrefetchScalarGridSpec(
            num_scalar_prefetch=3,
            grid=(),
            in_specs=[any_spec] * n_in,
            out_specs=any_spec,
            scratch_shapes=scratch_shapes,
        ),
        out_shape=out_shape,
        compiler_params=pltpu.CompilerParams(vmem_limit_bytes=VMEM_LIMIT),
    )(block_expert, first, nsub, *operands)


def _expert_weight_spec(be_ref, shape):
    return pl.BlockSpec(shape, lambda j, rb: (be_ref[rb], 0, j),
                        pipeline_mode=pl.Buffered(2, use_lookahead=True))


def _ffn_up_step(first_ref, nsub_ref, x_ref, wg_ref, bg_ref, wu_ref, bu_ref, act_ref, wgb, wub):
    rb = pl.program_id(1)

    @pl.when(first_ref[rb] == 1)
    def _():
        wgb[...] = wg_ref[0].astype(BF16)
        wub[...] = wu_ref[0].astype(BF16)

    def compute(rows):
        x = _unpack_bf16_pairs(x_ref[rows, :])
        gate = jnp.minimum(_dot(x, wgb[...]) + bg_ref[0], SWIGLU_LIMIT)
        up = jnp.clip(_dot(x, wub[...]) + bu_ref[0], -SWIGLU_LIMIT, SWIGLU_LIMIT)
        glu = gate * jax.nn.sigmoid(SWIGLU_ALPHA * gate)
        act_ref[rows, :] = ((up + 1.0) * glu).astype(BF16)

    _for_each_sub_block(nsub_ref[rb], compute, act_ref)


def _ffn_up(block_expert, first, nsub, xs, w_gate, b_gate, w_up, b_up, n_blocks):
    def in_specs(be_ref):
        w_spec = _expert_weight_spec(be_ref, (1, D_MODEL, FF_TILE))
        b_spec = pl.BlockSpec((1, 1, FF_TILE), lambda f, rb: (be_ref[rb], 0, f))
        return [pl.BlockSpec((SLOT_BLOCK, D_MODEL // 2), lambda f, rb: (rb, 0)), w_spec, b_spec, w_spec, b_spec]

    return _grouped_call(
        "ffn_up", _ffn_up_step, (D_FF // FF_TILE, n_blocks), in_specs,
        pl.BlockSpec((SLOT_BLOCK, FF_TILE), lambda f, rb: (rb, f)),
        (xs, w_gate, b_gate, w_up, b_up),
        jax.ShapeDtypeStruct((n_blocks * SLOT_BLOCK, D_FF), BF16),
        [pltpu.VMEM((D_MODEL, FF_TILE), BF16), pltpu.VMEM((D_MODEL, FF_TILE), BF16)],
    )(block_expert, first, nsub)


def _ffn_down_step(first_ref, nsub_ref, act_ref, wd_ref, bd_ref, y_ref, wdb):
    rb = pl.program_id(1)

    @pl.when(first_ref[rb] == 1)
    def _():
        wdb[...] = wd_ref[0].astype(BF16)

    def compute(rows):
        y_ref[rows, :] = _dot(act_ref[rows, :], wdb[...]) + bd_ref[0]

    _for_each_sub_block(nsub_ref[rb], compute, y_ref)


def _ffn_down(block_expert, first, nsub, act, w_down, b_down, n_blocks):
    def in_specs(be_ref):
        return [
            pl.BlockSpec((SLOT_BLOCK, D_FF), lambda n, rb: (rb, 0)),
            _expert_weight_spec(be_ref, (1, D_FF, OUT_TILE)),
            pl.BlockSpec((1, 1, OUT_TILE), lambda n, rb: (be_ref[rb], 0, n)),
        ]

    return _grouped_call(
        "ffn_down", _ffn_down_step, (D_MODEL // OUT_TILE, n_blocks), in_specs,
        pl.BlockSpec((SLOT_BLOCK, OUT_TILE), lambda n, rb: (rb, n)),
        (act, w_down, b_down),
        jax.ShapeDtypeStruct((n_blocks * SLOT_BLOCK, D_MODEL), F32),
        [pltpu.VMEM((D_FF, OUT_TILE), BF16)],
    )(block_expert, first, nsub)


def _combine_kernel(dest_ref, x1_ref, meta_ref, g_ref, ys_hbm, out_ref, ybuf, sem):
    i = pl.program_id(0)
    nt = pl.num_programs(0)
    tm = x1_ref.shape[0]

    def gather(tile, slot):
        def body(t, carry):
            for k in range(TOP_K):
                d = dest_ref[(tile * tm + t) * TOP_K + k]
                pltpu.make_async_copy(
                    ys_hbm.at[pl.ds(d, 1)], ybuf.at[slot, pl.ds(k * tm + t, 1)], sem.at[slot]).start(
                        priority=k % 2)
            return carry
        lax.fori_loop(0, tm, body, 0, unroll=8)

    @pl.when(i == 0)
    def _():
        gather(0, 0)

    @pl.when(i + 1 < nt)
    def _():
        gather(i + 1, (i + 1) % 2)

    slot = i % 2
    pltpu.make_async_copy(ys_hbm.at[pl.ds(0, TOP_K * tm)], ybuf.at[slot], sem.at[slot]).wait()
    acc = x1_ref[...]
    for k in range(TOP_K):
        acc = acc + ybuf[slot, k * tm:(k + 1) * tm, :] * meta_ref[:, TOP_K + k:TOP_K + k + 1]
    out_ref[...] = _rms_norm(acc, g_ref[...])


def _combine(dest_flat, x1, meta, g_final, ys):
    n = x1.shape[0]
    tm = TM_COMBINE
    return pl.pallas_call(
        _combine_kernel,
        name="combine",
        grid_spec=pltpu.PrefetchScalarGridSpec(
            num_scalar_prefetch=1,
            grid=(n // tm,),
            in_specs=[
                pl.BlockSpec((tm, D_MODEL), lambda i, d: (i, 0)),
                pl.BlockSpec((tm, LANES), lambda i, d: (i, 0)),
                pl.BlockSpec((1, D_MODEL), lambda i, d: (0, 0)),
                pl.BlockSpec(memory_space=pl.ANY),
            ],
            out_specs=pl.BlockSpec((tm, D_MODEL), lambda i, d: (i, 0)),
            scratch_shapes=[
                pltpu.VMEM((2, TOP_K * tm, D_MODEL), F32),
                pltpu.SemaphoreType.DMA((2,)),
            ],
        ),
        out_shape=jax.ShapeDtypeStruct((n, D_MODEL), F32),
        compiler_params=pltpu.CompilerParams(
            dimension_semantics=("arbitrary",), vmem_limit_bytes=VMEM_LIMIT),
    )(dest_flat, x1, meta, g_final, ys)


def kernel(x_prompt, x_sample, cache_pool, norm_mix_g, w_in, ln_v_g, ln_v_b, w_spatial, b_spatial, w_pool,
           pool_scale, out_norm_a_g, out_norm_b_g, w_out, norm_ffn_g, w_router, b_router, w_gate, b_gate,
           w_up, b_up, w_down, b_down, final_norm_g):
    depth = norm_mix_g.shape[0]
    assert depth == 1
    bsz, seq, _ = x_prompt.shape
    dec_b, dec_t, _ = x_sample.shape
    assert seq % TM_PROMPT == 0 and TM_PROMPT % GMLP_CHUNK == 0
    assert GMLP_CHUNK % dec_t == 0 and (dec_b * dec_t) % GMLP_CHUNK == 0 and PAST_LEN % GMLP_CHUNK == 0
    assert dec_t >= POOL_STATE and PAST_LEN + 1 >= POOL_MAX
    n_p = bsz * seq
    n_s = dec_b * dec_t
    n = n_p + n_s
    assert n_p % TM_DISPATCH == 0 and n_s % TM_DISPATCH == 0
    assert n_p % TM_COMBINE == 0 and n_s % TM_COMBINE == 0
    l = 0

    row = lambda v: v.reshape(1, -1)
    shared = [
        row(norm_mix_g[l]), w_in[l].astype(BF16), row(ln_v_g[l]), row(ln_v_b[l]),
    ]
    tail = [
        w_pool[l].astype(BF16), row(pool_scale[l]), row(out_norm_a_g[l]), row(out_norm_b_g[l]),
        w_out[l].astype(BF16), row(norm_ffn_g[l]), w_router[l].astype(BF16), row(b_router[l]),
    ]
    weights_p = shared + [w_spatial[l], b_spatial[l].T] + tail
    reps = GMLP_CHUNK // dec_t
    weights_s = shared + [
        jnp.tile(w_spatial[l][:, :dec_t, :dec_t], (1, reps, reps)),
        jnp.tile(b_spatial[l][:, :dec_t].T, (reps, 1)),
    ] + tail

    x1_p, hp_p, meta_p, ptail, cnt_p = _mixer_prompt(x_prompt, weights_p)
    hist = jnp.pad(cache_pool[l], ((0, 0), (POOL_MAX - POOL_STATE, 0), (0, 0)))
    x1_s, hp_s, meta_s, vn_s, p_s, cnt = _mixer_sample(
        x_sample.reshape(n_s, D_MODEL), hist, cnt_p, weights_s)

    i32 = jnp.int32
    counts = cnt[0, :N_EXPERTS].astype(i32)
    n_blocks = -(-(n * TOP_K + N_EXPERTS * (SLOT_BLOCK - 1)) // SLOT_BLOCK)
    blocks_e = (counts + SLOT_BLOCK - 1) // SLOT_BLOCK
    blk_end = jnp.cumsum(blocks_e)
    blk_start = blk_end - blocks_e
    region = blk_start * SLOT_BLOCK
    rows_first = counts - (blocks_e - 1) * SLOT_BLOCK
    n_used = blk_end[-1]
    experts = jnp.arange(N_EXPERTS, dtype=i32)

    def slot_index(meta_part):
        idx = meta_part[:, 0:TOP_K].astype(i32)
        rank = meta_part[:, 2 * TOP_K:3 * TOP_K].astype(i32)
        sel = idx[:, :, None] == experts
        reg = jnp.sum(jnp.where(sel, region, 0), axis=-1)
        rf = jnp.sum(jnp.where(sel, rows_first, 0), axis=-1)
        return (reg + jnp.where(rank < rf, rank, SLOT_BLOCK + rank - rf)).reshape(-1)

    dest_p = slot_index(meta_p)
    dest_s = slot_index(meta_s)
    blk = jnp.arange(n_blocks, dtype=i32)
    blk_c = jnp.minimum(blk, n_used - 1)
    block_expert = jnp.minimum(
        jnp.sum((blk_c[:, None] >= blk_end[None, :]).astype(i32), axis=-1), N_EXPERTS - 1)
    is_first = blk_c == blk_start[block_expert]
    used = blk < n_used
    first = (is_first & used).astype(i32)
    sub_first = (rows_first[block_expert] + SUB_BLOCK - 1) // SUB_BLOCK
    nsub = jnp.where(used, jnp.where(is_first, sub_first, SLOT_BLOCK // SUB_BLOCK), 0).astype(i32)

    xs = _dispatch(jnp.concatenate([dest_p, dest_s]), counts, region, n_used.reshape(1), hp_p, hp_s, n_blocks)
    act = _ffn_up(block_expert, first, nsub, xs, w_gate[l], b_gate[l].reshape(N_EXPERTS, 1, D_FF),
                  w_up[l], b_up[l].reshape(N_EXPERTS, 1, D_FF), n_blocks)
    ys = _ffn_down(block_expert, first, nsub, act, w_down[l], b_down[l].reshape(N_EXPERTS, 1, D_MODEL),
                   n_blocks)

    g_final = row(final_norm_g)
    y_p = _combine(dest_p, x1_p.reshape(n_p, D_MODEL), meta_p, g_final, ys)
    y_s = _combine(dest_s, x1_s, meta_s, g_final, ys)

    y_prompt = y_p.reshape(bsz, seq, D_MODEL)
    y_sample = y_s.reshape(dec_b, dec_t, D_MODEL)
    state_pool_prompt = ptail[:, POOL_MAX - POOL_STATE:][None]
    p_s3 = p_s.reshape(dec_b, dec_t, D_B)
    state_pool_sample = p_s3[:, dec_t - POOL_STATE:][None]
    state_chunk_v_sample = vn_s.reshape(dec_b, dec_t, D_A)[None]
    return (y_prompt, y_sample, state_pool_prompt, state_pool_sample, state_chunk_v_sample)
```

```python
import functools

import jax
import jax.numpy as jnp
from jax import lax
from jax.experimental import pallas as pl
from jax.experimental.pallas import tpu as pltpu

D_MODEL = 2048
D_A = 1024
D_B = 1024
GMLP_CHUNK = 128
N_HEADS_A = 8
HEAD_DIM_A = D_A // N_HEADS_A
POOL_WINDOWS = (2, 4, 8, 16)
N_POOL_GROUPS = len(POOL_WINDOWS)
POOL_GROUP_DIM = D_B // N_POOL_GROUPS
POOL_MAX = 16
POOL_STATE = POOL_MAX - 1
PAST_LEN = 1024
N_EXPERTS = 32
TOP_K = 4
D_FF = 2048
SWIGLU_LIMIT = 7.0
SWIGLU_ALPHA = 1.702
EPS = 1e-5

LANES = 128
TM_PROMPT = 256
SLOT_BLOCK = 512
SUB_BLOCK = 256
TM_DISPATCH = 256
FF_TILE = 1024
TM_COMBINE = 128
VMEM_LIMIT = 56 * 1024 * 1024

BF16 = jnp.bfloat16
F32 = jnp.float32


def _dot(a, b):
    return jnp.dot(a, b, preferred_element_type=F32)


def _rms_norm(x, g):
    return x * lax.rsqrt(jnp.mean(x * x, axis=-1, keepdims=True) + EPS) * g


def _layer_norm(x, g, b):
    mu = jnp.mean(x, axis=-1, keepdims=True)
    xc = x - mu
    return xc * lax.rsqrt(jnp.mean(xc * xc, axis=-1, keepdims=True) + EPS) * g + b


def _gelu(x):
    return 0.5 * x * (1.0 + lax.erf(x * (2.0 ** -0.5)))


def _mixer_front(x, g_mix, w_in, ln_g, ln_b):
    h = _rms_norm(x, g_mix[...]).astype(BF16)
    u = _gelu(_dot(h, w_in[:, 0:D_A]))
    vn = _layer_norm(_dot(h, w_in[:, D_A:2 * D_A]), ln_g[...], ln_b[...])
    p = _dot(h, w_in[:, 2 * D_A:2 * D_A + D_B])
    return u, vn, p


def _spatial_gate(u, vn, ws, bs_t, chunk, a_ref):
    rows = u.shape[0]
    t = lax.broadcasted_iota(jnp.int32, (GMLP_CHUNK, GMLP_CHUNK), 0)
    s = lax.broadcasted_iota(jnp.int32, (GMLP_CHUNK, GMLP_CHUNK), 1)
    mask = (s <= t) & ((t // chunk) == (s // chunk))
    vb = vn.astype(BF16)
    for h in range(N_HEADS_A):
        w = jnp.where(mask, ws[h], 0.0).astype(BF16)
        bias = bs_t[:, h:h + 1]
        c0 = h * HEAD_DIM_A
        for r0 in range(0, rows, GMLP_CHUNK):
            mixed = _dot(w, vb[r0:r0 + GMLP_CHUNK, c0:c0 + HEAD_DIM_A]) + bias
            a_ref[r0:r0 + GMLP_CHUNK, c0:c0 + HEAD_DIM_A] = u[r0:r0 + GMLP_CHUNK, c0:c0 + HEAD_DIM_A] * mixed


def _window_sums(e):
    parts = []
    s = e
    for g, w in enumerate(POOL_WINDOWS):
        s = s + pltpu.roll(s, w // 2, 0)
        parts.append(s[:, 0:POOL_GROUP_DIM])
        if g + 1 < N_POOL_GROUPS:
            s = s[:, POOL_GROUP_DIM:]
    return parts


def _pool_project(diffs, w_pool, pool_scale, b_ref):
    for g in range(N_POOL_GROUPS):
        c0 = g * POOL_GROUP_DIM
        out = _dot(diffs[g].astype(BF16), w_pool[g])
        b_ref[:, c0:c0 + POOL_GROUP_DIM] = out * pool_scale[:, c0:c0 + POOL_GROUP_DIM]


def _mixer_back(x, a_ref, b_ref, g_a, g_b, w_out, g_ffn):
    na = _rms_norm(a_ref[...], g_a[...]).astype(BF16)
    nb = _rms_norm(b_ref[...], g_b[...]).astype(BF16)
    x1 = x + _dot(na, w_out[0:D_A, :]) + _dot(nb, w_out[D_A:D_A + D_B, :])
    hb = _rms_norm(x1, g_ffn[...]).astype(BF16)
    return x1, hb


def _pack_bf16_pairs(hb):
    half = D_MODEL // 2
    lo = lax.bitcast_convert_type(hb[:, 0:half].astype(F32), jnp.int32)
    hi = lax.bitcast_convert_type(hb[:, half:D_MODEL].astype(F32), jnp.int32)
    return lax.shift_right_logical(lo, 16) | hi


def _route(hb, w_router, b_router, run_ref):
    rows = hb.shape[0]
    logits = _dot(hb, w_router[...]) + b_router[...]
    lane = lax.broadcasted_iota(jnp.int32, (rows, N_EXPERTS), 1)
    vals, idxs = [], []
    l = logits
    for _ in range(TOP_K):
        m = jnp.max(l, axis=-1, keepdims=True)
        i = jnp.min(jnp.where(l == m, lane, N_EXPERTS), axis=-1, keepdims=True)
        vals.append(m)
        idxs.append(i)
        l = jnp.where(lane == i, -jnp.inf, l)
    exps = [jnp.exp(v - vals[0]) for v in vals]
    denom = exps[0] + exps[1] + exps[2] + exps[3]
    gates = [e / denom for e in exps]

    onehot = jnp.zeros((rows, N_EXPERTS), F32)
    for i in idxs:
        onehot = jnp.where(lane == i, 1.0, onehot)
    r = lax.broadcasted_iota(jnp.int32, (rows, rows), 0)
    c = lax.broadcasted_iota(jnp.int32, (rows, rows), 1)
    before = jnp.where(c < r, 1.0, 0.0).astype(BF16)
    base = _dot(before, onehot.astype(BF16)) + run_ref[0:1, 0:N_EXPERTS]
    ranks = [jnp.sum(jnp.where(lane == i, base, 0.0), axis=-1, keepdims=True) for i in idxs]
    run_ref[0:1, 0:N_EXPERTS] = run_ref[0:1, 0:N_EXPERTS] + jnp.sum(onehot, axis=0, keepdims=True)

    lane_m = lax.broadcasted_iota(jnp.int32, (rows, LANES), 1)
    meta = jnp.zeros((rows, LANES), F32)
    cols = [i.astype(F32) for i in idxs] + gates + ranks
    for j, col in enumerate(cols):
        meta = jnp.where(lane_m == j, col, meta)
    return meta


def _mixer_prompt_kernel(x_ref, g_mix, w_in, ln_g, ln_b, ws, bs_t, w_pool, pool_scale, g_a, g_b, w_out,
                         g_ffn, w_router, b_router,
                         x1_ref, hp_ref, meta_ref, ptail_ref, cnt_ref,
                         halo_ref, run_ref, a_ref, b_ref):
    tm = x_ref.shape[1]
    j = pl.program_id(1)

    @pl.when((pl.program_id(0) == 0) & (j == 0))
    def _():
        run_ref[...] = jnp.zeros_like(run_ref)

    @pl.when(j == 0)
    def _():
        halo_ref[...] = jnp.zeros_like(halo_ref)

    x = x_ref[0]
    u, vn, p = _mixer_front(x, g_mix, w_in, ln_g, ln_b)
    _spatial_gate(u, vn, ws, bs_t, GMLP_CHUNK, a_ref)

    e = jnp.concatenate([halo_ref[...], p], axis=0)
    halo_ref[...] = p[tm - POOL_MAX:tm]
    ptail_ref[0] = p[tm - POOL_MAX:tm]
    pos = j * tm + lax.broadcasted_iota(jnp.int32, (tm, 1), 0)
    sums = _window_sums(e)
    diffs = []
    for g, w in enumerate(POOL_WINDOWS):
        inv_cnt = 1.0 / jnp.minimum(pos + 1, w).astype(F32)
        c0 = g * POOL_GROUP_DIM
        diffs.append(sums[g][POOL_MAX:] * inv_cnt - p[:, c0:c0 + POOL_GROUP_DIM])
    _pool_project(diffs, w_pool, pool_scale, b_ref)

    x1, hb = _mixer_back(x, a_ref, b_ref, g_a, g_b, w_out, g_ffn)
    x1_ref[0] = x1
    hp_ref[...] = _pack_bf16_pairs(hb)
    meta_ref[...] = _route(hb, w_router, b_router, run_ref)
    cnt_ref[...] = run_ref[...]


def _mixer_sample_kernel(x_ref, hist_ref, cnt_in_ref, g_mix, w_in, ln_g, ln_b, ws, bs_t, w_pool, pool_scale,
                         g_a, g_b, w_out, g_ffn, w_router, b_router,
                         x1_ref, hp_ref, meta_ref, vn_ref, p_ref, cnt_ref,
                         run_ref, a_ref, b_ref, ext_ref):
    n_streams, hist_rows, _ = hist_ref.shape
    t_new = x_ref.shape[0] // n_streams
    group = hist_rows + t_new
    run_ref[...] = cnt_in_ref[...]

    x = x_ref[...]
    u, vn, p = _mixer_front(x, g_mix, w_in, ln_g, ln_b)
    vn_ref[...] = vn
    p_ref[...] = p
    _spatial_gate(u, vn, ws, bs_t, t_new, a_ref)

    for b in range(n_streams):
        ext_ref[b * group:b * group + hist_rows, :] = hist_ref[b]
        ext_ref[b * group + hist_rows:(b + 1) * group, :] = p[b * t_new:(b + 1) * t_new]
    e = ext_ref[...]
    sums = _window_sums(e)
    diffs = []
    for g, w in enumerate(POOL_WINDOWS):
        c0 = g * POOL_GROUP_DIM
        d = sums[g] * (1.0 / w) - e[:, c0:c0 + POOL_GROUP_DIM]
        diffs.append(jnp.concatenate(
            [d[b * group + hist_rows:(b + 1) * group] for b in range(n_streams)], axis=0))
    _pool_project(diffs, w_pool, pool_scale, b_ref)

    x1, hb = _mixer_back(x, a_ref, b_ref, g_a, g_b, w_out, g_ffn)
    x1_ref[...] = x1
    hp_ref[...] = _pack_bf16_pairs(hb)
    meta_ref[...] = _route(hb, w_router, b_router, run_ref)
    cnt_ref[...] = run_ref[...]


def _const_spec(shape):
    zeros = (0,) * len(shape)
    return pl.BlockSpec(shape, lambda *_: zeros, pipeline_mode=pl.Buffered(1))


def _mixer_weight_specs():
    return [
        _const_spec((1, D_MODEL)),
        _const_spec((D_MODEL, 2 * D_A + D_B)),
        _const_spec((1, D_A)), _const_spec((1, D_A)),
        _const_spec((N_HEADS_A, GMLP_CHUNK, GMLP_CHUNK)),
        _const_spec((GMLP_CHUNK, N_HEADS_A)),
        _const_spec((N_POOL_GROUPS, POOL_GROUP_DIM, POOL_GROUP_DIM)),
        _const_spec((1, D_B)),
        _const_spec((1, D_A)), _const_spec((1, D_B)),
        _const_spec((D_A + D_B, D_MODEL)),
        _const_spec((1, D_MODEL)),
        _const_spec((D_MODEL, N_EXPERTS)),
        _const_spec((1, N_EXPERTS)),
    ]


def _mixer_prompt(x, weights):
    bsz, seq, _ = x.shape
    tm = TM_PROMPT
    nt = seq // tm
    return pl.pallas_call(
        _mixer_prompt_kernel,
        name="mixer_prompt",
        grid=(bsz, nt),
        in_specs=[pl.BlockSpec((1, tm, D_MODEL), lambda b, j: (b, j, 0))] + _mixer_weight_specs(),
        out_specs=[
            pl.BlockSpec((1, tm, D_MODEL), lambda b, j: (b, j, 0)),
            pl.BlockSpec((tm, D_MODEL // 2), lambda b, j: (b * nt + j, 0)),
            pl.BlockSpec((tm, LANES), lambda b, j: (b * nt + j, 0)),
            pl.BlockSpec((1, POOL_MAX, D_B), lambda b, j: (b, 0, 0)),
            pl.BlockSpec((8, LANES), lambda b, j: (0, 0)),
        ],
        out_shape=[
            jax.ShapeDtypeStruct((bsz, seq, D_MODEL), F32),
            jax.ShapeDtypeStruct((bsz * seq, D_MODEL // 2), jnp.int32),
            jax.ShapeDtypeStruct((bsz * seq, LANES), F32),
            jax.ShapeDtypeStruct((bsz, POOL_MAX, D_B), F32),
            jax.ShapeDtypeStruct((8, LANES), F32),
        ],
        scratch_shapes=[
            pltpu.VMEM((POOL_MAX, D_B), F32),
            pltpu.VMEM((8, LANES), F32),
            pltpu.VMEM((tm, D_A), F32),
            pltpu.VMEM((tm, D_B), F32),
        ],
        compiler_params=pltpu.CompilerParams(
            dimension_semantics=("arbitrary", "arbitrary"), vmem_limit_bytes=VMEM_LIMIT),
    )(x, *weights)


def _mixer_sample(x2d, hist, cnt_in, weights):
    rows = x2d.shape[0]
    n_streams, hist_rows, _ = hist.shape
    ext_rows = rows + n_streams * hist_rows
    full = lambda shape: pl.BlockSpec(shape, lambda i: (0,) * len(shape))
    return pl.pallas_call(
        _mixer_sample_kernel,
        name="mixer_sample",
        grid=(1,),
        in_specs=[full((rows, D_MODEL)), full(hist.shape), full((8, LANES))] + _mixer_weight_specs(),
        out_specs=[
            full((rows, D_MODEL)), full((rows, D_MODEL // 2)), full((rows, LANES)),
            full((rows, D_A)), full((rows, D_B)), full((8, LANES)),
        ],
        out_shape=[
            jax.ShapeDtypeStruct((rows, D_MODEL), F32),
            jax.ShapeDtypeStruct((rows, D_MODEL // 2), jnp.int32),
            jax.ShapeDtypeStruct((rows, LANES), F32),
            jax.ShapeDtypeStruct((rows, D_A), F32),
            jax.ShapeDtypeStruct((rows, D_B), F32),
            jax.ShapeDtypeStruct((8, LANES), F32),
        ],
        scratch_shapes=[
            pltpu.VMEM((8, LANES), F32),
            pltpu.VMEM((rows, D_A), F32),
            pltpu.VMEM((rows, D_B), F32),
            pltpu.VMEM((ext_rows, D_B), F32),
        ],
        compiler_params=pltpu.CompilerParams(
            dimension_semantics=("arbitrary",), vmem_limit_bytes=VMEM_LIMIT),
    )(x2d, hist, cnt_in, *weights)


def _dispatch_kernel(n_prompt_tiles, dest_ref, cnt_ref, region_ref, nused_ref,
                     hp_p_ref, hp_s_ref, xs_hbm, zbuf, sem, zsem):
    i = pl.program_id(0)
    tm = hp_p_ref.shape[0]
    n_blocks = xs_hbm.shape[0] // SLOT_BLOCK

    def pad_rows(e):
        cnt = cnt_ref[e]
        nblk = (cnt + SLOT_BLOCK - 1) // SLOT_BLOCK
        lo = region_ref[e] + cnt - (nblk - 1) * SLOT_BLOCK
        hi = region_ref[e] + SLOT_BLOCK
        return lo, jnp.where(nblk > 0, hi, lo)

    def zero_row_copy(r):
        return pltpu.make_async_copy(zbuf.at[pl.ds(0, 1)], xs_hbm.at[pl.ds(r, 1)], zsem.at[0])

    def zero_block_copy(rb):
        return pltpu.make_async_copy(zbuf, xs_hbm.at[pl.ds(rb * SLOT_BLOCK, SLOT_BLOCK)], zsem.at[1])

    def for_each_fill(row_fn, block_fn):
        def per_expert(e, carry):
            lo, hi = pad_rows(e)
            lax.fori_loop(lo, hi, lambda r, c: (row_fn(r), c)[1], 0)
            return carry
        lax.fori_loop(0, N_EXPERTS, per_expert, 0)
        lax.fori_loop(nused_ref[0], n_blocks, lambda rb, c: (block_fn(rb), c)[1], 0)

    @pl.when(i == 0)
    def _():
        zbuf[...] = jnp.zeros_like(zbuf)
        for_each_fill(lambda r: zero_row_copy(r).start(), lambda rb: zero_block_copy(rb).start())

    def scatter(src_ref):
        def body(t, carry):
            base = (i * tm + t) * TOP_K
            for k in range(TOP_K):
                d = dest_ref[base + k]
                pltpu.make_async_copy(
                    src_ref.at[pl.ds(t, 1)], xs_hbm.at[pl.ds(d, 1)], sem).start(priority=k % 2)
            return carry
        lax.fori_loop(0, tm, body, 0, unroll=8)

    @pl.when(i < n_prompt_tiles)
    def _():
        scatter(hp_p_ref)

    @pl.when(i >= n_prompt_tiles)
    def _():
        scatter(hp_s_ref)

    for _ in range(TOP_K):
        pltpu.make_async_copy(hp_p_ref, xs_hbm.at[pl.ds(0, tm)], sem).wait()

    @pl.when(i == 0)
    def _():
        for_each_fill(lambda r: zero_row_copy(r).wait(), lambda rb: zero_block_copy(rb).wait())


def _dispatch(dest, counts, region, n_used, hp_p, hp_s, n_blocks):
    tm = TM_DISPATCH
    ntp = hp_p.shape[0] // tm
    nts = hp_s.shape[0] // tm
    half = D_MODEL // 2
    return pl.pallas_call(
        functools.partial(_dispatch_kernel, ntp),
        name="dispatch",
        grid_spec=pltpu.PrefetchScalarGridSpec(
            num_scalar_prefetch=4,
            grid=(ntp + nts,),
            in_specs=[
                pl.BlockSpec((tm, half), lambda i, *_: (jnp.minimum(i, ntp - 1), 0)),
                pl.BlockSpec((tm, half), lambda i, *_: (jnp.maximum(i - ntp, 0), 0)),
            ],
            out_specs=pl.BlockSpec(memory_space=pl.ANY),
            scratch_shapes=[
                pltpu.VMEM((SLOT_BLOCK, half), jnp.int32),
                pltpu.SemaphoreType.DMA,
                pltpu.SemaphoreType.DMA((2,)),
            ],
        ),
        out_shape=jax.ShapeDtypeStruct((n_blocks * SLOT_BLOCK, half), jnp.int32),
        compiler_params=pltpu.CompilerParams(dimension_semantics=("arbitrary",)),
    )(dest, counts, region, n_used, hp_p, hp_s)


def _unpack_bf16_pairs(w):
    lo = lax.bitcast_convert_type(lax.shift_left(w, 16), F32).astype(BF16)
    hi = lax.bitcast_convert_type(w & jnp.int32(-65536), F32).astype(BF16)
    return jnp.concatenate([lo, hi], axis=1)


def _for_each_sub_block(nsub, compute, out_ref):
    for s in range(SLOT_BLOCK // SUB_BLOCK):
        rows = pl.ds(s * SUB_BLOCK, SUB_BLOCK)

        @pl.when(s < nsub)
        def _(rows=rows):
            compute(rows)

        @pl.when(s >= nsub)
        def _(rows=rows):
            out_ref[rows, :] = jnp.zeros((SUB_BLOCK, out_ref.shape[1]), out_ref.dtype)


def _grouped_call(name, step, grid, in_specs_fn, out_spec, operands, out_shape, scratch_shapes):
    n_in = len(operands)

    def outer(be_ref, first_ref, nsub_ref, *refs):
        ins, out, scratches = refs[:n_in], refs[n_in], refs[n_in + 1:]

        def body(*block_refs):
            step(first_ref, nsub_ref, *block_refs)

        pltpu.emit_pipeline(
            body, grid=grid, in_specs=in_specs_fn(be_ref), out_specs=[out_spec],
        )(*ins, out, scratches=scratches)

    any_spec = pl.BlockSpec(memory_space=pl.ANY)
    return lambda block_expert, first, nsub: pl.pallas_call(
        outer,
        name=name,
        grid_spec=pltpu.PrefetchScalarGridSpec(
            num_scalar_prefetch=3,
            grid=(),
            in_specs=[any_spec] * n_in,
            out_specs=any_spec,
            scratch_shapes=scratch_shapes,
        ),
        out_shape=out_shape,
        compiler_params=pltpu.CompilerParams(vmem_limit_bytes=VMEM_LIMIT),
    )(block_expert, first, nsub, *operands)


def _expert_weight_spec(be_ref, shape):
    return pl.BlockSpec(shape, lambda j, rb: (be_ref[rb], 0, j),
                        pipeline_mode=pl.Buffered(2, use_lookahead=True))


def _ffn_up_step(first_ref, nsub_ref, x_ref, wg_ref, bg_ref, wu_ref, bu_ref, act_ref, wgb, wub):
    rb = pl.program_id(1)

    @pl.when(first_ref[rb] == 1)
    def _():
        wgb[...] = wg_ref[0].astype(BF16)
        wub[...] = wu_ref[0].astype(BF16)

    def compute(rows):
        x = _unpack_bf16_pairs(x_ref[rows, :])
        gate = jnp.minimum(_dot(x, wgb[...]) + bg_ref[0], SWIGLU_LIMIT)
        up = jnp.clip(_dot(x, wub[...]) + bu_ref[0], -SWIGLU_LIMIT, SWIGLU_LIMIT)
        glu = gate * jax.nn.sigmoid(SWIGLU_ALPHA * gate)
        act_ref[rows, :] = ((up + 1.0) * glu).astype(BF16)

    _for_each_sub_block(nsub_ref[rb], compute, act_ref)


def _ffn_up(block_expert, first, nsub, xs, w_gate, b_gate, w_up, b_up, n_blocks):
    def in_specs(be_ref):
        w_spec = _expert_weight_spec(be_ref, (1, D_MODEL, FF_TILE))
        b_spec = pl.BlockSpec((1, 1, FF_TILE), lambda f, rb: (be_ref[rb], 0, f))
        return [pl.BlockSpec((SLOT_BLOCK, D_MODEL // 2), lambda f, rb: (rb, 0)), w_spec, b_spec, w_spec, b_spec]

    return _grouped_call(
        "ffn_up", _ffn_up_step, (D_FF // FF_TILE, n_blocks), in_specs,
        pl.BlockSpec((SLOT_BLOCK, FF_TILE), lambda f, rb: (rb, f)),
        (xs, w_gate, b_gate, w_up, b_up),
        jax.ShapeDtypeStruct((n_blocks * SLOT_BLOCK, D_FF), BF16),
        [pltpu.VMEM((D_MODEL, FF_TILE), BF16), pltpu.VMEM((D_MODEL, FF_TILE), BF16)],
    )(block_expert, first, nsub)


def _ffn_down_step(first_ref, nsub_ref, act_ref, wd_ref, bd_ref, y_ref, wdb):
    rb = pl.program_id(1)

    @pl.when(first_ref[rb] == 1)
    def _():
        wdb[...] = wd_ref[0].astype(BF16)

    def compute(rows):
        y = _dot(act_ref[rows, :], wdb[...]) + bd_ref[0]
        y_ref[rows, :] = _pack_bf16_pairs(y.astype(BF16))

    _for_each_sub_block(nsub_ref[rb], compute, y_ref)


def _ffn_down(block_expert, first, nsub, act, w_down, b_down, n_blocks):
    def in_specs(be_ref):
        return [
            pl.BlockSpec((SLOT_BLOCK, D_FF), lambda n, rb: (rb, 0)),
            _expert_weight_spec(be_ref, (1, D_FF, D_MODEL)),
            pl.BlockSpec((1, 1, D_MODEL), lambda n, rb: (be_ref[rb], 0, n)),
        ]

    return _grouped_call(
        "ffn_down", _ffn_down_step, (1, n_blocks), in_specs,
        pl.BlockSpec((SLOT_BLOCK, D_MODEL // 2), lambda n, rb: (rb, n)),
        (act, w_down, b_down),
        jax.ShapeDtypeStruct((n_blocks * SLOT_BLOCK, D_MODEL // 2), jnp.int32),
        [pltpu.VMEM((D_FF, D_MODEL), BF16)],
    )(block_expert, first, nsub)


def _combine_kernel(dest_ref, x1_ref, meta_ref, g_ref, ys_hbm, out_ref, ybuf, sem):
    i = pl.program_id(0)
    nt = pl.num_programs(0)
    tm = x1_ref.shape[0]

    def gather(tile, slot):
        def body(t, carry):
            for k in range(TOP_K):
                d = dest_ref[(tile * tm + t) * TOP_K + k]
                pltpu.make_async_copy(
                    ys_hbm.at[pl.ds(d, 1)], ybuf.at[slot, pl.ds(k * tm + t, 1)], sem.at[slot]).start(
                        priority=k % 2)
            return carry
        lax.fori_loop(0, tm, body, 0, unroll=8)

    @pl.when(i == 0)
    def _():
        gather(0, 0)

    @pl.when(i + 1 < nt)
    def _():
        gather(i + 1, (i + 1) % 2)

    slot = i % 2
    pltpu.make_async_copy(ys_hbm.at[pl.ds(0, TOP_K * tm)], ybuf.at[slot], sem.at[slot]).wait()
    half = D_MODEL // 2
    lo = x1_ref[:, 0:half]
    hi = x1_ref[:, half:D_MODEL]
    for k in range(TOP_K):
        w = ybuf[slot, k * tm:(k + 1) * tm, :]
        gate = meta_ref[:, TOP_K + k:TOP_K + k + 1]
        lo = lo + lax.bitcast_convert_type(lax.shift_left(w, 16), F32) * gate
        hi = hi + lax.bitcast_convert_type(w & jnp.int32(-65536), F32) * gate
    ms = (jnp.sum(lo * lo, axis=-1, keepdims=True) + jnp.sum(hi * hi, axis=-1, keepdims=True)) / D_MODEL
    scale = lax.rsqrt(ms + EPS)
    out_ref[:, 0:half] = lo * scale * g_ref[:, 0:half]
    out_ref[:, half:D_MODEL] = hi * scale * g_ref[:, half:D_MODEL]


def _combine(dest_flat, x1, meta, g_final, ys):
    n = x1.shape[0]
    tm = TM_COMBINE
    return pl.pallas_call(
        _combine_kernel,
        name="combine",
        grid_spec=pltpu.PrefetchScalarGridSpec(
            num_scalar_prefetch=1,
            grid=(n // tm,),
            in_specs=[
                pl.BlockSpec((tm, D_MODEL), lambda i, d: (i, 0)),
                pl.BlockSpec((tm, LANES), lambda i, d: (i, 0)),
                pl.BlockSpec((1, D_MODEL), lambda i, d: (0, 0)),
                pl.BlockSpec(memory_space=pl.ANY),
            ],
            out_specs=pl.BlockSpec((tm, D_MODEL), lambda i, d: (i, 0)),
            scratch_shapes=[
                pltpu.VMEM((2, TOP_K * tm, D_MODEL // 2), jnp.int32),
                pltpu.SemaphoreType.DMA((2,)),
            ],
        ),
        out_shape=jax.ShapeDtypeStruct((n, D_MODEL), F32),
        compiler_params=pltpu.CompilerParams(
            dimension_semantics=("arbitrary",), vmem_limit_bytes=VMEM_LIMIT),
    )(dest_flat, x1, meta, g_final, ys)


def kernel(x_prompt, x_sample, cache_pool, norm_mix_g, w_in, ln_v_g, ln_v_b, w_spatial, b_spatial, w_pool,
           pool_scale, out_norm_a_g, out_norm_b_g, w_out, norm_ffn_g, w_router, b_router, w_gate, b_gate,
           w_up, b_up, w_down, b_down, final_norm_g):
    depth = norm_mix_g.shape[0]
    assert depth == 1
    bsz, seq, _ = x_prompt.shape
    dec_b, dec_t, _ = x_sample.shape
    assert seq % TM_PROMPT == 0 and TM_PROMPT % GMLP_CHUNK == 0
    assert GMLP_CHUNK % dec_t == 0 and (dec_b * dec_t) % GMLP_CHUNK == 0 and PAST_LEN % GMLP_CHUNK == 0
    assert dec_t >= POOL_STATE and PAST_LEN + 1 >= POOL_MAX
    n_p = bsz * seq
    n_s = dec_b * dec_t
    n = n_p + n_s
    assert n_p % TM_DISPATCH == 0 and n_s % TM_DISPATCH == 0
    assert n_p % TM_COMBINE == 0 and n_s % TM_COMBINE == 0
    l = 0

    row = lambda v: v.reshape(1, -1)
    shared = [
        row(norm_mix_g[l]), w_in[l].astype(BF16), row(ln_v_g[l]), row(ln_v_b[l]),
    ]
    tail = [
        w_pool[l].astype(BF16), row(pool_scale[l]), row(out_norm_a_g[l]), row(out_norm_b_g[l]),
        w_out[l].astype(BF16), row(norm_ffn_g[l]), w_router[l].astype(BF16), row(b_router[l]),
    ]
    weights_p = shared + [w_spatial[l], b_spatial[l].T] + tail
    reps = GMLP_CHUNK // dec_t
    weights_s = shared + [
        jnp.tile(w_spatial[l][:, :dec_t, :dec_t], (1, reps, reps)),
        jnp.tile(b_spatial[l][:, :dec_t].T, (reps, 1)),
    ] + tail

    x1_p, hp_p, meta_p, ptail, cnt_p = _mixer_prompt(x_prompt, weights_p)
    hist = jnp.pad(cache_pool[l], ((0, 0), (POOL_MAX - POOL_STATE, 0), (0, 0)))
    x1_s, hp_s, meta_s, vn_s, p_s, cnt = _mixer_sample(
        x_sample.reshape(n_s, D_MODEL), hist, cnt_p, weights_s)

    i32 = jnp.int32
    counts = cnt[0, :N_EXPERTS].astype(i32)
    n_blocks = -(-(n * TOP_K + N_EXPERTS * (SLOT_BLOCK - 1)) // SLOT_BLOCK)
    blocks_e = (counts + SLOT_BLOCK - 1) // SLOT_BLOCK
    blk_end = jnp.cumsum(blocks_e)
    blk_start = blk_end - blocks_e
    region = blk_start * SLOT_BLOCK
    rows_first = counts - (blocks_e - 1) * SLOT_BLOCK
    n_used = blk_end[-1]
    experts = jnp.arange(N_EXPERTS, dtype=i32)

    def slot_index(meta_part):
        idx = meta_part[:, 0:TOP_K].astype(i32)
        rank = meta_part[:, 2 * TOP_K:3 * TOP_K].astype(i32)
        sel = idx[:, :, None] == experts
        reg = jnp.sum(jnp.where(sel, region, 0), axis=-1)
        rf = jnp.sum(jnp.where(sel, rows_first, 0), axis=-1)
        return (reg + jnp.where(rank < rf, rank, SLOT_BLOCK + rank - rf)).reshape(-1)

    dest_p = slot_index(meta_p)
    dest_s = slot_index(meta_s)
    blk = jnp.arange(n_blocks, dtype=i32)
    blk_c = jnp.minimum(blk, n_used - 1)
    block_expert = jnp.minimum(
        jnp.sum((blk_c[:, None] >= blk_end[None, :]).astype(i32), axis=-1), N_EXPERTS - 1)
    is_first = blk_c == blk_start[block_expert]
    used = blk < n_used
    first = (is_first & used).astype(i32)
    sub_first = (rows_first[block_expert] + SUB_BLOCK - 1) // SUB_BLOCK
    nsub = jnp.where(used, jnp.where(is_first, sub_first, SLOT_BLOCK // SUB_BLOCK), 0).astype(i32)

    xs = _dispatch(jnp.concatenate([dest_p, dest_s]), counts, region, n_used.reshape(1), hp_p, hp_s, n_blocks)
    act = _ffn_up(block_expert, first, nsub, xs, w_gate[l], b_gate[l].reshape(N_EXPERTS, 1, D_FF),
                  w_up[l], b_up[l].reshape(N_EXPERTS, 1, D_FF), n_blocks)
    ys = _ffn_down(block_expert, first, nsub, act, w_down[l], b_down[l].reshape(N_EXPERTS, 1, D_MODEL),
                   n_blocks)

    g_final = row(final_norm_g)
    y_p = _combine(dest_p, x1_p.reshape(n_p, D_MODEL), meta_p, g_final, ys)
    y_s = _combine(dest_s, x1_s, meta_s, g_final, ys)

    y_prompt = y_p.reshape(bsz, seq, D_MODEL)
    y_sample = y_s.reshape(dec_b, dec_t, D_MODEL)
    state_pool_prompt = ptail[:, POOL_MAX - POOL_STATE:][None]
    p_s3 = p_s.reshape(dec_b, dec_t, D_B)
    state_pool_sample = p_s3[:, dec_t - POOL_STATE:][None]
    state_chunk_v_sample = vn_s.reshape(dec_b, dec_t, D_A)[None]
    return (y_prompt, y_sample, state_pool_prompt, state_pool_sample, state_chunk_v_sample)
```

```python
import functools

import jax
import jax.numpy as jnp
from jax import lax
from jax.experimental import pallas as pl
from jax.experimental.pallas import tpu as pltpu

D_MODEL = 2048
D_A = 1024
D_B = 1024
GMLP_CHUNK = 128
N_HEADS_A = 8
HEAD_DIM_A = D_A // N_HEADS_A
POOL_WINDOWS = (2, 4, 8, 16)
N_POOL_GROUPS = len(POOL_WINDOWS)
POOL_GROUP_DIM = D_B // N_POOL_GROUPS
POOL_MAX = 16
POOL_STATE = POOL_MAX - 1
PAST_LEN = 1024
N_EXPERTS = 32
TOP_K = 4
D_FF = 2048
SWIGLU_LIMIT = 7.0
SWIGLU_ALPHA = 1.702
EPS = 1e-5

LANES = 128
TM_PROMPT = 256
SLOT_BLOCK = 512
SUB_BLOCK = 128
TM_DISPATCH = 1024
FF_TILE = 1024
TM_COMBINE = 128
VMEM_LIMIT = 56 * 1024 * 1024

BF16 = jnp.bfloat16
F32 = jnp.float32


def _dot(a, b):
    return jnp.dot(a, b, preferred_element_type=F32)


def _rms_norm(x, g):
    return x * lax.rsqrt(jnp.mean(x * x, axis=-1, keepdims=True) + EPS) * g


def _layer_norm(x, g, b):
    mu = jnp.mean(x, axis=-1, keepdims=True)
    xc = x - mu
    return xc * lax.rsqrt(jnp.mean(xc * xc, axis=-1, keepdims=True) + EPS) * g + b


def _gelu(x):
    return 0.5 * x * (1.0 + lax.erf(x * (2.0 ** -0.5)))


def _mixer_front(x, g_mix, w_in, ln_g, ln_b):
    h = _rms_norm(x, g_mix[...]).astype(BF16)
    u = _gelu(_dot(h, w_in[:, 0:D_A]))
    vn = _layer_norm(_dot(h, w_in[:, D_A:2 * D_A]), ln_g[...], ln_b[...])
    p = _dot(h, w_in[:, 2 * D_A:2 * D_A + D_B])
    return u, vn, p


def _spatial_gate(u, vn, ws, bs_t, chunk, a_ref):
    rows = u.shape[0]
    t = lax.broadcasted_iota(jnp.int32, (GMLP_CHUNK, GMLP_CHUNK), 0)
    s = lax.broadcasted_iota(jnp.int32, (GMLP_CHUNK, GMLP_CHUNK), 1)
    mask = (s <= t) & ((t // chunk) == (s // chunk))
    vb = vn.astype(BF16)
    for h in range(N_HEADS_A):
        w = jnp.where(mask, ws[h], 0.0).astype(BF16)
        bias = bs_t[:, h:h + 1]
        c0 = h * HEAD_DIM_A
        for r0 in range(0, rows, GMLP_CHUNK):
            mixed = _dot(w, vb[r0:r0 + GMLP_CHUNK, c0:c0 + HEAD_DIM_A]) + bias
            a_ref[r0:r0 + GMLP_CHUNK, c0:c0 + HEAD_DIM_A] = u[r0:r0 + GMLP_CHUNK, c0:c0 + HEAD_DIM_A] * mixed


def _window_sums(e):
    parts = []
    s = e
    for g, w in enumerate(POOL_WINDOWS):
        s = s + pltpu.roll(s, w // 2, 0)
        parts.append(s[:, 0:POOL_GROUP_DIM])
        if g + 1 < N_POOL_GROUPS:
            s = s[:, POOL_GROUP_DIM:]
    return parts


def _pool_project(diffs, w_pool, pool_scale, b_ref):
    for g in range(N_POOL_GROUPS):
        c0 = g * POOL_GROUP_DIM
        out = _dot(diffs[g].astype(BF16), w_pool[g])
        b_ref[:, c0:c0 + POOL_GROUP_DIM] = out * pool_scale[:, c0:c0 + POOL_GROUP_DIM]


def _mixer_back(x, a_ref, b_ref, g_a, g_b, w_out, g_ffn):
    na = _rms_norm(a_ref[...], g_a[...]).astype(BF16)
    nb = _rms_norm(b_ref[...], g_b[...]).astype(BF16)
    x1 = x + _dot(na, w_out[0:D_A, :]) + _dot(nb, w_out[D_A:D_A + D_B, :])
    hb = _rms_norm(x1, g_ffn[...]).astype(BF16)
    return x1, hb


def _pack_bf16_pairs(hb):
    half = D_MODEL // 2
    lo = lax.bitcast_convert_type(hb[:, 0:half].astype(F32), jnp.int32)
    hi = lax.bitcast_convert_type(hb[:, half:D_MODEL].astype(F32), jnp.int32)
    return lax.shift_right_logical(lo, 16) | hi


def _route(hb, w_router, b_router, run_ref):
    rows = hb.shape[0]
    logits = _dot(hb, w_router[...]) + b_router[...]
    lane = lax.broadcasted_iota(jnp.int32, (rows, N_EXPERTS), 1)
    vals, idxs = [], []
    l = logits
    for _ in range(TOP_K):
        m = jnp.max(l, axis=-1, keepdims=True)
        i = jnp.min(jnp.where(l == m, lane, N_EXPERTS), axis=-1, keepdims=True)
        vals.append(m)
        idxs.append(i)
        l = jnp.where(lane == i, -jnp.inf, l)
    exps = [jnp.exp(v - vals[0]) for v in vals]
    denom = exps[0] + exps[1] + exps[2] + exps[3]
    gates = [e / denom for e in exps]

    onehot = jnp.zeros((rows, N_EXPERTS), F32)
    for i in idxs:
        onehot = jnp.where(lane == i, 1.0, onehot)
    r = lax.broadcasted_iota(jnp.int32, (rows, rows), 0)
    c = lax.broadcasted_iota(jnp.int32, (rows, rows), 1)
    before = jnp.where(c < r, 1.0, 0.0).astype(BF16)
    base = _dot(before, onehot.astype(BF16)) + run_ref[0:1, 0:N_EXPERTS]
    ranks = [jnp.sum(jnp.where(lane == i, base, 0.0), axis=-1, keepdims=True) for i in idxs]
    run_ref[0:1, 0:N_EXPERTS] = run_ref[0:1, 0:N_EXPERTS] + jnp.sum(onehot, axis=0, keepdims=True)

    lane_m = lax.broadcasted_iota(jnp.int32, (rows, LANES), 1)
    meta = jnp.zeros((rows, LANES), F32)
    cols = [i.astype(F32) for i in idxs] + gates + ranks
    for j, col in enumerate(cols):
        meta = jnp.where(lane_m == j, col, meta)
    return meta


def _mixer_prompt_kernel(x_ref, g_mix, w_in, ln_g, ln_b, ws, bs_t, w_pool, pool_scale, g_a, g_b, w_out,
                         g_ffn, w_router, b_router,
                         x1_ref, hp_ref, meta_ref, ptail_ref, cnt_ref,
                         halo_ref, run_ref, a_ref, b_ref):
    tm = x_ref.shape[1]
    j = pl.program_id(1)

    @pl.when((pl.program_id(0) == 0) & (j == 0))
    def _():
        run_ref[...] = jnp.zeros_like(run_ref)

    @pl.when(j == 0)
    def _():
        halo_ref[...] = jnp.zeros_like(halo_ref)

    x = x_ref[0]
    u, vn, p = _mixer_front(x, g_mix, w_in, ln_g, ln_b)
    _spatial_gate(u, vn, ws, bs_t, GMLP_CHUNK, a_ref)

    e = jnp.concatenate([halo_ref[...], p], axis=0)
    halo_ref[...] = p[tm - POOL_MAX:tm]
    ptail_ref[0] = p[tm - POOL_MAX:tm]
    pos = j * tm + lax.broadcasted_iota(jnp.int32, (tm, 1), 0)
    sums = _window_sums(e)
    diffs = []
    for g, w in enumerate(POOL_WINDOWS):
        inv_cnt = 1.0 / jnp.minimum(pos + 1, w).astype(F32)
        c0 = g * POOL_GROUP_DIM
        diffs.append(sums[g][POOL_MAX:] * inv_cnt - p[:, c0:c0 + POOL_GROUP_DIM])
    _pool_project(diffs, w_pool, pool_scale, b_ref)

    x1, hb = _mixer_back(x, a_ref, b_ref, g_a, g_b, w_out, g_ffn)
    x1_ref[0] = x1
    hp_ref[...] = _pack_bf16_pairs(hb)
    meta_ref[...] = _route(hb, w_router, b_router, run_ref)
    cnt_ref[...] = run_ref[...]


def _mixer_sample_kernel(x_ref, hist_ref, cnt_in_ref, g_mix, w_in, ln_g, ln_b, ws, bs_t, w_pool, pool_scale,
                         g_a, g_b, w_out, g_ffn, w_router, b_router,
                         x1_ref, hp_ref, meta_ref, vn_ref, p_ref, cnt_ref,
                         run_ref, a_ref, b_ref, ext_ref):
    n_streams, hist_rows, _ = hist_ref.shape
    t_new = x_ref.shape[0] // n_streams
    group = hist_rows + t_new
    run_ref[...] = cnt_in_ref[...]

    x = x_ref[...]
    u, vn, p = _mixer_front(x, g_mix, w_in, ln_g, ln_b)
    vn_ref[...] = vn
    p_ref[...] = p
    _spatial_gate(u, vn, ws, bs_t, t_new, a_ref)

    for b in range(n_streams):
        ext_ref[b * group:b * group + hist_rows, :] = hist_ref[b]
        ext_ref[b * group + hist_rows:(b + 1) * group, :] = p[b * t_new:(b + 1) * t_new]
    e = ext_ref[...]
    sums = _window_sums(e)
    diffs = []
    for g, w in enumerate(POOL_WINDOWS):
        c0 = g * POOL_GROUP_DIM
        d = sums[g] * (1.0 / w) - e[:, c0:c0 + POOL_GROUP_DIM]
        diffs.append(jnp.concatenate(
            [d[b * group + hist_rows:(b + 1) * group] for b in range(n_streams)], axis=0))
    _pool_project(diffs, w_pool, pool_scale, b_ref)

    x1, hb = _mixer_back(x, a_ref, b_ref, g_a, g_b, w_out, g_ffn)
    x1_ref[...] = x1
    hp_ref[...] = _pack_bf16_pairs(hb)
    meta_ref[...] = _route(hb, w_router, b_router, run_ref)
    cnt_ref[...] = run_ref[...]


def _const_spec(shape):
    zeros = (0,) * len(shape)
    return pl.BlockSpec(shape, lambda *_: zeros, pipeline_mode=pl.Buffered(1))


def _mixer_weight_specs():
    return [
        _const_spec((1, D_MODEL)),
        _const_spec((D_MODEL, 2 * D_A + D_B)),
        _const_spec((1, D_A)), _const_spec((1, D_A)),
        _const_spec((N_HEADS_A, GMLP_CHUNK, GMLP_CHUNK)),
        _const_spec((GMLP_CHUNK, N_HEADS_A)),
        _const_spec((N_POOL_GROUPS, POOL_GROUP_DIM, POOL_GROUP_DIM)),
        _const_spec((1, D_B)),
        _const_spec((1, D_A)), _const_spec((1, D_B)),
        _const_spec((D_A + D_B, D_MODEL)),
        _const_spec((1, D_MODEL)),
        _const_spec((D_MODEL, N_EXPERTS)),
        _const_spec((1, N_EXPERTS)),
    ]


def _mixer_prompt(x, weights):
    bsz, seq, _ = x.shape
    tm = TM_PROMPT
    nt = seq // tm
    return pl.pallas_call(
        _mixer_prompt_kernel,
        name="mixer_prompt",
        grid=(bsz, nt),
        in_specs=[pl.BlockSpec((1, tm, D_MODEL), lambda b, j: (b, j, 0))] + _mixer_weight_specs(),
        out_specs=[
            pl.BlockSpec((1, tm, D_MODEL), lambda b, j: (b, j, 0)),
            pl.BlockSpec((tm, D_MODEL // 2), lambda b, j: (b * nt + j, 0)),
            pl.BlockSpec((tm, LANES), lambda b, j: (b * nt + j, 0)),
            pl.BlockSpec((1, POOL_MAX, D_B), lambda b, j: (b, 0, 0)),
            pl.BlockSpec((8, LANES), lambda b, j: (0, 0)),
        ],
        out_shape=[
            jax.ShapeDtypeStruct((bsz, seq, D_MODEL), F32),
            jax.ShapeDtypeStruct((bsz * seq, D_MODEL // 2), jnp.int32),
            jax.ShapeDtypeStruct((bsz * seq, LANES), F32),
            jax.ShapeDtypeStruct((bsz, POOL_MAX, D_B), F32),
            jax.ShapeDtypeStruct((8, LANES), F32),
        ],
        scratch_shapes=[
            pltpu.VMEM((POOL_MAX, D_B), F32),
            pltpu.VMEM((8, LANES), F32),
            pltpu.VMEM((tm, D_A), F32),
            pltpu.VMEM((tm, D_B), F32),
        ],
        compiler_params=pltpu.CompilerParams(
            dimension_semantics=("arbitrary", "arbitrary"), vmem_limit_bytes=VMEM_LIMIT),
    )(x, *weights)


def _mixer_sample(x2d, hist, cnt_in, weights):
    rows = x2d.shape[0]
    n_streams, hist_rows, _ = hist.shape
    ext_rows = rows + n_streams * hist_rows
    full = lambda shape: pl.BlockSpec(shape, lambda i: (0,) * len(shape))
    return pl.pallas_call(
        _mixer_sample_kernel,
        name="mixer_sample",
        grid=(1,),
        in_specs=[full((rows, D_MODEL)), full(hist.shape), full((8, LANES))] + _mixer_weight_specs(),
        out_specs=[
            full((rows, D_MODEL)), full((rows, D_MODEL // 2)), full((rows, LANES)),
            full((rows, D_A)), full((rows, D_B)), full((8, LANES)),
        ],
        out_shape=[
            jax.ShapeDtypeStruct((rows, D_MODEL), F32),
            jax.ShapeDtypeStruct((rows, D_MODEL // 2), jnp.int32),
            jax.ShapeDtypeStruct((rows, LANES), F32),
            jax.ShapeDtypeStruct((rows, D_A), F32),
            jax.ShapeDtypeStruct((rows, D_B), F32),
            jax.ShapeDtypeStruct((8, LANES), F32),
        ],
        scratch_shapes=[
            pltpu.VMEM((8, LANES), F32),
            pltpu.VMEM((rows, D_A), F32),
            pltpu.VMEM((rows, D_B), F32),
            pltpu.VMEM((ext_rows, D_B), F32),
        ],
        compiler_params=pltpu.CompilerParams(
            dimension_semantics=("arbitrary",), vmem_limit_bytes=VMEM_LIMIT),
    )(x2d, hist, cnt_in, *weights)


def _dispatch_kernel(n_prompt_tiles, dest_ref, cnt_ref, region_ref, nused_ref,
                     hp_p_ref, hp_s_ref, xs_hbm, zbuf, sem, zsem):
    i = pl.program_id(0)
    tm_p = hp_p_ref.shape[0]
    tm_s = hp_s_ref.shape[0]
    n_blocks = xs_hbm.shape[0] // SLOT_BLOCK

    def pad_rows(e):
        cnt = cnt_ref[e]
        nblk = (cnt + SLOT_BLOCK - 1) // SLOT_BLOCK
        first_rows = cnt - (nblk - 1) * SLOT_BLOCK
        lo = region_ref[e] + first_rows
        mid = region_ref[e] + (first_rows + SUB_BLOCK - 1) // SUB_BLOCK * SUB_BLOCK
        hi = region_ref[e] + SLOT_BLOCK
        used = nblk > 0
        return lo, jnp.where(used, mid, lo), jnp.where(used, (hi - mid) // SUB_BLOCK, 0)

    def zero_row_copy(r):
        return pltpu.make_async_copy(zbuf.at[pl.ds(0, 1)], xs_hbm.at[pl.ds(r, 1)], zsem.at[0])

    def zero_sub_block_copy(r):
        return pltpu.make_async_copy(
            zbuf.at[pl.ds(0, SUB_BLOCK)], xs_hbm.at[pl.ds(pl.multiple_of(r, SUB_BLOCK), SUB_BLOCK)], zsem.at[1])

    def zero_block_copy(rb):
        return pltpu.make_async_copy(zbuf, xs_hbm.at[pl.ds(rb * SLOT_BLOCK, SLOT_BLOCK)], zsem.at[2])

    def for_each_fill(row_fn, sub_block_fn, block_fn):
        def per_expert(e, carry):
            lo, mid, n_sub = pad_rows(e)
            lax.fori_loop(lo, mid, lambda r, c: (row_fn(r), c)[1], 0)
            lax.fori_loop(0, n_sub, lambda j, c: (sub_block_fn(mid + j * SUB_BLOCK), c)[1], 0)
            return carry
        lax.fori_loop(0, N_EXPERTS, per_expert, 0)
        lax.fori_loop(nused_ref[0], n_blocks, lambda rb, c: (block_fn(rb), c)[1], 0)

    @pl.when(i == 0)
    def _():
        zbuf[...] = jnp.zeros_like(zbuf)
        for_each_fill(lambda r: zero_row_copy(r).start(), lambda r: zero_sub_block_copy(r).start(),
                      lambda rb: zero_block_copy(rb).start())

    def scatter(src_ref, first_token):
        rows = src_ref.shape[0]

        def body(t, carry):
            base = (first_token + t) * TOP_K
            for k in range(TOP_K):
                d = dest_ref[base + k]
                pltpu.make_async_copy(
                    src_ref.at[pl.ds(t, 1)], xs_hbm.at[pl.ds(d, 1)], sem).start(priority=k % 2)
            return carry
        lax.fori_loop(0, rows, body, 0, unroll=8)
        for _ in range(TOP_K):
            pltpu.make_async_copy(src_ref, xs_hbm.at[pl.ds(0, rows)], sem).wait()

    @pl.when(i < n_prompt_tiles)
    def _():
        scatter(hp_p_ref, i * tm_p)

    @pl.when(i >= n_prompt_tiles)
    def _():
        scatter(hp_s_ref, n_prompt_tiles * tm_p + (i - n_prompt_tiles) * tm_s)

    @pl.when(i == 0)
    def _():
        for_each_fill(lambda r: zero_row_copy(r).wait(), lambda r: zero_sub_block_copy(r).wait(),
                      lambda rb: zero_block_copy(rb).wait())


def _dispatch(dest, counts, region, n_used, hp_p, hp_s, n_blocks):
    tm_p = min(TM_DISPATCH, hp_p.shape[0])
    tm_s = min(TM_DISPATCH, hp_s.shape[0])
    assert hp_p.shape[0] % tm_p == 0 and hp_s.shape[0] % tm_s == 0
    ntp = hp_p.shape[0] // tm_p
    nts = hp_s.shape[0] // tm_s
    half = D_MODEL // 2
    return pl.pallas_call(
        functools.partial(_dispatch_kernel, ntp),
        name="dispatch",
        grid_spec=pltpu.PrefetchScalarGridSpec(
            num_scalar_prefetch=4,
            grid=(ntp + nts,),
            in_specs=[
                pl.BlockSpec((tm_p, half), lambda i, *_: (jnp.minimum(i, ntp - 1), 0)),
                pl.BlockSpec((tm_s, half), lambda i, *_: (jnp.maximum(i - ntp, 0), 0)),
            ],
            out_specs=pl.BlockSpec(memory_space=pl.ANY),
            scratch_shapes=[
                pltpu.VMEM((SLOT_BLOCK, half), jnp.int32),
                pltpu.SemaphoreType.DMA,
                pltpu.SemaphoreType.DMA((3,)),
            ],
        ),
        out_shape=jax.ShapeDtypeStruct((n_blocks * SLOT_BLOCK, half), jnp.int32),
        compiler_params=pltpu.CompilerParams(dimension_semantics=("arbitrary",)),
    )(dest, counts, region, n_used, hp_p, hp_s)


def _unpack_bf16_pairs(w):
    lo = lax.bitcast_convert_type(lax.shift_left(w, 16), F32).astype(BF16)
    hi = lax.bitcast_convert_type(w & jnp.int32(-65536), F32).astype(BF16)
    return jnp.concatenate([lo, hi], axis=1)


def _for_each_sub_block(nsub, compute, out_ref):
    n_sub_max = SLOT_BLOCK // SUB_BLOCK

    @pl.when(nsub == n_sub_max)
    def _():
        compute(pl.ds(0, SLOT_BLOCK))

    for s in range(n_sub_max):
        rows = pl.ds(s * SUB_BLOCK, SUB_BLOCK)

        @pl.when((s < nsub) & (nsub < n_sub_max))
        def _(rows=rows):
            compute(rows)

        @pl.when(s >= nsub)
        def _(rows=rows):
            out_ref[rows, :] = jnp.zeros((SUB_BLOCK, out_ref.shape[1]), out_ref.dtype)


def _grouped_call(name, step, grid, in_specs_fn, out_spec, operands, out_shape, scratch_shapes):
    n_in = len(operands)

    def outer(be_ref, first_ref, nsub_ref, *refs):
        ins, out, scratches = refs[:n_in], refs[n_in], refs[n_in + 1:]

        def body(*block_refs):
            step(first_ref, nsub_ref, *block_refs)

        pltpu.emit_pipeline(
            body, grid=grid, in_specs=in_specs_fn(be_ref), out_specs=[out_spec],
        )(*ins, out, scratches=scratches)

    any_spec = pl.BlockSpec(memory_space=pl.ANY)
    return lambda block_expert, first, nsub: pl.pallas_call(
        outer,
        name=name,
        grid_spec=pltpu.PrefetchScalarGridSpec(
            num_scalar_prefetch=3,
            grid=(),
            in_specs=[any_spec] * n_in,
            out_specs=any_spec,
            scratch_shapes=scratch_shapes,
        ),
        out_shape=out_shape,
        compiler_params=pltpu.CompilerParams(vmem_limit_bytes=VMEM_LIMIT),
    )(block_expert, first, nsub, *operands)


def _expert_weight_spec(be_ref, shape):
    return pl.BlockSpec(shape, lambda j, rb: (be_ref[rb], 0, j),
                        pipeline_mode=pl.Buffered(2, use_lookahead=True))


def _ffn_up_step(first_ref, nsub_ref, x_ref, wg_ref, bg_ref, wu_ref, bu_ref, act_ref, wgb, wub):
    rb = pl.program_id(1)

    @pl.when(first_ref[rb] == 1)
    def _():
        wgb[...] = wg_ref[0].astype(BF16)
        wub[...] = wu_ref[0].astype(BF16)

    def compute(rows):
        x = _unpack_bf16_pairs(x_ref[rows, :])
        gate = jnp.minimum(_dot(x, wgb[...]) + bg_ref[0], SWIGLU_LIMIT)
        up = jnp.clip(_dot(x, wub[...]) + bu_ref[0], -SWIGLU_LIMIT, SWIGLU_LIMIT)
        glu = gate * jax.nn.sigmoid(SWIGLU_ALPHA * gate)
        act_ref[rows, :] = ((up + 1.0) * glu).astype(BF16)

    _for_each_sub_block(nsub_ref[rb], compute, act_ref)


def _ffn_up(block_expert, first, nsub, xs, w_gate, b_gate, w_up, b_up, n_blocks):
    def in_specs(be_ref):
        w_spec = _expert_weight_spec(be_ref, (1, D_MODEL, FF_TILE))
        b_spec = pl.BlockSpec((1, 1, FF_TILE), lambda f, rb: (be_ref[rb], 0, f))
        return [pl.BlockSpec((SLOT_BLOCK, D_MODEL // 2), lambda f, rb: (rb, 0)), w_spec, b_spec, w_spec, b_spec]

    return _grouped_call(
        "ffn_up", _ffn_up_step, (D_FF // FF_TILE, n_blocks), in_specs,
        pl.BlockSpec((SLOT_BLOCK, FF_TILE), lambda f, rb: (rb, f)),
        (xs, w_gate, b_gate, w_up, b_up),
        jax.ShapeDtypeStruct((n_blocks * SLOT_BLOCK, D_FF), BF16),
        [pltpu.VMEM((D_MODEL, FF_TILE), BF16), pltpu.VMEM((D_MODEL, FF_TILE), BF16)],
    )(block_expert, first, nsub)


def _ffn_down_step(first_ref, nsub_ref, act_ref, wd_ref, bd_ref, y_ref, wdb):
    rb = pl.program_id(1)

    @pl.when(first_ref[rb] == 1)
    def _():
        wdb[...] = wd_ref[0].astype(BF16)

    def compute(rows):
        y = _dot(act_ref[rows, :], wdb[...]) + bd_ref[0]
        y_ref[rows, :] = _pack_bf16_pairs(y.astype(BF16))

    _for_each_sub_block(nsub_ref[rb], compute, y_ref)


def _ffn_down(block_expert, first, nsub, act, w_down, b_down, n_blocks):
    def in_specs(be_ref):
        return [
            pl.BlockSpec((SLOT_BLOCK, D_FF), lambda n, rb: (rb, 0)),
            _expert_weight_spec(be_ref, (1, D_FF, D_MODEL)),
            pl.BlockSpec((1, 1, D_MODEL), lambda n, rb: (be_ref[rb], 0, n)),
        ]

    return _grouped_call(
        "ffn_down", _ffn_down_step, (1, n_blocks), in_specs,
        pl.BlockSpec((SLOT_BLOCK, D_MODEL // 2), lambda n, rb: (rb, n)),
        (act, w_down, b_down),
        jax.ShapeDtypeStruct((n_blocks * SLOT_BLOCK, D_MODEL // 2), jnp.int32),
        [pltpu.VMEM((D_FF, D_MODEL), BF16)],
    )(block_expert, first, nsub)


def _combine_kernel(dest_ref, x1_ref, meta_ref, g_ref, ys_hbm, out_ref, ybuf, sem):
    i = pl.program_id(0)
    nt = pl.num_programs(0)
    tm = x1_ref.shape[0]

    def gather(tile, slot):
        def body(t, carry):
            for k in range(TOP_K):
                d = dest_ref[(tile * tm + t) * TOP_K + k]
                pltpu.make_async_copy(
                    ys_hbm.at[pl.ds(d, 1)], ybuf.at[slot, pl.ds(k * tm + t, 1)], sem.at[slot]).start(
                        priority=k % 2)
            return carry
        lax.fori_loop(0, tm, body, 0, unroll=8)

    @pl.when(i == 0)
    def _():
        gather(0, 0)

    @pl.when(i + 1 < nt)
    def _():
        gather(i + 1, (i + 1) % 2)

    slot = i % 2
    pltpu.make_async_copy(ys_hbm.at[pl.ds(0, TOP_K * tm)], ybuf.at[slot], sem.at[slot]).wait()
    half = D_MODEL // 2
    lo = x1_ref[:, 0:half]
    hi = x1_ref[:, half:D_MODEL]
    for k in range(TOP_K):
        w = ybuf[slot, k * tm:(k + 1) * tm, :]
        gate = meta_ref[:, TOP_K + k:TOP_K + k + 1]
        lo = lo + lax.bitcast_convert_type(lax.shift_left(w, 16), F32) * gate
        hi = hi + lax.bitcast_convert_type(w & jnp.int32(-65536), F32) * gate
    ms = (jnp.sum(lo * lo, axis=-1, keepdims=True) + jnp.sum(hi * hi, axis=-1, keepdims=True)) / D_MODEL
    scale = lax.rsqrt(ms + EPS)
    out_ref[:, 0:half] = lo * scale * g_ref[:, 0:half]
    out_ref[:, half:D_MODEL] = hi * scale * g_ref[:, half:D_MODEL]


def _combine(dest_flat, x1, meta, g_final, ys):
    n = x1.shape[0]
    tm = TM_COMBINE
    return pl.pallas_call(
        _combine_kernel,
        name="combine",
        grid_spec=pltpu.PrefetchScalarGridSpec(
            num_scalar_prefetch=1,
            grid=(n // tm,),
            in_specs=[
                pl.BlockSpec((tm, D_MODEL), lambda i, d: (i, 0)),
                pl.BlockSpec((tm, LANES), lambda i, d: (i, 0)),
                pl.BlockSpec((1, D_MODEL), lambda i, d: (0, 0)),
                pl.BlockSpec(memory_space=pl.ANY),
            ],
            out_specs=pl.BlockSpec((tm, D_MODEL), lambda i, d: (i, 0)),
            scratch_shapes=[
                pltpu.VMEM((2, TOP_K * tm, D_MODEL // 2), jnp.int32),
                pltpu.SemaphoreType.DMA((2,)),
            ],
        ),
        out_shape=jax.ShapeDtypeStruct((n, D_MODEL), F32),
        compiler_params=pltpu.CompilerParams(
            dimension_semantics=("arbitrary",), vmem_limit_bytes=VMEM_LIMIT),
    )(dest_flat, x1, meta, g_final, ys)


def kernel(x_prompt, x_sample, cache_pool, norm_mix_g, w_in, ln_v_g, ln_v_b, w_spatial, b_spatial, w_pool,
           pool_scale, out_norm_a_g, out_norm_b_g, w_out, norm_ffn_g, w_router, b_router, w_gate, b_gate,
           w_up, b_up, w_down, b_down, final_norm_g):
    depth = norm_mix_g.shape[0]
    assert depth == 1
    bsz, seq, _ = x_prompt.shape
    dec_b, dec_t, _ = x_sample.shape
    assert seq % TM_PROMPT == 0 and TM_PROMPT % GMLP_CHUNK == 0
    assert GMLP_CHUNK % dec_t == 0 and (dec_b * dec_t) % GMLP_CHUNK == 0 and PAST_LEN % GMLP_CHUNK == 0
    assert dec_t >= POOL_STATE and PAST_LEN + 1 >= POOL_MAX
    n_p = bsz * seq
    n_s = dec_b * dec_t
    n = n_p + n_s
    assert n_p % TM_COMBINE == 0 and n_s % TM_COMBINE == 0
    l = 0

    row = lambda v: v.reshape(1, -1)
    shared = [
        row(norm_mix_g[l]), w_in[l].astype(BF16), row(ln_v_g[l]), row(ln_v_b[l]),
    ]
    tail = [
        w_pool[l].astype(BF16), row(pool_scale[l]), row(out_norm_a_g[l]), row(out_norm_b_g[l]),
        w_out[l].astype(BF16), row(norm_ffn_g[l]), w_router[l].astype(BF16), row(b_router[l]),
    ]
    weights_p = shared + [w_spatial[l], b_spatial[l].T] + tail
    reps = GMLP_CHUNK // dec_t
    weights_s = shared + [
        jnp.tile(w_spatial[l][:, :dec_t, :dec_t], (1, reps, reps)),
        jnp.tile(b_spatial[l][:, :dec_t].T, (reps, 1)),
    ] + tail

    x1_p, hp_p, meta_p, ptail, cnt_p = _mixer_prompt(x_prompt, weights_p)
    hist = jnp.pad(cache_pool[l], ((0, 0), (POOL_MAX - POOL_STATE, 0), (0, 0)))
    x1_s, hp_s, meta_s, vn_s, p_s, cnt = _mixer_sample(
        x_sample.reshape(n_s, D_MODEL), hist, cnt_p, weights_s)

    i32 = jnp.int32
    counts = cnt[0, :N_EXPERTS].astype(i32)
    n_blocks = -(-(n * TOP_K + N_EXPERTS * (SLOT_BLOCK - 1)) // SLOT_BLOCK)
    blocks_e = (counts + SLOT_BLOCK - 1) // SLOT_BLOCK
    blk_end = jnp.cumsum(blocks_e)
    blk_start = blk_end - blocks_e
    region = blk_start * SLOT_BLOCK
    rows_first = counts - (blocks_e - 1) * SLOT_BLOCK
    n_used = blk_end[-1]
    experts = jnp.arange(N_EXPERTS, dtype=i32)

    def slot_index(meta_part):
        idx = meta_part[:, 0:TOP_K].astype(i32)
        rank = meta_part[:, 2 * TOP_K:3 * TOP_K].astype(i32)
        sel = idx[:, :, None] == experts
        reg = jnp.sum(jnp.where(sel, region, 0), axis=-1)
        rf = jnp.sum(jnp.where(sel, rows_first, 0), axis=-1)
        return (reg + jnp.where(rank < rf, rank, SLOT_BLOCK + rank - rf)).reshape(-1)

    dest_p = slot_index(meta_p)
    dest_s = slot_index(meta_s)
    blk = jnp.arange(n_blocks, dtype=i32)
    blk_c = jnp.minimum(blk, n_used - 1)
    block_expert = jnp.minimum(
        jnp.sum((blk_c[:, None] >= blk_end[None, :]).astype(i32), axis=-1), N_EXPERTS - 1)
    is_first = blk_c == blk_start[block_expert]
    used = blk < n_used
    first = (is_first & used).astype(i32)
    sub_first = (rows_first[block_expert] + SUB_BLOCK - 1) // SUB_BLOCK
    nsub = jnp.where(used, jnp.where(is_first, sub_first, SLOT_BLOCK // SUB_BLOCK), 0).astype(i32)

    xs = _dispatch(jnp.concatenate([dest_p, dest_s]), counts, region, n_used.reshape(1), hp_p, hp_s, n_blocks)
    act = _ffn_up(block_expert, first, nsub, xs, w_gate[l], b_gate[l].reshape(N_EXPERTS, 1, D_FF),
                  w_up[l], b_up[l].reshape(N_EXPERTS, 1, D_FF), n_blocks)
    ys = _ffn_down(block_expert, first, nsub, act, w_down[l], b_down[l].reshape(N_EXPERTS, 1, D_MODEL),
                   n_blocks)

    g_final = row(final_norm_g)
    y_p = _combine(dest_p, x1_p.reshape(n_p, D_MODEL), meta_p, g_final, ys)
    y_s = _combine(dest_s, x1_s, meta_s, g_final, ys)

    y_prompt = y_p.reshape(bsz, seq, D_MODEL)
    y_sample = y_s.reshape(dec_b, dec_t, D_MODEL)
    state_pool_prompt = ptail[:, POOL_MAX - POOL_STATE:][None]
    p_s3 = p_s.reshape(dec_b, dec_t, D_B)
    state_pool_sample = p_s3[:, dec_t - POOL_STATE:][None]
    state_chunk_v_sample = vn_s.reshape(dec_b, dec_t, D_A)[None]
    return (y_prompt, y_sample, state_pool_prompt, state_pool_sample, state_chunk_v_sample)
```

```python
import functools

import jax
import jax.numpy as jnp
from jax import lax
from jax.experimental import pallas as pl
from jax.experimental.pallas import tpu as pltpu

D_MODEL = 2048
D_A = 1024
D_B = 1024
GMLP_CHUNK = 128
N_HEADS_A = 8
HEAD_DIM_A = D_A // N_HEADS_A
POOL_WINDOWS = (2, 4, 8, 16)
N_POOL_GROUPS = len(POOL_WINDOWS)
POOL_GROUP_DIM = D_B // N_POOL_GROUPS
POOL_MAX = 16
POOL_STATE = POOL_MAX - 1
PAST_LEN = 1024
N_EXPERTS = 32
TOP_K = 4
D_FF = 2048
SWIGLU_LIMIT = 7.0
SWIGLU_ALPHA = 1.702
EPS = 1e-5

LANES = 128
SUBLANES = 8
TM_PROMPT = 256
SLOT_BLOCK = 512
SUB_BLOCK = 128
TM_DISPATCH = 1024
FF_TILE = 1024
TM_COMBINE = 128
VMEM_LIMIT = 56 * 1024 * 1024

BF16 = jnp.bfloat16
F32 = jnp.float32


def _dot(a, b):
    return jnp.dot(a, b, preferred_element_type=F32)


def _rms_norm(x, g):
    return x * lax.rsqrt(jnp.mean(x * x, axis=-1, keepdims=True) + EPS) * g


def _layer_norm(x, g, b):
    mu = jnp.mean(x, axis=-1, keepdims=True)
    xc = x - mu
    return xc * lax.rsqrt(jnp.mean(xc * xc, axis=-1, keepdims=True) + EPS) * g + b


def _gelu(x):
    return 0.5 * x * (1.0 + lax.erf(x * (2.0 ** -0.5)))


def _mixer_front(x, g_mix, w_in, ln_g, ln_b):
    h = _rms_norm(x, g_mix[...]).astype(BF16)
    u = _gelu(_dot(h, w_in[:, 0:D_A]))
    vn = _layer_norm(_dot(h, w_in[:, D_A:2 * D_A]), ln_g[...], ln_b[...])
    p = _dot(h, w_in[:, 2 * D_A:2 * D_A + D_B])
    return u, vn, p


def _spatial_gate(u, vn, ws, bs_t, chunk, a_ref):
    rows = u.shape[0]
    t = lax.broadcasted_iota(jnp.int32, (GMLP_CHUNK, GMLP_CHUNK), 0)
    s = lax.broadcasted_iota(jnp.int32, (GMLP_CHUNK, GMLP_CHUNK), 1)
    mask = (s <= t) & ((t // chunk) == (s // chunk))
    vb = vn.astype(BF16)
    for h in range(N_HEADS_A):
        w = jnp.where(mask, ws[h], 0.0).astype(BF16)
        bias = bs_t[:, h:h + 1]
        c0 = h * HEAD_DIM_A
        for r0 in range(0, rows, GMLP_CHUNK):
            mixed = _dot(w, vb[r0:r0 + GMLP_CHUNK, c0:c0 + HEAD_DIM_A]) + bias
            a_ref[r0:r0 + GMLP_CHUNK, c0:c0 + HEAD_DIM_A] = u[r0:r0 + GMLP_CHUNK, c0:c0 + HEAD_DIM_A] * mixed


def _window_sums(e):
    parts = []
    s = e
    for g, w in enumerate(POOL_WINDOWS):
        s = s + pltpu.roll(s, w // 2, 0)
        parts.append(s[:, 0:POOL_GROUP_DIM])
        if g + 1 < N_POOL_GROUPS:
            s = s[:, POOL_GROUP_DIM:]
    return parts


def _pool_project(diffs, w_pool, pool_scale, b_ref):
    for g in range(N_POOL_GROUPS):
        c0 = g * POOL_GROUP_DIM
        out = _dot(diffs[g].astype(BF16), w_pool[g])
        b_ref[:, c0:c0 + POOL_GROUP_DIM] = out * pool_scale[:, c0:c0 + POOL_GROUP_DIM]


def _mixer_back(x, a_ref, b_ref, g_a, g_b, w_out, g_ffn):
    na = _rms_norm(a_ref[...], g_a[...]).astype(BF16)
    nb = _rms_norm(b_ref[...], g_b[...]).astype(BF16)
    x1 = x + _dot(na, w_out[0:D_A, :]) + _dot(nb, w_out[D_A:D_A + D_B, :])
    hb = _rms_norm(x1, g_ffn[...]).astype(BF16)
    return x1, hb


def _pack_bf16_pairs(hb):
    half = D_MODEL // 2
    lo = lax.bitcast_convert_type(hb[:, 0:half].astype(F32), jnp.int32)
    hi = lax.bitcast_convert_type(hb[:, half:D_MODEL].astype(F32), jnp.int32)
    return lax.shift_right_logical(lo, 16) | hi


def _route(hb, w_router, b_router, run_ref):
    rows = hb.shape[0]
    logits = _dot(hb, w_router[...]) + b_router[...]
    lane = lax.broadcasted_iota(jnp.int32, (rows, N_EXPERTS), 1)
    vals, idxs = [], []
    l = logits
    for _ in range(TOP_K):
        m = jnp.max(l, axis=-1, keepdims=True)
        i = jnp.min(jnp.where(l == m, lane, N_EXPERTS), axis=-1, keepdims=True)
        vals.append(m)
        idxs.append(i)
        l = jnp.where(lane == i, -jnp.inf, l)
    exps = [jnp.exp(v - vals[0]) for v in vals]
    denom = exps[0] + exps[1] + exps[2] + exps[3]
    gates = [e / denom for e in exps]

    onehot = jnp.zeros((rows, N_EXPERTS), F32)
    for i in idxs:
        onehot = jnp.where(lane == i, 1.0, onehot)
    r = lax.broadcasted_iota(jnp.int32, (rows, rows), 0)
    c = lax.broadcasted_iota(jnp.int32, (rows, rows), 1)
    before = jnp.where(c < r, 1.0, 0.0).astype(BF16)
    base = _dot(before, onehot.astype(BF16)) + run_ref[0:1, 0:N_EXPERTS]
    ranks = [jnp.sum(jnp.where(lane == i, base, 0.0), axis=-1, keepdims=True) for i in idxs]
    run_ref[0:1, 0:N_EXPERTS] = run_ref[0:1, 0:N_EXPERTS] + jnp.sum(onehot, axis=0, keepdims=True)

    lane_m = lax.broadcasted_iota(jnp.int32, (rows, LANES), 1)
    meta = jnp.zeros((rows, LANES), F32)
    for j, col in enumerate(gates):
        meta = jnp.where(lane_m == j, col, meta)
    lane_r = lax.broadcasted_iota(jnp.int32, (rows, 2 * TOP_K), 1)
    route = jnp.zeros((rows, 2 * TOP_K), jnp.int32)
    for j, col in enumerate(idxs + [r.astype(jnp.int32) for r in ranks]):
        route = jnp.where(lane_r == j, col, route)
    return meta, route


def _mixer_prompt_kernel(x_ref, g_mix, w_in, ln_g, ln_b, ws, bs_t, w_pool, pool_scale, g_a, g_b, w_out,
                         g_ffn, w_router, b_router,
                         x1_ref, hp_ref, meta_ref, route_ref, ptail_ref, cnt_ref,
                         halo_ref, run_ref, a_ref, b_ref):
    tm = x_ref.shape[1]
    j = pl.program_id(1)

    @pl.when((pl.program_id(0) == 0) & (j == 0))
    def _():
        run_ref[...] = jnp.zeros_like(run_ref)

    @pl.when(j == 0)
    def _():
        halo_ref[...] = jnp.zeros_like(halo_ref)

    x = x_ref[0]
    u, vn, p = _mixer_front(x, g_mix, w_in, ln_g, ln_b)
    _spatial_gate(u, vn, ws, bs_t, GMLP_CHUNK, a_ref)

    e = jnp.concatenate([halo_ref[...], p], axis=0)
    halo_ref[...] = p[tm - POOL_MAX:tm]
    ptail_ref[0] = p[tm - POOL_MAX:tm]
    pos = j * tm + lax.broadcasted_iota(jnp.int32, (tm, 1), 0)
    sums = _window_sums(e)
    diffs = []
    for g, w in enumerate(POOL_WINDOWS):
        inv_cnt = 1.0 / jnp.minimum(pos + 1, w).astype(F32)
        c0 = g * POOL_GROUP_DIM
        diffs.append(sums[g][POOL_MAX:] * inv_cnt - p[:, c0:c0 + POOL_GROUP_DIM])
    _pool_project(diffs, w_pool, pool_scale, b_ref)

    x1, hb = _mixer_back(x, a_ref, b_ref, g_a, g_b, w_out, g_ffn)
    x1_ref[0] = x1
    hp_ref[...] = _pack_bf16_pairs(hb)
    meta_ref[...], route_ref[...] = _route(hb, w_router, b_router, run_ref)
    cnt_ref[...] = run_ref[...]


def _mixer_sample_kernel(x_ref, hist_ref, cnt_in_ref, g_mix, w_in, ln_g, ln_b, ws, bs_t, w_pool, pool_scale,
                         g_a, g_b, w_out, g_ffn, w_router, b_router,
                         x1_ref, hp_ref, meta_ref, route_ref, vn_ref, p_ref, cnt_ref,
                         run_ref, a_ref, b_ref, ext_ref):
    n_streams, hist_rows, _ = hist_ref.shape
    t_new = x_ref.shape[0] // n_streams
    group = hist_rows + t_new
    run_ref[...] = cnt_in_ref[...]

    x = x_ref[...]
    u, vn, p = _mixer_front(x, g_mix, w_in, ln_g, ln_b)
    vn_ref[...] = vn
    p_ref[...] = p
    _spatial_gate(u, vn, ws, bs_t, t_new, a_ref)

    for b in range(n_streams):
        ext_ref[b * group:b * group + hist_rows, :] = hist_ref[b]
        ext_ref[b * group + hist_rows:(b + 1) * group, :] = p[b * t_new:(b + 1) * t_new]
    e = ext_ref[...]
    sums = _window_sums(e)
    diffs = []
    for g, w in enumerate(POOL_WINDOWS):
        c0 = g * POOL_GROUP_DIM
        d = sums[g] * (1.0 / w) - e[:, c0:c0 + POOL_GROUP_DIM]
        diffs.append(jnp.concatenate(
            [d[b * group + hist_rows:(b + 1) * group] for b in range(n_streams)], axis=0))
    _pool_project(diffs, w_pool, pool_scale, b_ref)

    x1, hb = _mixer_back(x, a_ref, b_ref, g_a, g_b, w_out, g_ffn)
    x1_ref[...] = x1
    hp_ref[...] = _pack_bf16_pairs(hb)
    meta_ref[...], route_ref[...] = _route(hb, w_router, b_router, run_ref)
    cnt_ref[...] = run_ref[...]


def _const_spec(shape):
    zeros = (0,) * len(shape)
    return pl.BlockSpec(shape, lambda *_: zeros, pipeline_mode=pl.Buffered(1))


def _mixer_weight_specs():
    return [
        _const_spec((1, D_MODEL)),
        _const_spec((D_MODEL, 2 * D_A + D_B)),
        _const_spec((1, D_A)), _const_spec((1, D_A)),
        _const_spec((N_HEADS_A, GMLP_CHUNK, GMLP_CHUNK)),
        _const_spec((GMLP_CHUNK, N_HEADS_A)),
        _const_spec((N_POOL_GROUPS, POOL_GROUP_DIM, POOL_GROUP_DIM)),
        _const_spec((1, D_B)),
        _const_spec((1, D_A)), _const_spec((1, D_B)),
        _const_spec((D_A + D_B, D_MODEL)),
        _const_spec((1, D_MODEL)),
        _const_spec((D_MODEL, N_EXPERTS)),
        _const_spec((1, N_EXPERTS)),
    ]


def _mixer_prompt(x, weights):
    bsz, seq, _ = x.shape
    tm = TM_PROMPT
    nt = seq // tm
    return pl.pallas_call(
        _mixer_prompt_kernel,
        name="mixer_prompt",
        grid=(bsz, nt),
        in_specs=[pl.BlockSpec((1, tm, D_MODEL), lambda b, j: (b, j, 0))] + _mixer_weight_specs(),
        out_specs=[
            pl.BlockSpec((1, tm, D_MODEL), lambda b, j: (b, j, 0)),
            pl.BlockSpec((tm, D_MODEL // 2), lambda b, j: (b * nt + j, 0)),
            pl.BlockSpec((tm, LANES), lambda b, j: (b * nt + j, 0)),
            pl.BlockSpec((tm, 2 * TOP_K), lambda b, j: (b * nt + j, 0)),
            pl.BlockSpec((1, POOL_MAX, D_B), lambda b, j: (b, 0, 0)),
            pl.BlockSpec((8, LANES), lambda b, j: (0, 0)),
        ],
        out_shape=[
            jax.ShapeDtypeStruct((bsz, seq, D_MODEL), F32),
            jax.ShapeDtypeStruct((bsz * seq, D_MODEL // 2), jnp.int32),
            jax.ShapeDtypeStruct((bsz * seq, LANES), F32),
            jax.ShapeDtypeStruct((bsz * seq, 2 * TOP_K), jnp.int32),
            jax.ShapeDtypeStruct((bsz, POOL_MAX, D_B), F32),
            jax.ShapeDtypeStruct((8, LANES), F32),
        ],
        scratch_shapes=[
            pltpu.VMEM((POOL_MAX, D_B), F32),
            pltpu.VMEM((8, LANES), F32),
            pltpu.VMEM((tm, D_A), F32),
            pltpu.VMEM((tm, D_B), F32),
        ],
        compiler_params=pltpu.CompilerParams(
            dimension_semantics=("arbitrary", "arbitrary"), vmem_limit_bytes=VMEM_LIMIT),
    )(x, *weights)


def _mixer_sample(x2d, hist, cnt_in, weights):
    rows = x2d.shape[0]
    n_streams, hist_rows, _ = hist.shape
    ext_rows = rows + n_streams * hist_rows
    full = lambda shape: pl.BlockSpec(shape, lambda i: (0,) * len(shape))
    return pl.pallas_call(
        _mixer_sample_kernel,
        name="mixer_sample",
        grid=(1,),
        in_specs=[full((rows, D_MODEL)), full(hist.shape), full((8, LANES))] + _mixer_weight_specs(),
        out_specs=[
            full((rows, D_MODEL)), full((rows, D_MODEL // 2)), full((rows, LANES)), full((rows, 2 * TOP_K)),
            full((rows, D_A)), full((rows, D_B)), full((8, LANES)),
        ],
        out_shape=[
            jax.ShapeDtypeStruct((rows, D_MODEL), F32),
            jax.ShapeDtypeStruct((rows, D_MODEL // 2), jnp.int32),
            jax.ShapeDtypeStruct((rows, LANES), F32),
            jax.ShapeDtypeStruct((rows, 2 * TOP_K), jnp.int32),
            jax.ShapeDtypeStruct((rows, D_A), F32),
            jax.ShapeDtypeStruct((rows, D_B), F32),
            jax.ShapeDtypeStruct((8, LANES), F32),
        ],
        scratch_shapes=[
            pltpu.VMEM((8, LANES), F32),
            pltpu.VMEM((rows, D_A), F32),
            pltpu.VMEM((rows, D_B), F32),
            pltpu.VMEM((ext_rows, D_B), F32),
        ],
        compiler_params=pltpu.CompilerParams(
            dimension_semantics=("arbitrary",), vmem_limit_bytes=VMEM_LIMIT),
    )(x2d, hist, cnt_in, *weights)


def _dispatch_kernel(n_prompt_tiles, dest_ref, cnt_ref, region_ref, nused_ref,
                     hp_p_ref, hp_s_ref, xs_hbm, zbuf, sem, zsem):
    i = pl.program_id(0)
    tm_p = hp_p_ref.shape[0]
    tm_s = hp_s_ref.shape[0]
    n_blocks = xs_hbm.shape[0] // SLOT_BLOCK

    def pad_rows(e):
        cnt = cnt_ref[e]
        lo = region_ref[e] + cnt
        mid = region_ref[e] + (cnt + SUB_BLOCK - 1) // SUB_BLOCK * SUB_BLOCK
        hi = region_ref[e] + (cnt + SLOT_BLOCK - 1) // SLOT_BLOCK * SLOT_BLOCK
        return lo, mid, (hi - mid) // SUB_BLOCK

    def zero_row_copy(r):
        return pltpu.make_async_copy(zbuf.at[pl.ds(0, 1)], xs_hbm.at[pl.ds(r, 1)], zsem.at[0])

    def zero_sub_block_copy(r):
        return pltpu.make_async_copy(
            zbuf.at[pl.ds(0, SUB_BLOCK)], xs_hbm.at[pl.ds(pl.multiple_of(r, SUB_BLOCK), SUB_BLOCK)], zsem.at[1])

    def zero_block_copy(rb):
        return pltpu.make_async_copy(zbuf, xs_hbm.at[pl.ds(rb * SLOT_BLOCK, SLOT_BLOCK)], zsem.at[2])

    def for_each_fill(row_fn, sub_block_fn, block_fn):
        def per_expert(e, carry):
            lo, mid, n_sub = pad_rows(e)
            lax.fori_loop(lo, mid, lambda r, c: (row_fn(r), c)[1], 0)
            lax.fori_loop(0, n_sub, lambda j, c: (sub_block_fn(mid + j * SUB_BLOCK), c)[1], 0)
            return carry
        lax.fori_loop(0, N_EXPERTS, per_expert, 0)
        lax.fori_loop(nused_ref[0], n_blocks, lambda rb, c: (block_fn(rb), c)[1], 0)

    @pl.when(i == 0)
    def _():
        zbuf[...] = jnp.zeros_like(zbuf)
        for_each_fill(lambda r: zero_row_copy(r).start(), lambda r: zero_sub_block_copy(r).start(),
                      lambda rb: zero_block_copy(rb).start())

    def scatter(src_ref, first_token):
        rows = src_ref.shape[0]

        def body(t, carry):
            base = (first_token + t) * TOP_K
            for k in range(TOP_K):
                d = dest_ref[base + k]
                pltpu.make_async_copy(
                    src_ref.at[pl.ds(t, 1)], xs_hbm.at[pl.ds(d, 1)], sem).start(priority=k % 2)
            return carry
        lax.fori_loop(0, rows, body, 0, unroll=8)
        for _ in range(TOP_K):
            pltpu.make_async_copy(src_ref, xs_hbm.at[pl.ds(0, rows)], sem).wait()

    @pl.when(i < n_prompt_tiles)
    def _():
        scatter(hp_p_ref, i * tm_p)

    @pl.when(i >= n_prompt_tiles)
    def _():
        scatter(hp_s_ref, n_prompt_tiles * tm_p + (i - n_prompt_tiles) * tm_s)

    @pl.when(i == 0)
    def _():
        for_each_fill(lambda r: zero_row_copy(r).wait(), lambda r: zero_sub_block_copy(r).wait(),
                      lambda rb: zero_block_copy(rb).wait())


def _dispatch(dest, counts, region, n_used, hp_p, hp_s, n_blocks):
    tm_p = min(TM_DISPATCH, hp_p.shape[0])
    tm_s = min(TM_DISPATCH, hp_s.shape[0])
    assert hp_p.shape[0] % tm_p == 0 and hp_s.shape[0] % tm_s == 0
    ntp = hp_p.shape[0] // tm_p
    nts = hp_s.shape[0] // tm_s
    half = D_MODEL // 2
    return pl.pallas_call(
        functools.partial(_dispatch_kernel, ntp),
        name="dispatch",
        grid_spec=pltpu.PrefetchScalarGridSpec(
            num_scalar_prefetch=4,
            grid=(ntp + nts,),
            in_specs=[
                pl.BlockSpec((tm_p, half), lambda i, *_: (jnp.minimum(i, ntp - 1), 0)),
                pl.BlockSpec((tm_s, half), lambda i, *_: (jnp.maximum(i - ntp, 0), 0)),
            ],
            out_specs=pl.BlockSpec(memory_space=pl.ANY),
            scratch_shapes=[
                pltpu.VMEM((SLOT_BLOCK, half), jnp.int32),
                pltpu.SemaphoreType.DMA,
                pltpu.SemaphoreType.DMA((3,)),
            ],
        ),
        out_shape=jax.ShapeDtypeStruct((n_blocks * SLOT_BLOCK, half), jnp.int32),
        compiler_params=pltpu.CompilerParams(dimension_semantics=("arbitrary",)),
    )(dest, counts, region, n_used, hp_p, hp_s)


def _unpack_bf16_pairs(w):
    lo = lax.bitcast_convert_type(lax.shift_left(w, 16), F32).astype(BF16)
    hi = lax.bitcast_convert_type(w & jnp.int32(-65536), F32).astype(BF16)
    return jnp.concatenate([lo, hi], axis=1)


def _for_each_sub_block(nsub, compute, out_ref):
    n_sub_max = SLOT_BLOCK // SUB_BLOCK

    @pl.when(nsub == n_sub_max)
    def _():
        compute(pl.ds(0, SLOT_BLOCK))

    for s in range(n_sub_max):
        rows = pl.ds(s * SUB_BLOCK, SUB_BLOCK)

        @pl.when((s < nsub) & (nsub < n_sub_max))
        def _(rows=rows):
            compute(rows)

        @pl.when(s >= nsub)
        def _(rows=rows):
            out_ref[rows, :] = jnp.zeros((SUB_BLOCK, out_ref.shape[1]), out_ref.dtype)


def _grouped_call(name, step, grid, in_specs_fn, out_spec, operands, out_shape, scratch_shapes):
    n_in = len(operands)

    def outer(be_ref, first_ref, nsub_ref, *refs):
        ins, out, scratches = refs[:n_in], refs[n_in], refs[n_in + 1:]

        def body(*block_refs):
            step(be_ref, first_ref, nsub_ref, *block_refs)

        pltpu.emit_pipeline(
            body, grid=grid, in_specs=in_specs_fn(be_ref), out_specs=[out_spec],
        )(*ins, out, scratches=scratches)

    any_spec = pl.BlockSpec(memory_space=pl.ANY)
    return lambda block_expert, first, nsub: pl.pallas_call(
        outer,
        name=name,
        grid_spec=pltpu.PrefetchScalarGridSpec(
            num_scalar_prefetch=3,
            grid=(),
            in_specs=[any_spec] * n_in,
            out_specs=any_spec,
            scratch_shapes=scratch_shapes,
        ),
        out_shape=out_shape,
        compiler_params=pltpu.CompilerParams(vmem_limit_bytes=VMEM_LIMIT),
    )(block_expert, first, nsub, *operands)


def _expert_weight_spec(be_ref, shape):
    return pl.BlockSpec(shape, lambda j, rb: (be_ref[rb], 0, j),
                        pipeline_mode=pl.Buffered(2, use_lookahead=True))


def _ffn_up_step(be_ref, first_ref, nsub_ref, x_ref, wg_ref, bg_ref, wu_ref, bu_ref, act_ref, wgb, wub):
    rb = pl.program_id(1)
    expert = pl.ds(be_ref[rb], 1)

    @pl.when(first_ref[rb] == 1)
    def _():
        wgb[...] = wg_ref[0].astype(BF16)
        wub[...] = wu_ref[0].astype(BF16)

    def compute(rows):
        x = _unpack_bf16_pairs(x_ref[rows, :])
        gate = jnp.minimum(_dot(x, wgb[...]) + bg_ref[expert, :], SWIGLU_LIMIT)
        up = jnp.clip(_dot(x, wub[...]) + bu_ref[expert, :], -SWIGLU_LIMIT, SWIGLU_LIMIT)
        glu = gate * jax.nn.sigmoid(SWIGLU_ALPHA * gate)
        act_ref[rows, :] = ((up + 1.0) * glu).astype(BF16)

    _for_each_sub_block(nsub_ref[rb], compute, act_ref)


def _ffn_up(block_expert, first, nsub, xs, w_gate, b_gate, w_up, b_up, n_blocks):
    def in_specs(be_ref):
        w_spec = _expert_weight_spec(be_ref, (1, D_MODEL, FF_TILE))
        b_spec = pl.BlockSpec((N_EXPERTS, FF_TILE), lambda f, rb: (0, f))
        return [pl.BlockSpec((SLOT_BLOCK, D_MODEL // 2), lambda f, rb: (rb, 0)), w_spec, b_spec, w_spec, b_spec]

    return _grouped_call(
        "ffn_up", _ffn_up_step, (D_FF // FF_TILE, n_blocks), in_specs,
        pl.BlockSpec((SLOT_BLOCK, FF_TILE), lambda f, rb: (rb, f)),
        (xs, w_gate, b_gate, w_up, b_up),
        jax.ShapeDtypeStruct((n_blocks * SLOT_BLOCK, D_FF), BF16),
        [pltpu.VMEM((D_MODEL, FF_TILE), BF16), pltpu.VMEM((D_MODEL, FF_TILE), BF16)],
    )(block_expert, first, nsub)


def _ffn_down_step(be_ref, first_ref, nsub_ref, act_ref, wd_ref, bd_ref, y_ref, wdb):
    rb = pl.program_id(1)
    expert = pl.ds(be_ref[rb], 1)

    @pl.when(first_ref[rb] == 1)
    def _():
        wdb[...] = wd_ref[0].astype(BF16)

    def compute(rows):
        y = _dot(act_ref[rows, :], wdb[...]) + bd_ref[expert, :]
        y_ref[rows, :] = _pack_bf16_pairs(y.astype(BF16))

    _for_each_sub_block(nsub_ref[rb], compute, y_ref)


def _ffn_down(block_expert, first, nsub, act, w_down, b_down, n_blocks):
    def in_specs(be_ref):
        return [
            pl.BlockSpec((SLOT_BLOCK, D_FF), lambda n, rb: (rb, 0)),
            _expert_weight_spec(be_ref, (1, D_FF, D_MODEL)),
            pl.BlockSpec((N_EXPERTS, D_MODEL), lambda n, rb: (0, n)),
        ]

    return _grouped_call(
        "ffn_down", _ffn_down_step, (1, n_blocks), in_specs,
        pl.BlockSpec((SLOT_BLOCK, D_MODEL // 2), lambda n, rb: (rb, n)),
        (act, w_down, b_down),
        jax.ShapeDtypeStruct((n_blocks * SLOT_BLOCK, D_MODEL // 2), jnp.int32),
        [pltpu.VMEM((D_FF, D_MODEL), BF16)],
    )(block_expert, first, nsub)


def _combine_kernel(dest_ref, x1_ref, meta_ref, g_ref, ys_hbm, out_ref, ybuf0, ybuf1, sem0, sem1):
    i = pl.program_id(0)
    n_steps = pl.num_programs(0)
    tm = TM_COMBINE
    half = D_MODEL // 2

    def gather(tile, ybuf, sem):
        def body(grp, carry):
            for j in range(SUBLANES):
                base = (tile * tm + grp * SUBLANES + j) * TOP_K
                for k in range(TOP_K):
                    d = dest_ref[base + k]
                    pltpu.make_async_copy(
                        ys_hbm.at[lax.shift_right_logical(d, 3), pl.ds(d & (SUBLANES - 1), 1)],
                        ybuf.at[k, grp, pl.ds(j, 1)], sem).start(priority=k % 2)
            return carry
        lax.fori_loop(0, tm // SUBLANES, body, 0)

    def finish(r0, ybuf, sem):
        for k in range(TOP_K):
            pltpu.make_async_copy(ys_hbm.at[pl.ds(0, tm // SUBLANES)], ybuf.at[k], sem).wait()
        lo = x1_ref[r0:r0 + tm, 0:half]
        hi = x1_ref[r0:r0 + tm, half:D_MODEL]
        for k in range(TOP_K):
            w = ybuf[k].reshape(tm, half)
            gate = meta_ref[r0:r0 + tm, k:k + 1]
            lo = lo + lax.bitcast_convert_type(lax.shift_left(w, 16), F32) * gate
            hi = hi + lax.bitcast_convert_type(w & jnp.int32(-65536), F32) * gate
        ms = (jnp.sum(lo * lo, axis=-1, keepdims=True) + jnp.sum(hi * hi, axis=-1, keepdims=True)) / D_MODEL
        scale = lax.rsqrt(ms + EPS)
        out_ref[r0:r0 + tm, 0:half] = lo * scale * g_ref[:, 0:half]
        out_ref[r0:r0 + tm, half:D_MODEL] = hi * scale * g_ref[:, half:D_MODEL]

    @pl.when(i == 0)
    def _():
        gather(0, ybuf0, sem0)

    gather(2 * i + 1, ybuf1, sem1)
    finish(0, ybuf0, sem0)

    @pl.when(i + 1 < n_steps)
    def _():
        gather(2 * i + 2, ybuf0, sem0)

    finish(tm, ybuf1, sem1)


def _combine(dest_flat, x1, meta, g_final, ys):
    n = x1.shape[0]
    tm = 2 * TM_COMBINE
    assert n % tm == 0
    ybuf = pltpu.VMEM((TOP_K, TM_COMBINE // SUBLANES, SUBLANES, D_MODEL // 2), jnp.int32)
    ys = ys.reshape(ys.shape[0] // SUBLANES, SUBLANES, D_MODEL // 2)
    return pl.pallas_call(
        _combine_kernel,
        name="combine",
        grid_spec=pltpu.PrefetchScalarGridSpec(
            num_scalar_prefetch=1,
            grid=(n // tm,),
            in_specs=[
                pl.BlockSpec((tm, D_MODEL), lambda i, d: (i, 0)),
                pl.BlockSpec((tm, LANES), lambda i, d: (i, 0)),
                pl.BlockSpec((1, D_MODEL), lambda i, d: (0, 0)),
                pl.BlockSpec(memory_space=pl.ANY),
            ],
            out_specs=pl.BlockSpec((tm, D_MODEL), lambda i, d: (i, 0)),
            scratch_shapes=[ybuf, ybuf, pltpu.SemaphoreType.DMA, pltpu.SemaphoreType.DMA],
        ),
        out_shape=jax.ShapeDtypeStruct((n, D_MODEL), F32),
        compiler_params=pltpu.CompilerParams(
            dimension_semantics=("arbitrary",), vmem_limit_bytes=VMEM_LIMIT),
    )(dest_flat, x1, meta, g_final, ys)


def kernel(x_prompt, x_sample, cache_pool, norm_mix_g, w_in, ln_v_g, ln_v_b, w_spatial, b_spatial, w_pool,
           pool_scale, out_norm_a_g, out_norm_b_g, w_out, norm_ffn_g, w_router, b_router, w_gate, b_gate,
           w_up, b_up, w_down, b_down, final_norm_g):
    depth = norm_mix_g.shape[0]
    assert depth == 1
    bsz, seq, _ = x_prompt.shape
    dec_b, dec_t, _ = x_sample.shape
    assert seq % TM_PROMPT == 0 and TM_PROMPT % GMLP_CHUNK == 0
    assert GMLP_CHUNK % dec_t == 0 and (dec_b * dec_t) % GMLP_CHUNK == 0 and PAST_LEN % GMLP_CHUNK == 0
    assert dec_t >= POOL_STATE and PAST_LEN + 1 >= POOL_MAX
    n_p = bsz * seq
    n_s = dec_b * dec_t
    n = n_p + n_s
    assert n_p % TM_COMBINE == 0 and n_s % TM_COMBINE == 0
    l = 0

    row = lambda v: v.reshape(1, -1)
    shared = [
        row(norm_mix_g[l]), w_in[l].astype(BF16), row(ln_v_g[l]), row(ln_v_b[l]),
    ]
    tail = [
        w_pool[l].astype(BF16), row(pool_scale[l]), row(out_norm_a_g[l]), row(out_norm_b_g[l]),
        w_out[l].astype(BF16), row(norm_ffn_g[l]), w_router[l].astype(BF16), row(b_router[l]),
    ]
    weights_p = shared + [w_spatial[l], b_spatial[l].T] + tail
    reps = GMLP_CHUNK // dec_t
    weights_s = shared + [
        jnp.tile(w_spatial[l][:, :dec_t, :dec_t], (1, reps, reps)),
        jnp.tile(b_spatial[l][:, :dec_t].T, (reps, 1)),
    ] + tail

    x1_p, hp_p, meta_p, route_p, ptail, cnt_p = _mixer_prompt(x_prompt, weights_p)
    hist = jnp.pad(cache_pool[l], ((0, 0), (POOL_MAX - POOL_STATE, 0), (0, 0)))
    x1_s, hp_s, meta_s, route_s, vn_s, p_s, cnt = _mixer_sample(
        x_sample.reshape(n_s, D_MODEL), hist, cnt_p, weights_s)

    i32 = jnp.int32
    counts = cnt[0, :N_EXPERTS].astype(i32)
    n_blocks = -(-(n * TOP_K + N_EXPERTS * (SLOT_BLOCK - 1)) // SLOT_BLOCK)
    blocks_e = (counts + SLOT_BLOCK - 1) // SLOT_BLOCK
    blk_end = jnp.cumsum(blocks_e)
    blk_start = blk_end - blocks_e
    region = blk_start * SLOT_BLOCK
    rows_last = counts - (blocks_e - 1) * SLOT_BLOCK
    n_used = blk_end[-1]
    experts = jnp.arange(N_EXPERTS, dtype=i32)

    def slot_index(route):
        sel = route[:, 0:TOP_K, None] == experts
        return (jnp.sum(jnp.where(sel, region, 0), axis=-1) + route[:, TOP_K:2 * TOP_K]).reshape(-1)

    dest_p = slot_index(route_p)
    dest_s = slot_index(route_s)
    blk = jnp.arange(n_blocks, dtype=i32)
    blk_c = jnp.minimum(blk, n_used - 1)
    block_expert = jnp.minimum(
        jnp.sum((blk_c[:, None] >= blk_end[None, :]).astype(i32), axis=-1), N_EXPERTS - 1)
    used = blk < n_used
    first = ((blk_c == blk_start[block_expert]) & used).astype(i32)
    is_last = blk_c == blk_end[block_expert] - 1
    sub_last = (rows_last[block_expert] + SUB_BLOCK - 1) // SUB_BLOCK
    nsub = jnp.where(used, jnp.where(is_last, sub_last, SLOT_BLOCK // SUB_BLOCK), 0).astype(i32)

    xs = _dispatch(jnp.concatenate([dest_p, dest_s]), counts, region, n_used.reshape(1), hp_p, hp_s, n_blocks)
    act = _ffn_up(block_expert, first, nsub, xs, w_gate[l], b_gate[l], w_up[l], b_up[l], n_blocks)
    ys = _ffn_down(block_expert, first, nsub, act, w_down[l], b_down[l], n_blocks)

    g_final = row(final_norm_g)
    y_p = _combine(dest_p, x1_p.reshape(n_p, D_MODEL), meta_p, g_final, ys)
    y_s = _combine(dest_s, x1_s, meta_s, g_final, ys)

    y_prompt = y_p.reshape(bsz, seq, D_MODEL)
    y_sample = y_s.reshape(dec_b, dec_t, D_MODEL)
    state_pool_prompt = ptail[:, POOL_MAX - POOL_STATE:][None]
    p_s3 = p_s.reshape(dec_b, dec_t, D_B)
    state_pool_sample = p_s3[:, dec_t - POOL_STATE:][None]
    state_chunk_v_sample = vn_s.reshape(dec_b, dec_t, D_A)[None]
    return (y_prompt, y_sample, state_pool_prompt, state_pool_sample, state_chunk_v_sample)
```

```python
import functools

import jax
import jax.numpy as jnp
from jax import lax
from jax.experimental import pallas as pl
from jax.experimental.pallas import tpu as pltpu

D_MODEL = 2048
D_A = 1024
D_B = 1024
GMLP_CHUNK = 128
N_HEADS_A = 8
HEAD_DIM_A = D_A // N_HEADS_A
POOL_WINDOWS = (2, 4, 8, 16)
N_POOL_GROUPS = len(POOL_WINDOWS)
POOL_GROUP_DIM = D_B // N_POOL_GROUPS
POOL_MAX = 16
POOL_STATE = POOL_MAX - 1
PAST_LEN = 1024
N_EXPERTS = 32
TOP_K = 4
D_FF = 2048
SWIGLU_LIMIT = 7.0
SWIGLU_ALPHA = 1.702
EPS = 1e-5

LANES = 128
SUBLANES = 8
TM_PROMPT = 256
SLOT_BLOCK = 512
SUB_BLOCK = 128
TM_DISPATCH = 1024
FF_TILE = 1024
TM_COMBINE = 128
VMEM_LIMIT = 56 * 1024 * 1024

BF16 = jnp.bfloat16
F32 = jnp.float32


def _dot(a, b):
    return jnp.dot(a, b, preferred_element_type=F32)


def _rms_norm(x, g):
    return x * lax.rsqrt(jnp.mean(x * x, axis=-1, keepdims=True) + EPS) * g


def _layer_norm(x, g, b):
    mu = jnp.mean(x, axis=-1, keepdims=True)
    xc = x - mu
    return xc * lax.rsqrt(jnp.mean(xc * xc, axis=-1, keepdims=True) + EPS) * g + b


def _gelu(x):
    return 0.5 * x * (1.0 + lax.erf(x * (2.0 ** -0.5)))


def _mixer_front(x, g_mix, w_in, ln_g, ln_b):
    h = _rms_norm(x, g_mix[...]).astype(BF16)
    u = _gelu(_dot(h, w_in[:, 0:D_A]))
    vn = _layer_norm(_dot(h, w_in[:, D_A:2 * D_A]), ln_g[...], ln_b[...])
    p = _dot(h, w_in[:, 2 * D_A:2 * D_A + D_B])
    return u, vn, p


def _spatial_gate(u, vn, ws, bs_t, chunk, a_ref):
    rows = u.shape[0]
    t = lax.broadcasted_iota(jnp.int32, (GMLP_CHUNK, GMLP_CHUNK), 0)
    s = lax.broadcasted_iota(jnp.int32, (GMLP_CHUNK, GMLP_CHUNK), 1)
    mask = (s <= t) & ((t // chunk) == (s // chunk))
    vb = vn.astype(BF16)
    for h in range(N_HEADS_A):
        w = jnp.where(mask, ws[h], 0.0).astype(BF16)
        bias = bs_t[:, h:h + 1]
        c0 = h * HEAD_DIM_A
        for r0 in range(0, rows, GMLP_CHUNK):
            mixed = _dot(w, vb[r0:r0 + GMLP_CHUNK, c0:c0 + HEAD_DIM_A]) + bias
            a_ref[r0:r0 + GMLP_CHUNK, c0:c0 + HEAD_DIM_A] = u[r0:r0 + GMLP_CHUNK, c0:c0 + HEAD_DIM_A] * mixed


def _window_sums(e):
    parts = []
    s = e
    for g, w in enumerate(POOL_WINDOWS):
        s = s + pltpu.roll(s, w // 2, 0)
        parts.append(s[:, 0:POOL_GROUP_DIM])
        if g + 1 < N_POOL_GROUPS:
            s = s[:, POOL_GROUP_DIM:]
    return parts


def _pool_project(diffs, w_pool, pool_scale, b_ref):
    for g in range(N_POOL_GROUPS):
        c0 = g * POOL_GROUP_DIM
        out = _dot(diffs[g].astype(BF16), w_pool[g])
        b_ref[:, c0:c0 + POOL_GROUP_DIM] = out * pool_scale[:, c0:c0 + POOL_GROUP_DIM]


def _mixer_back(x, a_ref, b_ref, g_a, g_b, w_out, g_ffn):
    na = _rms_norm(a_ref[...], g_a[...]).astype(BF16)
    nb = _rms_norm(b_ref[...], g_b[...]).astype(BF16)
    x1 = x + _dot(na, w_out[0:D_A, :]) + _dot(nb, w_out[D_A:D_A + D_B, :])
    hb = _rms_norm(x1, g_ffn[...]).astype(BF16)
    return x1, hb


def _pack_bf16_pairs(hb):
    half = D_MODEL // 2
    lo = lax.bitcast_convert_type(hb[:, 0:half].astype(F32), jnp.int32)
    hi = lax.bitcast_convert_type(hb[:, half:D_MODEL].astype(F32), jnp.int32)
    return lax.shift_right_logical(lo, 16) | hi


def _route(hb, w_router, b_router, run_ref):
    rows = hb.shape[0]
    logits = _dot(hb, w_router[...]) + b_router[...]
    l = logits.T[0:N_EXPERTS]
    sub = lax.broadcasted_iota(jnp.int32, (N_EXPERTS, rows), 0)
    vals, idxs = [], []
    for _ in range(TOP_K):
        m = jnp.max(l, axis=0, keepdims=True)
        i = jnp.min(jnp.where(l == m, sub, N_EXPERTS), axis=0, keepdims=True)
        vals.append(m)
        idxs.append(i)
        l = jnp.where(sub == i, -jnp.inf, l)
    exps = [jnp.exp(v - vals[0]) for v in vals]
    denom = exps[0] + exps[1] + exps[2] + exps[3]
    gates = [e / denom for e in exps]

    onehot = jnp.zeros((N_EXPERTS, rows), F32)
    for i in idxs:
        onehot = jnp.where(sub == i, 1.0, onehot)
    s = lax.broadcasted_iota(jnp.int32, (rows, rows), 0)
    t = lax.broadcasted_iota(jnp.int32, (rows, rows), 1)
    earlier = jnp.where(s < t, 1.0, 0.0).astype(BF16)
    base = _dot(onehot.astype(BF16), earlier) + run_ref[:, 0:1]
    ranks = [jnp.sum(jnp.where(sub == i, base, 0.0), axis=0, keepdims=True) for i in idxs]
    run_ref[...] = run_ref[...] + jnp.sum(onehot, axis=1, keepdims=True)

    route_t = jnp.concatenate(idxs + [r.astype(jnp.int32) for r in ranks], axis=0)
    sub_g = lax.broadcasted_iota(jnp.int32, (LANES, rows), 0)
    gates_t = jnp.zeros((LANES, rows), F32)
    for j, g in enumerate(gates):
        gates_t = jnp.where(sub_g == j, g, gates_t)
    return gates_t.T, route_t


def _mixer_prompt_kernel(x_ref, g_mix, w_in, ln_g, ln_b, ws, bs_t, w_pool, pool_scale, g_a, g_b, w_out,
                         g_ffn, w_router, b_router,
                         x1_ref, hp_ref, meta_ref, route_ref, ptail_ref, cnt_ref,
                         halo_ref, run_ref, a_ref, b_ref):
    tm = x_ref.shape[1]
    j = pl.program_id(1)

    @pl.when((pl.program_id(0) == 0) & (j == 0))
    def _():
        run_ref[...] = jnp.zeros_like(run_ref)

    @pl.when(j == 0)
    def _():
        halo_ref[...] = jnp.zeros_like(halo_ref)

    x = x_ref[0]
    u, vn, p = _mixer_front(x, g_mix, w_in, ln_g, ln_b)
    _spatial_gate(u, vn, ws, bs_t, GMLP_CHUNK, a_ref)

    e = jnp.concatenate([halo_ref[...], p], axis=0)
    halo_ref[...] = p[tm - POOL_MAX:tm]
    ptail_ref[0] = p[tm - POOL_MAX:tm]
    pos = j * tm + lax.broadcasted_iota(jnp.int32, (tm, 1), 0)
    sums = _window_sums(e)
    diffs = []
    for g, w in enumerate(POOL_WINDOWS):
        inv_cnt = 1.0 / jnp.minimum(pos + 1, w).astype(F32)
        c0 = g * POOL_GROUP_DIM
        diffs.append(sums[g][POOL_MAX:] * inv_cnt - p[:, c0:c0 + POOL_GROUP_DIM])
    _pool_project(diffs, w_pool, pool_scale, b_ref)

    x1, hb = _mixer_back(x, a_ref, b_ref, g_a, g_b, w_out, g_ffn)
    x1_ref[0] = x1
    hp_ref[...] = _pack_bf16_pairs(hb)
    meta_ref[...], route_ref[...] = _route(hb, w_router, b_router, run_ref)
    cnt_ref[...] = run_ref[...]


def _mixer_sample_kernel(x_ref, hist_ref, cnt_in_ref, g_mix, w_in, ln_g, ln_b, ws, bs_t, w_pool, pool_scale,
                         g_a, g_b, w_out, g_ffn, w_router, b_router,
                         x1_ref, hp_ref, meta_ref, route_ref, vn_ref, p_ref, cnt_ref,
                         run_ref, a_ref, b_ref, ext_ref):
    n_streams, hist_rows, _ = hist_ref.shape
    t_new = x_ref.shape[0] // n_streams
    group = hist_rows + t_new
    run_ref[...] = cnt_in_ref[...]

    x = x_ref[...]
    u, vn, p = _mixer_front(x, g_mix, w_in, ln_g, ln_b)
    vn_ref[...] = vn
    p_ref[...] = p
    _spatial_gate(u, vn, ws, bs_t, t_new, a_ref)

    for b in range(n_streams):
        ext_ref[b * group:b * group + hist_rows, :] = hist_ref[b]
        ext_ref[b * group + hist_rows:(b + 1) * group, :] = p[b * t_new:(b + 1) * t_new]
    e = ext_ref[...]
    sums = _window_sums(e)
    diffs = []
    for g, w in enumerate(POOL_WINDOWS):
        c0 = g * POOL_GROUP_DIM
        d = sums[g] * (1.0 / w) - e[:, c0:c0 + POOL_GROUP_DIM]
        diffs.append(jnp.concatenate(
            [d[b * group + hist_rows:(b + 1) * group] for b in range(n_streams)], axis=0))
    _pool_project(diffs, w_pool, pool_scale, b_ref)

    x1, hb = _mixer_back(x, a_ref, b_ref, g_a, g_b, w_out, g_ffn)
    x1_ref[...] = x1
    hp_ref[...] = _pack_bf16_pairs(hb)
    meta_ref[...], route_ref[...] = _route(hb, w_router, b_router, run_ref)
    cnt_ref[...] = run_ref[...]


def _const_spec(shape):
    zeros = (0,) * len(shape)
    return pl.BlockSpec(shape, lambda *_: zeros, pipeline_mode=pl.Buffered(1))


def _mixer_weight_specs():
    return [
        _const_spec((1, D_MODEL)),
        _const_spec((D_MODEL, 2 * D_A + D_B)),
        _const_spec((1, D_A)), _const_spec((1, D_A)),
        _const_spec((N_HEADS_A, GMLP_CHUNK, GMLP_CHUNK)),
        _const_spec((GMLP_CHUNK, N_HEADS_A)),
        _const_spec((N_POOL_GROUPS, POOL_GROUP_DIM, POOL_GROUP_DIM)),
        _const_spec((1, D_B)),
        _const_spec((1, D_A)), _const_spec((1, D_B)),
        _const_spec((D_A + D_B, D_MODEL)),
        _const_spec((1, D_MODEL)),
        _const_spec((D_MODEL, LANES)),
        _const_spec((1, LANES)),
    ]


def _mixer_prompt(x, weights):
    bsz, seq, _ = x.shape
    tm = TM_PROMPT
    nt = seq // tm
    return pl.pallas_call(
        _mixer_prompt_kernel,
        name="mixer_prompt",
        grid=(bsz, nt),
        in_specs=[pl.BlockSpec((1, tm, D_MODEL), lambda b, j: (b, j, 0))] + _mixer_weight_specs(),
        out_specs=[
            pl.BlockSpec((1, tm, D_MODEL), lambda b, j: (b, j, 0)),
            pl.BlockSpec((tm, D_MODEL // 2), lambda b, j: (b * nt + j, 0)),
            pl.BlockSpec((tm, LANES), lambda b, j: (b * nt + j, 0)),
            pl.BlockSpec((2 * TOP_K, tm), lambda b, j: (0, b * nt + j)),
            pl.BlockSpec((1, POOL_MAX, D_B), lambda b, j: (b, 0, 0)),
            pl.BlockSpec((N_EXPERTS, LANES), lambda b, j: (0, 0)),
        ],
        out_shape=[
            jax.ShapeDtypeStruct((bsz, seq, D_MODEL), F32),
            jax.ShapeDtypeStruct((bsz * seq, D_MODEL // 2), jnp.int32),
            jax.ShapeDtypeStruct((bsz * seq, LANES), F32),
            jax.ShapeDtypeStruct((2 * TOP_K, bsz * seq), jnp.int32),
            jax.ShapeDtypeStruct((bsz, POOL_MAX, D_B), F32),
            jax.ShapeDtypeStruct((N_EXPERTS, LANES), F32),
        ],
        scratch_shapes=[
            pltpu.VMEM((POOL_MAX, D_B), F32),
            pltpu.VMEM((N_EXPERTS, LANES), F32),
            pltpu.VMEM((tm, D_A), F32),
            pltpu.VMEM((tm, D_B), F32),
        ],
        compiler_params=pltpu.CompilerParams(
            dimension_semantics=("arbitrary", "arbitrary"), vmem_limit_bytes=VMEM_LIMIT),
    )(x, *weights)


def _mixer_sample(x2d, hist, cnt_in, weights):
    rows = x2d.shape[0]
    n_streams, hist_rows, _ = hist.shape
    ext_rows = rows + n_streams * hist_rows
    full = lambda shape: pl.BlockSpec(shape, lambda i: (0,) * len(shape))
    return pl.pallas_call(
        _mixer_sample_kernel,
        name="mixer_sample",
        grid=(1,),
        in_specs=[full((rows, D_MODEL)), full(hist.shape), full((N_EXPERTS, LANES))] + _mixer_weight_specs(),
        out_specs=[
            full((rows, D_MODEL)), full((rows, D_MODEL // 2)), full((rows, LANES)), full((2 * TOP_K, rows)),
            full((rows, D_A)), full((rows, D_B)), full((N_EXPERTS, LANES)),
        ],
        out_shape=[
            jax.ShapeDtypeStruct((rows, D_MODEL), F32),
            jax.ShapeDtypeStruct((rows, D_MODEL // 2), jnp.int32),
            jax.ShapeDtypeStruct((rows, LANES), F32),
            jax.ShapeDtypeStruct((2 * TOP_K, rows), jnp.int32),
            jax.ShapeDtypeStruct((rows, D_A), F32),
            jax.ShapeDtypeStruct((rows, D_B), F32),
            jax.ShapeDtypeStruct((N_EXPERTS, LANES), F32),
        ],
        scratch_shapes=[
            pltpu.VMEM((N_EXPERTS, LANES), F32),
            pltpu.VMEM((rows, D_A), F32),
            pltpu.VMEM((rows, D_B), F32),
            pltpu.VMEM((ext_rows, D_B), F32),
        ],
        compiler_params=pltpu.CompilerParams(
            dimension_semantics=("arbitrary",), vmem_limit_bytes=VMEM_LIMIT),
    )(x2d, hist, cnt_in, *weights)


def _dispatch_kernel(n_prompt, n_sample, dest_ref, cnt_ref, region_ref, nused_ref,
                     hp_p_ref, hp_s_ref, xs_hbm, zbuf, sem, zsem):
    i = pl.program_id(0)
    tm_p = hp_p_ref.shape[0]
    tm_s = hp_s_ref.shape[0]
    n_prompt_tiles = n_prompt // tm_p
    n_blocks = xs_hbm.shape[0] // SLOT_BLOCK

    def pad_rows(e):
        cnt = cnt_ref[e]
        lo = region_ref[e] + cnt
        mid = region_ref[e] + (cnt + SUB_BLOCK - 1) // SUB_BLOCK * SUB_BLOCK
        hi = region_ref[e] + (cnt + SLOT_BLOCK - 1) // SLOT_BLOCK * SLOT_BLOCK
        return lo, mid, (hi - mid) // SUB_BLOCK

    def zero_row_copy(r):
        return pltpu.make_async_copy(zbuf.at[pl.ds(0, 1)], xs_hbm.at[pl.ds(r, 1)], zsem.at[0])

    def zero_sub_block_copy(r):
        return pltpu.make_async_copy(
            zbuf.at[pl.ds(0, SUB_BLOCK)], xs_hbm.at[pl.ds(pl.multiple_of(r, SUB_BLOCK), SUB_BLOCK)], zsem.at[1])

    def zero_block_copy(rb):
        return pltpu.make_async_copy(zbuf, xs_hbm.at[pl.ds(rb * SLOT_BLOCK, SLOT_BLOCK)], zsem.at[2])

    def for_each_fill(row_fn, sub_block_fn, block_fn):
        def per_expert(e, carry):
            lo, mid, n_sub = pad_rows(e)
            lax.fori_loop(lo, mid, lambda r, c: (row_fn(r), c)[1], 0)
            lax.fori_loop(0, n_sub, lambda j, c: (sub_block_fn(mid + j * SUB_BLOCK), c)[1], 0)
            return carry
        lax.fori_loop(0, N_EXPERTS, per_expert, 0)
        lax.fori_loop(nused_ref[0], n_blocks, lambda rb, c: (block_fn(rb), c)[1], 0)

    @pl.when(i == 0)
    def _():
        zbuf[...] = jnp.zeros_like(zbuf)
        for_each_fill(lambda r: zero_row_copy(r).start(), lambda r: zero_sub_block_copy(r).start(),
                      lambda rb: zero_block_copy(rb).start())

    def scatter(src_ref, first_index, k_stride):
        rows = src_ref.shape[0]

        def body(t, carry):
            for k in range(TOP_K):
                d = dest_ref[first_index + k * k_stride + t]
                pltpu.make_async_copy(
                    src_ref.at[pl.ds(t, 1)], xs_hbm.at[pl.ds(d, 1)], sem).start(priority=k % 2)
            return carry
        lax.fori_loop(0, rows, body, 0, unroll=8)
        for _ in range(TOP_K):
            pltpu.make_async_copy(src_ref, xs_hbm.at[pl.ds(0, rows)], sem).wait()

    @pl.when(i < n_prompt_tiles)
    def _():
        scatter(hp_p_ref, i * tm_p, n_prompt)

    @pl.when(i >= n_prompt_tiles)
    def _():
        scatter(hp_s_ref, TOP_K * n_prompt + (i - n_prompt_tiles) * tm_s, n_sample)

    @pl.when(i == 0)
    def _():
        for_each_fill(lambda r: zero_row_copy(r).wait(), lambda r: zero_sub_block_copy(r).wait(),
                      lambda rb: zero_block_copy(rb).wait())


def _dispatch(dest, counts, region, n_used, hp_p, hp_s, n_blocks):
    tm_p = min(TM_DISPATCH, hp_p.shape[0])
    tm_s = min(TM_DISPATCH, hp_s.shape[0])
    assert hp_p.shape[0] % tm_p == 0 and hp_s.shape[0] % tm_s == 0
    ntp = hp_p.shape[0] // tm_p
    nts = hp_s.shape[0] // tm_s
    half = D_MODEL // 2
    return pl.pallas_call(
        functools.partial(_dispatch_kernel, hp_p.shape[0], hp_s.shape[0]),
        name="dispatch",
        grid_spec=pltpu.PrefetchScalarGridSpec(
            num_scalar_prefetch=4,
            grid=(ntp + nts,),
            in_specs=[
                pl.BlockSpec((tm_p, half), lambda i, *_: (jnp.minimum(i, ntp - 1), 0)),
                pl.BlockSpec((tm_s, half), lambda i, *_: (jnp.maximum(i - ntp, 0), 0)),
            ],
            out_specs=pl.BlockSpec(memory_space=pl.ANY),
            scratch_shapes=[
                pltpu.VMEM((SLOT_BLOCK, half), jnp.int32),
                pltpu.SemaphoreType.DMA,
                pltpu.SemaphoreType.DMA((3,)),
            ],
        ),
        out_shape=jax.ShapeDtypeStruct((n_blocks * SLOT_BLOCK, half), jnp.int32),
        compiler_params=pltpu.CompilerParams(dimension_semantics=("arbitrary",)),
    )(dest, counts, region, n_used, hp_p, hp_s)


def _unpack_bf16_pairs(w):
    lo = lax.bitcast_convert_type(lax.shift_left(w, 16), F32).astype(BF16)
    hi = lax.bitcast_convert_type(w & jnp.int32(-65536), F32).astype(BF16)
    return jnp.concatenate([lo, hi], axis=1)


def _for_each_sub_block(nsub, compute, out_ref):
    n_sub_max = SLOT_BLOCK // SUB_BLOCK

    @pl.when(nsub == n_sub_max)
    def _():
        compute(pl.ds(0, SLOT_BLOCK))

    for s in range(n_sub_max):
        rows = pl.ds(s * SUB_BLOCK, SUB_BLOCK)

        @pl.when((s < nsub) & (nsub < n_sub_max))
        def _(rows=rows):
            compute(rows)

        @pl.when(s >= nsub)
        def _(rows=rows):
            out_ref[rows, :] = jnp.zeros((SUB_BLOCK, out_ref.shape[1]), out_ref.dtype)


def _grouped_call(name, step, grid, in_specs_fn, out_spec, operands, out_shape, scratch_shapes):
    n_in = len(operands)

    def outer(be_ref, first_ref, nsub_ref, *refs):
        ins, out, scratches = refs[:n_in], refs[n_in], refs[n_in + 1:]

        def body(*block_refs):
            step(be_ref, first_ref, nsub_ref, *block_refs)

        pltpu.emit_pipeline(
            body, grid=grid, in_specs=in_specs_fn(be_ref), out_specs=[out_spec],
        )(*ins, out, scratches=scratches)

    any_spec = pl.BlockSpec(memory_space=pl.ANY)
    return lambda block_expert, first, nsub: pl.pallas_call(
        outer,
        name=name,
        grid_spec=pltpu.PrefetchScalarGridSpec(
            num_scalar_prefetch=3,
            grid=(),
            in_specs=[any_spec] * n_in,
            out_specs=any_spec,
            scratch_shapes=scratch_shapes,
        ),
        out_shape=out_shape,
        compiler_params=pltpu.CompilerParams(vmem_limit_bytes=VMEM_LIMIT),
    )(block_expert, first, nsub, *operands)


def _expert_weight_spec(be_ref, shape):
    return pl.BlockSpec(shape, lambda j, rb: (be_ref[rb], 0, j),
                        pipeline_mode=pl.Buffered(2, use_lookahead=True))


def _ffn_up_step(be_ref, first_ref, nsub_ref, x_ref, wg_ref, bg_ref, wu_ref, bu_ref, act_ref, wgb, wub):
    rb = pl.program_id(1)
    expert = pl.ds(be_ref[rb], 1)

    @pl.when(first_ref[rb] == 1)
    def _():
        wgb[...] = wg_ref[0].astype(BF16)
        wub[...] = wu_ref[0].astype(BF16)

    def compute(rows):
        x = _unpack_bf16_pairs(x_ref[rows, :])
        gate = jnp.minimum(_dot(x, wgb[...]) + bg_ref[expert, :], SWIGLU_LIMIT)
        up = jnp.clip(_dot(x, wub[...]) + bu_ref[expert, :], -SWIGLU_LIMIT, SWIGLU_LIMIT)
        glu = gate * jax.nn.sigmoid(SWIGLU_ALPHA * gate)
        act_ref[rows, :] = ((up + 1.0) * glu).astype(BF16)

    _for_each_sub_block(nsub_ref[rb], compute, act_ref)


def _ffn_up(block_expert, first, nsub, xs, w_gate, b_gate, w_up, b_up, n_blocks):
    def in_specs(be_ref):
        w_spec = _expert_weight_spec(be_ref, (1, D_MODEL, FF_TILE))
        b_spec = pl.BlockSpec((N_EXPERTS, FF_TILE), lambda f, rb: (0, f))
        return [pl.BlockSpec((SLOT_BLOCK, D_MODEL // 2), lambda f, rb: (rb, 0)), w_spec, b_spec, w_spec, b_spec]

    return _grouped_call(
        "ffn_up", _ffn_up_step, (D_FF // FF_TILE, n_blocks), in_specs,
        pl.BlockSpec((SLOT_BLOCK, FF_TILE), lambda f, rb: (rb, f)),
        (xs, w_gate, b_gate, w_up, b_up),
        jax.ShapeDtypeStruct((n_blocks * SLOT_BLOCK, D_FF), BF16),
        [pltpu.VMEM((D_MODEL, FF_TILE), BF16), pltpu.VMEM((D_MODEL, FF_TILE), BF16)],
    )(block_expert, first, nsub)


def _ffn_down_step(be_ref, first_ref, nsub_ref, act_ref, wd_ref, bd_ref, y_ref, wdb):
    rb = pl.program_id(1)
    expert = pl.ds(be_ref[rb], 1)

    @pl.when(first_ref[rb] == 1)
    def _():
        wdb[...] = wd_ref[0].astype(BF16)

    def compute(rows):
        y = _dot(act_ref[rows, :], wdb[...]) + bd_ref[expert, :]
        y_ref[rows, :] = _pack_bf16_pairs(y.astype(BF16))

    _for_each_sub_block(nsub_ref[rb], compute, y_ref)


def _ffn_down(block_expert, first, nsub, act, w_down, b_down, n_blocks):
    def in_specs(be_ref):
        return [
            pl.BlockSpec((SLOT_BLOCK, D_FF), lambda n, rb: (rb, 0)),
            _expert_weight_spec(be_ref, (1, D_FF, D_MODEL)),
            pl.BlockSpec((N_EXPERTS, D_MODEL), lambda n, rb: (0, n)),
        ]

    return _grouped_call(
        "ffn_down", _ffn_down_step, (1, n_blocks), in_specs,
        pl.BlockSpec((SLOT_BLOCK, D_MODEL // 2), lambda n, rb: (rb, n)),
        (act, w_down, b_down),
        jax.ShapeDtypeStruct((n_blocks * SLOT_BLOCK, D_MODEL // 2), jnp.int32),
        [pltpu.VMEM((D_FF, D_MODEL), BF16)],
    )(block_expert, first, nsub)


def _combine_kernel(n_tokens, dest_ref, x1_ref, meta_ref, g_ref, ys_hbm, out_ref, ybuf0, ybuf1, sem0, sem1):
    i = pl.program_id(0)
    n_steps = pl.num_programs(0)
    tm = TM_COMBINE
    half = D_MODEL // 2

    def gather(tile, ybuf, sem):
        def body(grp, carry):
            for j in range(SUBLANES):
                token = tile * tm + grp * SUBLANES + j
                for k in range(TOP_K):
                    d = dest_ref[k * n_tokens + token]
                    pltpu.make_async_copy(
                        ys_hbm.at[lax.shift_right_logical(d, 3), pl.ds(d & (SUBLANES - 1), 1)],
                        ybuf.at[k, grp, pl.ds(j, 1)], sem).start(priority=k % 2)
            return carry
        lax.fori_loop(0, tm // SUBLANES, body, 0)

    def finish(r0, ybuf, sem):
        for k in range(TOP_K):
            pltpu.make_async_copy(ys_hbm.at[pl.ds(0, tm // SUBLANES)], ybuf.at[k], sem).wait()
        lo = x1_ref[r0:r0 + tm, 0:half]
        hi = x1_ref[r0:r0 + tm, half:D_MODEL]
        for k in range(TOP_K):
            w = ybuf[k].reshape(tm, half)
            gate = meta_ref[r0:r0 + tm, k:k + 1]
            lo = lo + lax.bitcast_convert_type(lax.shift_left(w, 16), F32) * gate
            hi = hi + lax.bitcast_convert_type(w & jnp.int32(-65536), F32) * gate
        ms = (jnp.sum(lo * lo, axis=-1, keepdims=True) + jnp.sum(hi * hi, axis=-1, keepdims=True)) / D_MODEL
        scale = lax.rsqrt(ms + EPS)
        out_ref[r0:r0 + tm, 0:half] = lo * scale * g_ref[:, 0:half]
        out_ref[r0:r0 + tm, half:D_MODEL] = hi * scale * g_ref[:, half:D_MODEL]

    @pl.when(i == 0)
    def _():
        gather(0, ybuf0, sem0)

    gather(2 * i + 1, ybuf1, sem1)
    finish(0, ybuf0, sem0)

    @pl.when(i + 1 < n_steps)
    def _():
        gather(2 * i + 2, ybuf0, sem0)

    finish(tm, ybuf1, sem1)


def _combine(dest_flat, x1, meta, g_final, ys):
    n = x1.shape[0]
    tm = 2 * TM_COMBINE
    assert n % tm == 0
    ybuf = pltpu.VMEM((TOP_K, TM_COMBINE // SUBLANES, SUBLANES, D_MODEL // 2), jnp.int32)
    ys = ys.reshape(ys.shape[0] // SUBLANES, SUBLANES, D_MODEL // 2)
    return pl.pallas_call(
        functools.partial(_combine_kernel, n),
        name="combine",
        grid_spec=pltpu.PrefetchScalarGridSpec(
            num_scalar_prefetch=1,
            grid=(n // tm,),
            in_specs=[
                pl.BlockSpec((tm, D_MODEL), lambda i, d: (i, 0)),
                pl.BlockSpec((tm, LANES), lambda i, d: (i, 0)),
                pl.BlockSpec((1, D_MODEL), lambda i, d: (0, 0)),
                pl.BlockSpec(memory_space=pl.ANY),
            ],
            out_specs=pl.BlockSpec((tm, D_MODEL), lambda i, d: (i, 0)),
            scratch_shapes=[ybuf, ybuf, pltpu.SemaphoreType.DMA, pltpu.SemaphoreType.DMA],
        ),
        out_shape=jax.ShapeDtypeStruct((n, D_MODEL), F32),
        compiler_params=pltpu.CompilerParams(
            dimension_semantics=("arbitrary",), vmem_limit_bytes=VMEM_LIMIT),
    )(dest_flat, x1, meta, g_final, ys)


def kernel(x_prompt, x_sample, cache_pool, norm_mix_g, w_in, ln_v_g, ln_v_b, w_spatial, b_spatial, w_pool,
           pool_scale, out_norm_a_g, out_norm_b_g, w_out, norm_ffn_g, w_router, b_router, w_gate, b_gate,
           w_up, b_up, w_down, b_down, final_norm_g):
    depth = norm_mix_g.shape[0]
    assert depth == 1
    bsz, seq, _ = x_prompt.shape
    dec_b, dec_t, _ = x_sample.shape
    assert seq % TM_PROMPT == 0 and TM_PROMPT % GMLP_CHUNK == 0
    assert GMLP_CHUNK % dec_t == 0 and (dec_b * dec_t) % GMLP_CHUNK == 0 and PAST_LEN % GMLP_CHUNK == 0
    assert dec_t >= POOL_STATE and PAST_LEN + 1 >= POOL_MAX
    n_p = bsz * seq
    n_s = dec_b * dec_t
    n = n_p + n_s
    assert n_p % TM_COMBINE == 0 and n_s % TM_COMBINE == 0
    l = 0

    row = lambda v: v.reshape(1, -1)
    shared = [
        row(norm_mix_g[l]), w_in[l].astype(BF16), row(ln_v_g[l]), row(ln_v_b[l]),
    ]
    tail = [
        w_pool[l].astype(BF16), row(pool_scale[l]), row(out_norm_a_g[l]), row(out_norm_b_g[l]),
        w_out[l].astype(BF16), row(norm_ffn_g[l]),
        jnp.pad(w_router[l].astype(BF16), ((0, 0), (0, LANES - N_EXPERTS))),
        jnp.pad(row(b_router[l]), ((0, 0), (0, LANES - N_EXPERTS))),
    ]
    weights_p = shared + [w_spatial[l], b_spatial[l].T] + tail
    reps = GMLP_CHUNK // dec_t
    weights_s = shared + [
        jnp.tile(w_spatial[l][:, :dec_t, :dec_t], (1, reps, reps)),
        jnp.tile(b_spatial[l][:, :dec_t].T, (reps, 1)),
    ] + tail

    x1_p, hp_p, meta_p, route_p, ptail, cnt_p = _mixer_prompt(x_prompt, weights_p)
    hist = jnp.pad(cache_pool[l], ((0, 0), (POOL_MAX - POOL_STATE, 0), (0, 0)))
    x1_s, hp_s, meta_s, route_s, vn_s, p_s, cnt = _mixer_sample(
        x_sample.reshape(n_s, D_MODEL), hist, cnt_p, weights_s)

    i32 = jnp.int32
    counts = cnt[:, 0].astype(i32)
    n_blocks = -(-(n * TOP_K + N_EXPERTS * (SLOT_BLOCK - 1)) // SLOT_BLOCK)
    blocks_e = (counts + SLOT_BLOCK - 1) // SLOT_BLOCK
    blk_end = jnp.cumsum(blocks_e)
    blk_start = blk_end - blocks_e
    region = blk_start * SLOT_BLOCK
    rows_last = counts - (blocks_e - 1) * SLOT_BLOCK
    n_used = blk_end[-1]
    experts = jnp.arange(N_EXPERTS, dtype=i32)

    def slot_index(route_t):
        sel = route_t[0:TOP_K, None, :] == experts[None, :, None]
        first_slot = jnp.sum(jnp.where(sel, region[None, :, None], 0), axis=1)
        return (first_slot + route_t[TOP_K:2 * TOP_K]).reshape(-1)

    dest_p = slot_index(route_p)
    dest_s = slot_index(route_s)
    blk = jnp.arange(n_blocks, dtype=i32)
    blk_c = jnp.minimum(blk, n_used - 1)
    block_expert = jnp.minimum(
        jnp.sum((blk_c[:, None] >= blk_end[None, :]).astype(i32), axis=-1), N_EXPERTS - 1)
    used = blk < n_used
    first = ((blk_c == blk_start[block_expert]) & used).astype(i32)
    is_last = blk_c == blk_end[block_expert] - 1
    sub_last = (rows_last[block_expert] + SUB_BLOCK - 1) // SUB_BLOCK
    nsub = jnp.where(used, jnp.where(is_last, sub_last, SLOT_BLOCK // SUB_BLOCK), 0).astype(i32)

    xs = _dispatch(jnp.concatenate([dest_p, dest_s]), counts, region, n_used.reshape(1), hp_p, hp_s, n_blocks)
    act = _ffn_up(block_expert, first, nsub, xs, w_gate[l], b_gate[l], w_up[l], b_up[l], n_blocks)
    ys = _ffn_down(block_expert, first, nsub, act, w_down[l], b_down[l], n_blocks)

    g_final = row(final_norm_g)
    y_p = _combine(dest_p, x1_p.reshape(n_p, D_MODEL), meta_p, g_final, ys)
    y_s = _combine(dest_s, x1_s, meta_s, g_final, ys)

    y_prompt = y_p.reshape(bsz, seq, D_MODEL)
    y_sample = y_s.reshape(dec_b, dec_t, D_MODEL)
    state_pool_prompt = ptail[:, POOL_MAX - POOL_STATE:][None]
    p_s3 = p_s.reshape(dec_b, dec_t, D_B)
    state_pool_sample = p_s3[:, dec_t - POOL_STATE:][None]
    state_chunk_v_sample = vn_s.reshape(dec_b, dec_t, D_A)[None]
    return (y_prompt, y_sample, state_pool_prompt, state_pool_sample, state_chunk_v_sample)
```

```python
import functools

import jax
import jax.numpy as jnp
from jax import lax
from jax.experimental import pallas as pl
from jax.experimental.pallas import tpu as pltpu

D_MODEL = 2048
D_A = 1024
D_B = 1024
GMLP_CHUNK = 128
N_HEADS_A = 8
HEAD_DIM_A = D_A // N_HEADS_A
POOL_WINDOWS = (2, 4, 8, 16)
N_POOL_GROUPS = len(POOL_WINDOWS)
POOL_GROUP_DIM = D_B // N_POOL_GROUPS
POOL_MAX = 16
POOL_STATE = POOL_MAX - 1
PAST_LEN = 1024
N_EXPERTS = 32
TOP_K = 4
D_FF = 2048
SWIGLU_LIMIT = 7.0
SWIGLU_ALPHA = 1.702
EPS = 1e-5

LANES = 128
SUBLANES = 8
TM_PROMPT = 256
SLOT_BLOCK = 512
SUB_BLOCK = 128
TM_DISPATCH = 1024
FF_TILE = 1024
TM_COMBINE = 128
VMEM_LIMIT = 56 * 1024 * 1024

BF16 = jnp.bfloat16
F32 = jnp.float32


def _dot(a, b):
    return jnp.dot(a, b, preferred_element_type=F32)


def _rms_norm(x, g):
    return x * lax.rsqrt(jnp.mean(x * x, axis=-1, keepdims=True) + EPS) * g


def _layer_norm(x, g, b):
    mu = jnp.mean(x, axis=-1, keepdims=True)
    xc = x - mu
    return xc * lax.rsqrt(jnp.mean(xc * xc, axis=-1, keepdims=True) + EPS) * g + b


def _gelu(x):
    return 0.5 * x * (1.0 + lax.erf(x * (2.0 ** -0.5)))


def _mixer_front(x, g_mix, w_in, ln_g, ln_b):
    h = _rms_norm(x, g_mix[...]).astype(BF16)
    u = _gelu(_dot(h, w_in[:, 0:D_A]))
    vn = _layer_norm(_dot(h, w_in[:, D_A:2 * D_A]), ln_g[...], ln_b[...])
    p = _dot(h, w_in[:, 2 * D_A:2 * D_A + D_B])
    return u, vn, p


def _spatial_gate(u, vn, ws, bs_t, chunk, a_ref):
    rows = u.shape[0]
    t = lax.broadcasted_iota(jnp.int32, (GMLP_CHUNK, GMLP_CHUNK), 0)
    s = lax.broadcasted_iota(jnp.int32, (GMLP_CHUNK, GMLP_CHUNK), 1)
    mask = (s <= t) & ((t // chunk) == (s // chunk))
    vb = vn.astype(BF16)
    for h in range(N_HEADS_A):
        w = jnp.where(mask, ws[h], 0.0).astype(BF16)
        bias = bs_t[:, h:h + 1]
        c0 = h * HEAD_DIM_A
        for r0 in range(0, rows, GMLP_CHUNK):
            mixed = _dot(w, vb[r0:r0 + GMLP_CHUNK, c0:c0 + HEAD_DIM_A]) + bias
            a_ref[r0:r0 + GMLP_CHUNK, c0:c0 + HEAD_DIM_A] = u[r0:r0 + GMLP_CHUNK, c0:c0 + HEAD_DIM_A] * mixed


def _window_sums(e):
    parts = []
    s = e
    for g, w in enumerate(POOL_WINDOWS):
        s = s + pltpu.roll(s, w // 2, 0)
        parts.append(s[:, 0:POOL_GROUP_DIM])
        if g + 1 < N_POOL_GROUPS:
            s = s[:, POOL_GROUP_DIM:]
    return parts


def _pool_project(diffs, w_pool, pool_scale, b_ref):
    for g in range(N_POOL_GROUPS):
        c0 = g * POOL_GROUP_DIM
        out = _dot(diffs[g].astype(BF16), w_pool[g])
        b_ref[:, c0:c0 + POOL_GROUP_DIM] = out * pool_scale[:, c0:c0 + POOL_GROUP_DIM]


def _mixer_back(x, a_ref, b_ref, g_a, g_b, w_out, g_ffn):
    na = _rms_norm(a_ref[...], g_a[...]).astype(BF16)
    nb = _rms_norm(b_ref[...], g_b[...]).astype(BF16)
    x1 = x + _dot(na, w_out[0:D_A, :]) + _dot(nb, w_out[D_A:D_A + D_B, :])
    hb = _rms_norm(x1, g_ffn[...]).astype(BF16)
    return x1, hb


def _pack_bf16_pairs(hb):
    half = D_MODEL // 2
    lo = lax.bitcast_convert_type(hb[:, 0:half].astype(F32), jnp.int32)
    hi = lax.bitcast_convert_type(hb[:, half:D_MODEL].astype(F32), jnp.int32)
    return lax.shift_right_logical(lo, 16) | hi


def _route(hb, w_router, b_router, run_ref):
    rows = hb.shape[0]
    logits = _dot(hb, w_router[...]) + b_router[...]
    l = logits.T[0:N_EXPERTS]
    sub = lax.broadcasted_iota(jnp.int32, (N_EXPERTS, rows), 0)
    vals, idxs = [], []
    for _ in range(TOP_K):
        m = jnp.max(l, axis=0, keepdims=True)
        i = jnp.min(jnp.where(l == m, sub, N_EXPERTS), axis=0, keepdims=True)
        vals.append(m)
        idxs.append(i)
        l = jnp.where(sub == i, -jnp.inf, l)
    exps = [jnp.exp(v - vals[0]) for v in vals]
    denom = exps[0] + exps[1] + exps[2] + exps[3]
    gates = [e / denom for e in exps]

    onehot = jnp.zeros((N_EXPERTS, rows), F32)
    for i in idxs:
        onehot = jnp.where(sub == i, 1.0, onehot)
    s = lax.broadcasted_iota(jnp.int32, (rows, rows), 0)
    t = lax.broadcasted_iota(jnp.int32, (rows, rows), 1)
    earlier = jnp.where(s < t, 1.0, 0.0).astype(BF16)
    base = _dot(onehot.astype(BF16), earlier) + run_ref[:, 0:1]
    ranks = [jnp.sum(jnp.where(sub == i, base, 0.0), axis=0, keepdims=True) for i in idxs]
    run_ref[...] = run_ref[...] + jnp.sum(onehot, axis=1, keepdims=True)

    route_t = jnp.concatenate(idxs + [r.astype(jnp.int32) for r in ranks], axis=0)
    sub_g = lax.broadcasted_iota(jnp.int32, (LANES, rows), 0)
    gates_t = jnp.zeros((LANES, rows), F32)
    for j, g in enumerate(gates):
        gates_t = jnp.where(sub_g == j, g, gates_t)
    return gates_t.T, route_t


def _mixer_prompt_kernel(x_ref, g_mix, w_in, ln_g, ln_b, ws, bs_t, w_pool, pool_scale, g_a, g_b, w_out,
                         g_ffn, w_router, b_router,
                         x1_ref, hp_ref, meta_ref, route_ref, ptail_ref, cnt_ref,
                         halo_ref, run_ref, a_ref, b_ref):
    tm = x_ref.shape[1]
    j = pl.program_id(1)

    @pl.when((pl.program_id(0) == 0) & (j == 0))
    def _():
        run_ref[...] = jnp.zeros_like(run_ref)

    @pl.when(j == 0)
    def _():
        halo_ref[...] = jnp.zeros_like(halo_ref)

    x = x_ref[0]
    u, vn, p = _mixer_front(x, g_mix, w_in, ln_g, ln_b)
    _spatial_gate(u, vn, ws, bs_t, GMLP_CHUNK, a_ref)

    e = jnp.concatenate([halo_ref[...], p], axis=0)
    halo_ref[...] = p[tm - POOL_MAX:tm]
    ptail_ref[0] = p[tm - POOL_MAX:tm]
    pos = j * tm + lax.broadcasted_iota(jnp.int32, (tm, 1), 0)
    sums = _window_sums(e)
    diffs = []
    for g, w in enumerate(POOL_WINDOWS):
        inv_cnt = 1.0 / jnp.minimum(pos + 1, w).astype(F32)
        c0 = g * POOL_GROUP_DIM
        diffs.append(sums[g][POOL_MAX:] * inv_cnt - p[:, c0:c0 + POOL_GROUP_DIM])
    _pool_project(diffs, w_pool, pool_scale, b_ref)

    x1, hb = _mixer_back(x, a_ref, b_ref, g_a, g_b, w_out, g_ffn)
    x1_ref[0] = x1
    hp_ref[...] = _pack_bf16_pairs(hb)
    meta_ref[...], route_ref[...] = _route(hb, w_router, b_router, run_ref)
    cnt_ref[...] = run_ref[...]


def _slot_layout(run_ref, route_refs, dest_refs, cnt_ref, region_ref, tab_ref):
    cnt = run_ref[...].astype(jnp.int32)
    sub = lax.broadcasted_iota(jnp.int32, cnt.shape, 0)
    blocks = (cnt + (SLOT_BLOCK - 1)) // SLOT_BLOCK
    blk_end = blocks
    shift = 1
    while shift < N_EXPERTS:
        blk_end = blk_end + jnp.where(sub >= shift, pltpu.roll(blk_end, shift, 0), 0)
        shift *= 2
    blk_start = blk_end - blocks
    region = blk_start * SLOT_BLOCK
    rows_last = cnt - (blocks - 1) * SLOT_BLOCK
    cnt_ref[...] = cnt
    region_ref[...] = region

    def of_expert(table, e):
        return table[e:e + 1, 0:1]

    for route_ref, dest_ref in zip(route_refs, dest_refs):
        idx = route_ref[0:TOP_K, :]
        first_slot = jnp.zeros_like(idx)
        for e in range(N_EXPERTS):
            first_slot = jnp.where(idx == e, of_expert(region, e), first_slot)
        dest_ref[...] = first_slot + route_ref[TOP_K:2 * TOP_K, :]

    n_used = of_expert(blk_end, N_EXPERTS - 1)
    blk = lax.broadcasted_iota(jnp.int32, (1, tab_ref.shape[1]), 1)
    blk_c = jnp.minimum(blk, n_used - 1)
    expert = jnp.zeros_like(blk)
    for e in range(N_EXPERTS - 1):
        expert = expert + jnp.where(blk_c >= of_expert(blk_end, e), 1, 0)
    start = jnp.zeros_like(blk)
    end = jnp.zeros_like(blk)
    last_rows = jnp.zeros_like(blk)
    for e in range(N_EXPERTS):
        hit = expert == e
        start = jnp.where(hit, of_expert(blk_start, e), start)
        end = jnp.where(hit, of_expert(blk_end, e), end)
        last_rows = jnp.where(hit, of_expert(rows_last, e), last_rows)
    used = blk < n_used
    first = jnp.where((blk_c == start) & used, 1, 0)
    sub_last = (last_rows + (SUB_BLOCK - 1)) // SUB_BLOCK
    nsub = jnp.where(used, jnp.where(blk_c == end - 1, sub_last, SLOT_BLOCK // SUB_BLOCK), 0)
    n_used_row = jnp.zeros_like(blk) + n_used
    zero = jnp.zeros_like(blk)
    tab_ref[...] = jnp.concatenate([expert, first, nsub, n_used_row, zero, zero, zero, zero], axis=0)


def _mixer_sample_kernel(x_ref, hist_ref, cnt_in_ref, route_p_ref,
                         g_mix, w_in, ln_g, ln_b, ws, bs_t, w_pool, pool_scale,
                         g_a, g_b, w_out, g_ffn, w_router, b_router,
                         x1_ref, hp_ref, meta_ref, route_ref, vn_ref, p_ref,
                         dest_p_ref, dest_s_ref, cnt_ref, region_ref, tab_ref,
                         run_ref, a_ref, b_ref, ext_ref):
    n_streams, hist_rows, _ = hist_ref.shape
    t_new = x_ref.shape[0] // n_streams
    group = hist_rows + t_new
    run_ref[...] = cnt_in_ref[...]

    x = x_ref[...]
    u, vn, p = _mixer_front(x, g_mix, w_in, ln_g, ln_b)
    vn_ref[...] = vn
    p_ref[...] = p
    _spatial_gate(u, vn, ws, bs_t, t_new, a_ref)

    for b in range(n_streams):
        ext_ref[b * group:b * group + hist_rows, :] = hist_ref[b]
        ext_ref[b * group + hist_rows:(b + 1) * group, :] = p[b * t_new:(b + 1) * t_new]
    e = ext_ref[...]
    sums = _window_sums(e)
    diffs = []
    for g, w in enumerate(POOL_WINDOWS):
        c0 = g * POOL_GROUP_DIM
        d = sums[g] * (1.0 / w) - e[:, c0:c0 + POOL_GROUP_DIM]
        diffs.append(jnp.concatenate(
            [d[b * group + hist_rows:(b + 1) * group] for b in range(n_streams)], axis=0))
    _pool_project(diffs, w_pool, pool_scale, b_ref)

    x1, hb = _mixer_back(x, a_ref, b_ref, g_a, g_b, w_out, g_ffn)
    x1_ref[...] = x1
    hp_ref[...] = _pack_bf16_pairs(hb)
    meta_ref[...], route_ref[...] = _route(hb, w_router, b_router, run_ref)
    _slot_layout(run_ref, (route_p_ref, route_ref), (dest_p_ref, dest_s_ref), cnt_ref, region_ref, tab_ref)


def _const_spec(shape):
    zeros = (0,) * len(shape)
    return pl.BlockSpec(shape, lambda *_: zeros, pipeline_mode=pl.Buffered(1))


def _mixer_weight_specs():
    return [
        _const_spec((1, D_MODEL)),
        _const_spec((D_MODEL, 2 * D_A + D_B)),
        _const_spec((1, D_A)), _const_spec((1, D_A)),
        _const_spec((N_HEADS_A, GMLP_CHUNK, GMLP_CHUNK)),
        _const_spec((GMLP_CHUNK, N_HEADS_A)),
        _const_spec((N_POOL_GROUPS, POOL_GROUP_DIM, POOL_GROUP_DIM)),
        _const_spec((1, D_B)),
        _const_spec((1, D_A)), _const_spec((1, D_B)),
        _const_spec((D_A + D_B, D_MODEL)),
        _const_spec((1, D_MODEL)),
        _const_spec((D_MODEL, LANES)),
        _const_spec((1, LANES)),
    ]


def _mixer_prompt(x, weights):
    bsz, seq, _ = x.shape
    tm = TM_PROMPT
    nt = seq // tm
    return pl.pallas_call(
        _mixer_prompt_kernel,
        name="mixer_prompt",
        grid=(bsz, nt),
        in_specs=[pl.BlockSpec((1, tm, D_MODEL), lambda b, j: (b, j, 0))] + _mixer_weight_specs(),
        out_specs=[
            pl.BlockSpec((1, tm, D_MODEL), lambda b, j: (b, j, 0)),
            pl.BlockSpec((tm, D_MODEL // 2), lambda b, j: (b * nt + j, 0)),
            pl.BlockSpec((tm, LANES), lambda b, j: (b * nt + j, 0)),
            pl.BlockSpec((2 * TOP_K, tm), lambda b, j: (0, b * nt + j)),
            pl.BlockSpec((1, POOL_MAX, D_B), lambda b, j: (b, 0, 0)),
            pl.BlockSpec((N_EXPERTS, LANES), lambda b, j: (0, 0)),
        ],
        out_shape=[
            jax.ShapeDtypeStruct((bsz, seq, D_MODEL), F32),
            jax.ShapeDtypeStruct((bsz * seq, D_MODEL // 2), jnp.int32),
            jax.ShapeDtypeStruct((bsz * seq, LANES), F32),
            jax.ShapeDtypeStruct((2 * TOP_K, bsz * seq), jnp.int32),
            jax.ShapeDtypeStruct((bsz, POOL_MAX, D_B), F32),
            jax.ShapeDtypeStruct((N_EXPERTS, LANES), F32),
        ],
        scratch_shapes=[
            pltpu.VMEM((POOL_MAX, D_B), F32),
            pltpu.VMEM((N_EXPERTS, LANES), F32),
            pltpu.VMEM((tm, D_A), F32),
            pltpu.VMEM((tm, D_B), F32),
        ],
        compiler_params=pltpu.CompilerParams(
            dimension_semantics=("arbitrary", "arbitrary"), vmem_limit_bytes=VMEM_LIMIT),
    )(x, *weights)


def _mixer_sample(x2d, hist, cnt_in, route_p, weights, n_blocks):
    rows = x2d.shape[0]
    n_p = route_p.shape[1]
    n_streams, hist_rows, _ = hist.shape
    ext_rows = rows + n_streams * hist_rows
    tab_lanes = -(-n_blocks // LANES) * LANES
    full = lambda shape: pl.BlockSpec(shape, lambda i: (0,) * len(shape))
    i32 = jnp.int32
    out_shapes = [
        ((rows, D_MODEL), F32),
        ((rows, D_MODEL // 2), i32),
        ((rows, LANES), F32),
        ((2 * TOP_K, rows), i32),
        ((rows, D_A), F32),
        ((rows, D_B), F32),
        ((TOP_K, n_p), i32),
        ((TOP_K, rows), i32),
        ((N_EXPERTS, LANES), i32),
        ((N_EXPERTS, LANES), i32),
        ((SUBLANES, tab_lanes), i32),
    ]
    return pl.pallas_call(
        _mixer_sample_kernel,
        name="mixer_sample",
        grid=(1,),
        in_specs=[full((rows, D_MODEL)), full(hist.shape), full((N_EXPERTS, LANES)), full(route_p.shape)]
        + _mixer_weight_specs(),
        out_specs=[full(shape) for shape, _ in out_shapes],
        out_shape=[jax.ShapeDtypeStruct(shape, dtype) for shape, dtype in out_shapes],
        scratch_shapes=[
            pltpu.VMEM((N_EXPERTS, LANES), F32),
            pltpu.VMEM((rows, D_A), F32),
            pltpu.VMEM((rows, D_B), F32),
            pltpu.VMEM((ext_rows, D_B), F32),
        ],
        compiler_params=pltpu.CompilerParams(
            dimension_semantics=("arbitrary",), vmem_limit_bytes=VMEM_LIMIT),
    )(x2d, hist, cnt_in, route_p, *weights)


def _dispatch_kernel(n_prompt, n_sample, dest_ref, cnt_ref, region_ref, nused_ref,
                     hp_p_ref, hp_s_ref, xs_hbm, zbuf, sem, zsem):
    i = pl.program_id(0)
    tm_p = hp_p_ref.shape[0]
    tm_s = hp_s_ref.shape[0]
    n_prompt_tiles = n_prompt // tm_p
    n_blocks = xs_hbm.shape[0] // SLOT_BLOCK

    def pad_rows(e):
        cnt = cnt_ref[e]
        lo = region_ref[e] + cnt
        mid = region_ref[e] + (cnt + SUB_BLOCK - 1) // SUB_BLOCK * SUB_BLOCK
        hi = region_ref[e] + (cnt + SLOT_BLOCK - 1) // SLOT_BLOCK * SLOT_BLOCK
        return lo, mid, (hi - mid) // SUB_BLOCK

    def zero_row_copy(r):
        return pltpu.make_async_copy(zbuf.at[pl.ds(0, 1)], xs_hbm.at[pl.ds(r, 1)], zsem.at[0])

    def zero_sub_block_copy(r):
        return pltpu.make_async_copy(
            zbuf.at[pl.ds(0, SUB_BLOCK)], xs_hbm.at[pl.ds(pl.multiple_of(r, SUB_BLOCK), SUB_BLOCK)], zsem.at[1])

    def zero_block_copy(rb):
        return pltpu.make_async_copy(zbuf, xs_hbm.at[pl.ds(rb * SLOT_BLOCK, SLOT_BLOCK)], zsem.at[2])

    def for_each_fill(row_fn, sub_block_fn, block_fn):
        def per_expert(e, carry):
            lo, mid, n_sub = pad_rows(e)
            lax.fori_loop(lo, mid, lambda r, c: (row_fn(r), c)[1], 0)
            lax.fori_loop(0, n_sub, lambda j, c: (sub_block_fn(mid + j * SUB_BLOCK), c)[1], 0)
            return carry
        lax.fori_loop(0, N_EXPERTS, per_expert, 0)
        lax.fori_loop(nused_ref[0], n_blocks, lambda rb, c: (block_fn(rb), c)[1], 0)

    @pl.when(i == 0)
    def _():
        zbuf[...] = jnp.zeros_like(zbuf)
        for_each_fill(lambda r: zero_row_copy(r).start(), lambda r: zero_sub_block_copy(r).start(),
                      lambda rb: zero_block_copy(rb).start())

    def scatter(src_ref, first_index, k_stride):
        rows = src_ref.shape[0]

        def body(t, carry):
            for k in range(TOP_K):
                d = dest_ref[first_index + k * k_stride + t]
                pltpu.make_async_copy(
                    src_ref.at[pl.ds(t, 1)], xs_hbm.at[pl.ds(d, 1)], sem).start(priority=k % 2)
            return carry
        lax.fori_loop(0, rows, body, 0, unroll=8)
        for _ in range(TOP_K):
            pltpu.make_async_copy(src_ref, xs_hbm.at[pl.ds(0, rows)], sem).wait()

    @pl.when(i < n_prompt_tiles)
    def _():
        scatter(hp_p_ref, i * tm_p, n_prompt)

    @pl.when(i >= n_prompt_tiles)
    def _():
        scatter(hp_s_ref, TOP_K * n_prompt + (i - n_prompt_tiles) * tm_s, n_sample)

    @pl.when(i == 0)
    def _():
        for_each_fill(lambda r: zero_row_copy(r).wait(), lambda r: zero_sub_block_copy(r).wait(),
                      lambda rb: zero_block_copy(rb).wait())


def _dispatch(dest, counts, region, n_used, hp_p, hp_s, n_blocks):
    tm_p = min(TM_DISPATCH, hp_p.shape[0])
    tm_s = min(TM_DISPATCH, hp_s.shape[0])
    assert hp_p.shape[0] % tm_p == 0 and hp_s.shape[0] % tm_s == 0
    ntp = hp_p.shape[0] // tm_p
    nts = hp_s.shape[0] // tm_s
    half = D_MODEL // 2
    return pl.pallas_call(
        functools.partial(_dispatch_kernel, hp_p.shape[0], hp_s.shape[0]),
        name="dispatch",
        grid_spec=pltpu.PrefetchScalarGridSpec(
            num_scalar_prefetch=4,
            grid=(ntp + nts,),
            in_specs=[
                pl.BlockSpec((tm_p, half), lambda i, *_: (jnp.minimum(i, ntp - 1), 0)),
                pl.BlockSpec((tm_s, half), lambda i, *_: (jnp.maximum(i - ntp, 0), 0)),
            ],
            out_specs=pl.BlockSpec(memory_space=pl.ANY),
            scratch_shapes=[
                pltpu.VMEM((SLOT_BLOCK, half), jnp.int32),
                pltpu.SemaphoreType.DMA,
                pltpu.SemaphoreType.DMA((3,)),
            ],
        ),
        out_shape=jax.ShapeDtypeStruct((n_blocks * SLOT_BLOCK, half), jnp.int32),
        compiler_params=pltpu.CompilerParams(dimension_semantics=("arbitrary",)),
    )(dest, counts, region, n_used, hp_p, hp_s)


def _unpack_bf16_pairs(w):
    lo = lax.bitcast_convert_type(lax.shift_left(w, 16), F32).astype(BF16)
    hi = lax.bitcast_convert_type(w & jnp.int32(-65536), F32).astype(BF16)
    return jnp.concatenate([lo, hi], axis=1)


def _for_each_sub_block(nsub, compute, out_ref):
    n_sub_max = SLOT_BLOCK // SUB_BLOCK

    @pl.when(nsub == n_sub_max)
    def _():
        compute(pl.ds(0, SLOT_BLOCK))

    for s in range(n_sub_max):
        rows = pl.ds(s * SUB_BLOCK, SUB_BLOCK)

        @pl.when((s < nsub) & (nsub < n_sub_max))
        def _(rows=rows):
            compute(rows)

        @pl.when(s >= nsub)
        def _(rows=rows):
            out_ref[rows, :] = jnp.zeros((SUB_BLOCK, out_ref.shape[1]), out_ref.dtype)


def _grouped_call(name, step, grid, in_specs_fn, out_spec, operands, out_shape, scratch_shapes):
    n_in = len(operands)

    def outer(be_ref, first_ref, nsub_ref, *refs):
        ins, out, scratches = refs[:n_in], refs[n_in], refs[n_in + 1:]

        def body(*block_refs):
            step(be_ref, first_ref, nsub_ref, *block_refs)

        pltpu.emit_pipeline(
            body, grid=grid, in_specs=in_specs_fn(be_ref), out_specs=[out_spec],
        )(*ins, out, scratches=scratches)

    any_spec = pl.BlockSpec(memory_space=pl.ANY)
    return lambda block_expert, first, nsub: pl.pallas_call(
        outer,
        name=name,
        grid_spec=pltpu.PrefetchScalarGridSpec(
            num_scalar_prefetch=3,
            grid=(),
            in_specs=[any_spec] * n_in,
            out_specs=any_spec,
            scratch_shapes=scratch_shapes,
        ),
        out_shape=out_shape,
        compiler_params=pltpu.CompilerParams(vmem_limit_bytes=VMEM_LIMIT),
    )(block_expert, first, nsub, *operands)


def _expert_weight_spec(be_ref, shape):
    return pl.BlockSpec(shape, lambda j, rb: (be_ref[rb], 0, j),
                        pipeline_mode=pl.Buffered(2, use_lookahead=True))


def _ffn_up_step(be_ref, first_ref, nsub_ref, x_ref, wg_ref, bg_ref, wu_ref, bu_ref, act_ref, wgb, wub):
    rb = pl.program_id(1)
    expert = pl.ds(be_ref[rb], 1)

    @pl.when(first_ref[rb] == 1)
    def _():
        wgb[...] = wg_ref[0].astype(BF16)
        wub[...] = wu_ref[0].astype(BF16)

    def compute(rows):
        x = _unpack_bf16_pairs(x_ref[rows, :])
        gate = jnp.minimum(_dot(x, wgb[...]) + bg_ref[expert, :], SWIGLU_LIMIT)
        up = jnp.clip(_dot(x, wub[...]) + bu_ref[expert, :], -SWIGLU_LIMIT, SWIGLU_LIMIT)
        glu = gate * jax.nn.sigmoid(SWIGLU_ALPHA * gate)
        act_ref[rows, :] = ((up + 1.0) * glu).astype(BF16)

    _for_each_sub_block(nsub_ref[rb], compute, act_ref)


def _ffn_up(block_expert, first, nsub, xs, w_gate, b_gate, w_up, b_up, n_blocks):
    def in_specs(be_ref):
        w_spec = _expert_weight_spec(be_ref, (1, D_MODEL, FF_TILE))
        b_spec = pl.BlockSpec((N_EXPERTS, FF_TILE), lambda f, rb: (0, f))
        return [pl.BlockSpec((SLOT_BLOCK, D_MODEL // 2), lambda f, rb: (rb, 0)), w_spec, b_spec, w_spec, b_spec]

    return _grouped_call(
        "ffn_up", _ffn_up_step, (D_FF // FF_TILE, n_blocks), in_specs,
        pl.BlockSpec((SLOT_BLOCK, FF_TILE), lambda f, rb: (rb, f)),
        (xs, w_gate, b_gate, w_up, b_up),
        jax.ShapeDtypeStruct((n_blocks * SLOT_BLOCK, D_FF), BF16),
        [pltpu.VMEM((D_MODEL, FF_TILE), BF16), pltpu.VMEM((D_MODEL, FF_TILE), BF16)],
    )(block_expert, first, nsub)


def _ffn_down_step(be_ref, first_ref, nsub_ref, act_ref, wd_ref, bd_ref, y_ref, wdb):
    rb = pl.program_id(1)
    expert = pl.ds(be_ref[rb], 1)

    @pl.when(first_ref[rb] == 1)
    def _():
        wdb[...] = wd_ref[0].astype(BF16)

    def compute(rows):
        y = _dot(act_ref[rows, :], wdb[...]) + bd_ref[expert, :]
        y_ref[rows, :] = _pack_bf16_pairs(y.astype(BF16))

    _for_each_sub_block(nsub_ref[rb], compute, y_ref)


def _ffn_down(block_expert, first, nsub, act, w_down, b_down, n_blocks):
    def in_specs(be_ref):
        return [
            pl.BlockSpec((SLOT_BLOCK, D_FF), lambda n, rb: (rb, 0)),
            _expert_weight_spec(be_ref, (1, D_FF, D_MODEL)),
            pl.BlockSpec((N_EXPERTS, D_MODEL), lambda n, rb: (0, n)),
        ]

    return _grouped_call(
        "ffn_down", _ffn_down_step, (1, n_blocks), in_specs,
        pl.BlockSpec((SLOT_BLOCK, D_MODEL // 2), lambda n, rb: (rb, n)),
        (act, w_down, b_down),
        jax.ShapeDtypeStruct((n_blocks * SLOT_BLOCK, D_MODEL // 2), jnp.int32),
        [pltpu.VMEM((D_FF, D_MODEL), BF16)],
    )(block_expert, first, nsub)


def _combine_kernel(n_tokens, dest_ref, x1_ref, meta_ref, g_ref, ys_hbm, out_ref, ybuf0, ybuf1, sem0, sem1):
    i = pl.program_id(0)
    n_steps = pl.num_programs(0)
    tm = TM_COMBINE
    half = D_MODEL // 2

    def gather(tile, ybuf, sem):
        def body(grp, carry):
            for j in range(SUBLANES):
                token = tile * tm + grp * SUBLANES + j
                for k in range(TOP_K):
                    d = dest_ref[k * n_tokens + token]
                    pltpu.make_async_copy(
                        ys_hbm.at[lax.shift_right_logical(d, 3), pl.ds(d & (SUBLANES - 1), 1)],
                        ybuf.at[k, grp, pl.ds(j, 1)], sem).start(priority=k % 2)
            return carry
        lax.fori_loop(0, tm // SUBLANES, body, 0)

    def finish(r0, ybuf, sem):
        for k in range(TOP_K):
            pltpu.make_async_copy(ys_hbm.at[pl.ds(0, tm // SUBLANES)], ybuf.at[k], sem).wait()
        lo = x1_ref[r0:r0 + tm, 0:half]
        hi = x1_ref[r0:r0 + tm, half:D_MODEL]
        for k in range(TOP_K):
            w = ybuf[k].reshape(tm, half)
            gate = meta_ref[r0:r0 + tm, k:k + 1]
            lo = lo + lax.bitcast_convert_type(lax.shift_left(w, 16), F32) * gate
            hi = hi + lax.bitcast_convert_type(w & jnp.int32(-65536), F32) * gate
        ms = (jnp.sum(lo * lo, axis=-1, keepdims=True) + jnp.sum(hi * hi, axis=-1, keepdims=True)) / D_MODEL
        scale = lax.rsqrt(ms + EPS)
        out_ref[r0:r0 + tm, 0:half] = lo * scale * g_ref[:, 0:half]
        out_ref[r0:r0 + tm, half:D_MODEL] = hi * scale * g_ref[:, half:D_MODEL]

    @pl.when(i == 0)
    def _():
        gather(0, ybuf0, sem0)

    gather(2 * i + 1, ybuf1, sem1)
    finish(0, ybuf0, sem0)

    @pl.when(i + 1 < n_steps)
    def _():
        gather(2 * i + 2, ybuf0, sem0)

    finish(tm, ybuf1, sem1)


def _combine(dest_flat, x1, meta, g_final, ys):
    n = x1.shape[0]
    tm = 2 * TM_COMBINE
    assert n % tm == 0
    ybuf = pltpu.VMEM((TOP_K, TM_COMBINE // SUBLANES, SUBLANES, D_MODEL // 2), jnp.int32)
    ys = ys.reshape(ys.shape[0] // SUBLANES, SUBLANES, D_MODEL // 2)
    return pl.pallas_call(
        functools.partial(_combine_kernel, n),
        name="combine",
        grid_spec=pltpu.PrefetchScalarGridSpec(
            num_scalar_prefetch=1,
            grid=(n // tm,),
            in_specs=[
                pl.BlockSpec((tm, D_MODEL), lambda i, d: (i, 0)),
                pl.BlockSpec((tm, LANES), lambda i, d: (i, 0)),
                pl.BlockSpec((1, D_MODEL), lambda i, d: (0, 0)),
                pl.BlockSpec(memory_space=pl.ANY),
            ],
            out_specs=pl.BlockSpec((tm, D_MODEL), lambda i, d: (i, 0)),
            scratch_shapes=[ybuf, ybuf, pltpu.SemaphoreType.DMA, pltpu.SemaphoreType.DMA],
        ),
        out_shape=jax.ShapeDtypeStruct((n, D_MODEL), F32),
        compiler_params=pltpu.CompilerParams(
            dimension_semantics=("arbitrary",), vmem_limit_bytes=VMEM_LIMIT),
    )(dest_flat, x1, meta, g_final, ys)


def kernel(x_prompt, x_sample, cache_pool, norm_mix_g, w_in, ln_v_g, ln_v_b, w_spatial, b_spatial, w_pool,
           pool_scale, out_norm_a_g, out_norm_b_g, w_out, norm_ffn_g, w_router, b_router, w_gate, b_gate,
           w_up, b_up, w_down, b_down, final_norm_g):
    depth = norm_mix_g.shape[0]
    assert depth == 1
    bsz, seq, _ = x_prompt.shape
    dec_b, dec_t, _ = x_sample.shape
    assert seq % TM_PROMPT == 0 and TM_PROMPT % GMLP_CHUNK == 0
    assert GMLP_CHUNK % dec_t == 0 and (dec_b * dec_t) % GMLP_CHUNK == 0 and PAST_LEN % GMLP_CHUNK == 0
    assert dec_t >= POOL_STATE and PAST_LEN + 1 >= POOL_MAX
    n_p = bsz * seq
    n_s = dec_b * dec_t
    n = n_p + n_s
    assert n_p % TM_COMBINE == 0 and n_s % TM_COMBINE == 0
    l = 0

    row = lambda v: v.reshape(1, -1)
    shared = [
        row(norm_mix_g[l]), w_in[l].astype(BF16), row(ln_v_g[l]), row(ln_v_b[l]),
    ]
    tail = [
        w_pool[l].astype(BF16), row(pool_scale[l]), row(out_norm_a_g[l]), row(out_norm_b_g[l]),
        w_out[l].astype(BF16), row(norm_ffn_g[l]),
        jnp.pad(w_router[l].astype(BF16), ((0, 0), (0, LANES - N_EXPERTS))),
        jnp.pad(row(b_router[l]), ((0, 0), (0, LANES - N_EXPERTS))),
    ]
    weights_p = shared + [w_spatial[l], b_spatial[l].T] + tail
    reps = GMLP_CHUNK // dec_t
    weights_s = shared + [
        jnp.tile(w_spatial[l][:, :dec_t, :dec_t], (1, reps, reps)),
        jnp.tile(b_spatial[l][:, :dec_t].T, (reps, 1)),
    ] + tail

    x1_p, hp_p, meta_p, route_p, ptail, cnt_p = _mixer_prompt(x_prompt, weights_p)
    hist = jnp.pad(cache_pool[l], ((0, 0), (POOL_MAX - POOL_STATE, 0), (0, 0)))
    n_blocks = -(-(n * TOP_K + N_EXPERTS * (SLOT_BLOCK - 1)) // SLOT_BLOCK)
    x1_s, hp_s, meta_s, _, vn_s, p_s, dest_p, dest_s, cnt, region, tables = _mixer_sample(
        x_sample.reshape(n_s, D_MODEL), hist, cnt_p, route_p, weights_s, n_blocks)

    dest_p = dest_p.reshape(-1)
    dest_s = dest_s.reshape(-1)
    block_expert = tables[0, :n_blocks]
    first = tables[1, :n_blocks]
    nsub = tables[2, :n_blocks]
    n_used = tables[3, :1]

    xs = _dispatch(jnp.concatenate([dest_p, dest_s]), cnt[:, 0], region[:, 0], n_used, hp_p, hp_s, n_blocks)
    act = _ffn_up(block_expert, first, nsub, xs, w_gate[l], b_gate[l], w_up[l], b_up[l], n_blocks)
    ys = _ffn_down(block_expert, first, nsub, act, w_down[l], b_down[l], n_blocks)

    g_final = row(final_norm_g)
    y_p = _combine(dest_p, x1_p.reshape(n_p, D_MODEL), meta_p, g_final, ys)
    y_s = _combine(dest_s, x1_s, meta_s, g_final, ys)

    y_prompt = y_p.reshape(bsz, seq, D_MODEL)
    y_sample = y_s.reshape(dec_b, dec_t, D_MODEL)
    state_pool_prompt = ptail[:, POOL_MAX - POOL_STATE:][None]
    p_s3 = p_s.reshape(dec_b, dec_t, D_B)
    state_pool_sample = p_s3[:, dec_t - POOL_STATE:][None]
    state_chunk_v_sample = vn_s.reshape(dec_b, dec_t, D_A)[None]
    return (y_prompt, y_sample, state_pool_prompt, state_pool_sample, state_chunk_v_sample)
```

```python
import functools

import jax
import jax.numpy as jnp
from jax import lax
from jax.experimental import pallas as pl
from jax.experimental.pallas import tpu as pltpu

D_MODEL = 2048
D_A = 1024
D_B = 1024
GMLP_CHUNK = 128
N_HEADS_A = 8
HEAD_DIM_A = D_A // N_HEADS_A
POOL_WINDOWS = (2, 4, 8, 16)
N_POOL_GROUPS = len(POOL_WINDOWS)
POOL_GROUP_DIM = D_B // N_POOL_GROUPS
POOL_MAX = 16
POOL_STATE = POOL_MAX - 1
PAST_LEN = 1024
N_EXPERTS = 32
TOP_K = 4
D_FF = 2048
SWIGLU_LIMIT = 7.0
SWIGLU_ALPHA = 1.702
EPS = 1e-5

LANES = 128
SUBLANES = 8
TM_PROMPT = 256
SLOT_BLOCK = 512
SUB_BLOCK = 128
TM_DISPATCH = 1024
FF_TILE = 1024
TM_COMBINE = 128
VMEM_LIMIT = 56 * 1024 * 1024

BF16 = jnp.bfloat16
F32 = jnp.float32


def _dot(a, b):
    return jnp.dot(a, b, preferred_element_type=F32)


def _rms_norm(x, g):
    return x * lax.rsqrt(jnp.mean(x * x, axis=-1, keepdims=True) + EPS) * g


def _layer_norm(x, g, b):
    mu = jnp.mean(x, axis=-1, keepdims=True)
    xc = x - mu
    return xc * lax.rsqrt(jnp.mean(xc * xc, axis=-1, keepdims=True) + EPS) * g + b


def _gelu(x):
    return 0.5 * x * (1.0 + lax.erf(x * (2.0 ** -0.5)))


def _mixer_front(x, g_mix, w_in, ln_g, ln_b):
    h = _rms_norm(x, g_mix[...]).astype(BF16)
    u = _gelu(_dot(h, w_in[:, 0:D_A]))
    vn = _layer_norm(_dot(h, w_in[:, D_A:2 * D_A]), ln_g[...], ln_b[...])
    p = _dot(h, w_in[:, 2 * D_A:2 * D_A + D_B])
    return u, vn, p


def _spatial_gate(u, vn, ws, bs_t, chunk, a_ref):
    rows = u.shape[0]
    t = lax.broadcasted_iota(jnp.int32, (GMLP_CHUNK, GMLP_CHUNK), 0)
    s = lax.broadcasted_iota(jnp.int32, (GMLP_CHUNK, GMLP_CHUNK), 1)
    mask = (s <= t) & ((t // chunk) == (s // chunk))
    vb = vn.astype(BF16)
    for h in range(N_HEADS_A):
        w = jnp.where(mask, ws[h], 0.0).astype(BF16)
        bias = bs_t[:, h:h + 1]
        c0 = h * HEAD_DIM_A
        for r0 in range(0, rows, GMLP_CHUNK):
            mixed = _dot(w, vb[r0:r0 + GMLP_CHUNK, c0:c0 + HEAD_DIM_A]) + bias
            a_ref[r0:r0 + GMLP_CHUNK, c0:c0 + HEAD_DIM_A] = u[r0:r0 + GMLP_CHUNK, c0:c0 + HEAD_DIM_A] * mixed


def _window_sums(e):
    parts = []
    s = e
    for g, w in enumerate(POOL_WINDOWS):
        s = s + pltpu.roll(s, w // 2, 0)
        parts.append(s[:, 0:POOL_GROUP_DIM])
        if g + 1 < N_POOL_GROUPS:
            s = s[:, POOL_GROUP_DIM:]
    return parts


def _pool_project(diffs, w_pool, pool_scale, b_ref):
    for g in range(N_POOL_GROUPS):
        c0 = g * POOL_GROUP_DIM
        out = _dot(diffs[g].astype(BF16), w_pool[g])
        b_ref[:, c0:c0 + POOL_GROUP_DIM] = out * pool_scale[:, c0:c0 + POOL_GROUP_DIM]


def _mixer_back(x, a_ref, b_ref, g_a, g_b, w_out, g_ffn):
    na = _rms_norm(a_ref[...], g_a[...]).astype(BF16)
    nb = _rms_norm(b_ref[...], g_b[...]).astype(BF16)
    x1 = x + _dot(na, w_out[0:D_A, :]) + _dot(nb, w_out[D_A:D_A + D_B, :])
    hb = _rms_norm(x1, g_ffn[...]).astype(BF16)
    return x1, hb


def _pack_bf16_pairs(hb):
    half = D_MODEL // 2
    lo = lax.bitcast_convert_type(hb[:, 0:half].astype(F32), jnp.int32)
    hi = lax.bitcast_convert_type(hb[:, half:D_MODEL].astype(F32), jnp.int32)
    return lax.shift_right_logical(lo, 16) | hi


def _route(hb, w_router, b_router, run_ref):
    rows = hb.shape[0]
    logits = _dot(hb, w_router[...]) + b_router[...]
    l = logits.T[0:N_EXPERTS]
    sub = lax.broadcasted_iota(jnp.int32, (N_EXPERTS, rows), 0)
    vals, idxs = [], []
    for _ in range(TOP_K):
        m = jnp.max(l, axis=0, keepdims=True)
        i = jnp.min(jnp.where(l == m, sub, N_EXPERTS), axis=0, keepdims=True)
        vals.append(m)
        idxs.append(i)
        l = jnp.where(sub == i, -jnp.inf, l)
    exps = [jnp.exp(v - vals[0]) for v in vals]
    denom = exps[0] + exps[1] + exps[2] + exps[3]
    gates = [e / denom for e in exps]

    onehot = jnp.zeros((N_EXPERTS, rows), F32)
    for i in idxs:
        onehot = jnp.where(sub == i, 1.0, onehot)
    s = lax.broadcasted_iota(jnp.int32, (rows, rows), 0)
    t = lax.broadcasted_iota(jnp.int32, (rows, rows), 1)
    earlier = jnp.where(s < t, 1.0, 0.0).astype(BF16)
    base = _dot(onehot.astype(BF16), earlier) + run_ref[:, 0:1]
    ranks = [jnp.sum(jnp.where(sub == i, base, 0.0), axis=0, keepdims=True) for i in idxs]
    run_ref[...] = run_ref[...] + jnp.sum(onehot, axis=1, keepdims=True)

    route_t = jnp.concatenate(idxs + [r.astype(jnp.int32) for r in ranks], axis=0)
    sub_g = lax.broadcasted_iota(jnp.int32, (LANES, rows), 0)
    gates_t = jnp.zeros((LANES, rows), F32)
    for j, g in enumerate(gates):
        gates_t = jnp.where(sub_g == j, g, gates_t)
    return gates_t.T, route_t


def _mixer_prompt_kernel(x_ref, g_mix, w_in, ln_g, ln_b, ws, bs_t, w_pool, pool_scale, g_a, g_b, w_out,
                         g_ffn, w_router, b_router,
                         x1_ref, hp_ref, meta_ref, route_ref, ptail_ref, cnt_ref,
                         halo_ref, run_ref, a_ref, b_ref):
    tm = x_ref.shape[1]
    j = pl.program_id(1)

    @pl.when((pl.program_id(0) == 0) & (j == 0))
    def _():
        run_ref[...] = jnp.zeros_like(run_ref)

    @pl.when(j == 0)
    def _():
        halo_ref[...] = jnp.zeros_like(halo_ref)

    x = x_ref[0]
    u, vn, p = _mixer_front(x, g_mix, w_in, ln_g, ln_b)
    _spatial_gate(u, vn, ws, bs_t, GMLP_CHUNK, a_ref)

    e = jnp.concatenate([halo_ref[...], p], axis=0)
    halo_ref[...] = p[tm - POOL_MAX:tm]
    ptail_ref[0] = p[tm - POOL_MAX:tm]
    pos = j * tm + lax.broadcasted_iota(jnp.int32, (tm, 1), 0)
    sums = _window_sums(e)
    diffs = []
    for g, w in enumerate(POOL_WINDOWS):
        inv_cnt = 1.0 / jnp.minimum(pos + 1, w).astype(F32)
        c0 = g * POOL_GROUP_DIM
        diffs.append(sums[g][POOL_MAX:] * inv_cnt - p[:, c0:c0 + POOL_GROUP_DIM])
    _pool_project(diffs, w_pool, pool_scale, b_ref)

    x1, hb = _mixer_back(x, a_ref, b_ref, g_a, g_b, w_out, g_ffn)
    x1_ref[0] = x1
    hp_ref[...] = _pack_bf16_pairs(hb)
    meta_ref[...], route_ref[...] = _route(hb, w_router, b_router, run_ref)
    cnt_ref[...] = run_ref[...]


def _slot_layout(run_ref, route_refs, dest_refs, cnt_ref, region_ref, tab_ref):
    cnt = run_ref[...].astype(jnp.int32)
    sub = lax.broadcasted_iota(jnp.int32, cnt.shape, 0)
    blocks = (cnt + (SLOT_BLOCK - 1)) // SLOT_BLOCK
    blk_end = blocks
    shift = 1
    while shift < N_EXPERTS:
        blk_end = blk_end + jnp.where(sub >= shift, pltpu.roll(blk_end, shift, 0), 0)
        shift *= 2
    blk_start = blk_end - blocks
    region = blk_start * SLOT_BLOCK
    rows_last = cnt - (blocks - 1) * SLOT_BLOCK
    cnt_ref[...] = cnt
    region_ref[...] = region

    def of_expert(table, e):
        return table[e:e + 1, 0:1]

    for route_ref, dest_ref in zip(route_refs, dest_refs):
        idx = route_ref[0:TOP_K, :]
        first_slot = jnp.zeros_like(idx)
        for e in range(N_EXPERTS):
            first_slot = jnp.where(idx == e, of_expert(region, e), first_slot)
        dest_ref[...] = first_slot + route_ref[TOP_K:2 * TOP_K, :]

    n_used = of_expert(blk_end, N_EXPERTS - 1)
    blk = lax.broadcasted_iota(jnp.int32, (1, tab_ref.shape[1]), 1)
    blk_c = jnp.minimum(blk, n_used - 1)
    expert = jnp.zeros_like(blk)
    for e in range(N_EXPERTS - 1):
        expert = expert + jnp.where(blk_c >= of_expert(blk_end, e), 1, 0)
    start = jnp.zeros_like(blk)
    end = jnp.zeros_like(blk)
    last_rows = jnp.zeros_like(blk)
    for e in range(N_EXPERTS):
        hit = expert == e
        start = jnp.where(hit, of_expert(blk_start, e), start)
        end = jnp.where(hit, of_expert(blk_end, e), end)
        last_rows = jnp.where(hit, of_expert(rows_last, e), last_rows)
    used = blk < n_used
    first = jnp.where((blk_c == start) & used, 1, 0)
    sub_last = (last_rows + (SUB_BLOCK - 1)) // SUB_BLOCK
    nsub = jnp.where(used, jnp.where(blk_c == end - 1, sub_last, SLOT_BLOCK // SUB_BLOCK), 0)
    n_used_row = jnp.zeros_like(blk) + n_used
    zero = jnp.zeros_like(blk)
    tab_ref[...] = jnp.concatenate([expert, first, nsub, n_used_row, zero, zero, zero, zero], axis=0)


def _mixer_sample_kernel(x_ref, hist_ref, cnt_in_ref, route_p_ref,
                         g_mix, w_in, ln_g, ln_b, ws, bs_t, w_pool, pool_scale,
                         g_a, g_b, w_out, g_ffn, w_router, b_router,
                         x1_ref, hp_ref, meta_ref, route_ref, vn_ref, p_ref,
                         dest_p_ref, dest_s_ref, cnt_ref, region_ref, tab_ref,
                         run_ref, a_ref, b_ref, ext_ref):
    n_streams, hist_rows, _ = hist_ref.shape
    t_new = x_ref.shape[0] // n_streams
    group = hist_rows + t_new
    run_ref[...] = cnt_in_ref[...]

    x = x_ref[...]
    u, vn, p = _mixer_front(x, g_mix, w_in, ln_g, ln_b)
    vn_ref[...] = vn
    p_ref[...] = p
    _spatial_gate(u, vn, ws, bs_t, t_new, a_ref)

    for b in range(n_streams):
        ext_ref[b * group:b * group + hist_rows, :] = hist_ref[b]
        ext_ref[b * group + hist_rows:(b + 1) * group, :] = p[b * t_new:(b + 1) * t_new]
    e = ext_ref[...]
    sums = _window_sums(e)
    diffs = []
    for g, w in enumerate(POOL_WINDOWS):
        c0 = g * POOL_GROUP_DIM
        d = sums[g] * (1.0 / w) - e[:, c0:c0 + POOL_GROUP_DIM]
        diffs.append(jnp.concatenate(
            [d[b * group + hist_rows:(b + 1) * group] for b in range(n_streams)], axis=0))
    _pool_project(diffs, w_pool, pool_scale, b_ref)

    x1, hb = _mixer_back(x, a_ref, b_ref, g_a, g_b, w_out, g_ffn)
    x1_ref[...] = x1
    hp_ref[...] = _pack_bf16_pairs(hb)
    meta_ref[...], route_ref[...] = _route(hb, w_router, b_router, run_ref)
    _slot_layout(run_ref, (route_p_ref, route_ref), (dest_p_ref, dest_s_ref), cnt_ref, region_ref, tab_ref)


def _const_spec(shape):
    zeros = (0,) * len(shape)
    return pl.BlockSpec(shape, lambda *_: zeros, pipeline_mode=pl.Buffered(1))


def _mixer_weight_specs():
    return [
        _const_spec((1, D_MODEL)),
        _const_spec((D_MODEL, 2 * D_A + D_B)),
        _const_spec((1, D_A)), _const_spec((1, D_A)),
        _const_spec((N_HEADS_A, GMLP_CHUNK, GMLP_CHUNK)),
        _const_spec((GMLP_CHUNK, N_HEADS_A)),
        _const_spec((N_POOL_GROUPS, POOL_GROUP_DIM, POOL_GROUP_DIM)),
        _const_spec((1, D_B)),
        _const_spec((1, D_A)), _const_spec((1, D_B)),
        _const_spec((D_A + D_B, D_MODEL)),
        _const_spec((1, D_MODEL)),
        _const_spec((D_MODEL, LANES)),
        _const_spec((1, LANES)),
    ]


def _mixer_prompt(x, weights):
    bsz, seq, _ = x.shape
    tm = TM_PROMPT
    nt = seq // tm
    return pl.pallas_call(
        _mixer_prompt_kernel,
        name="mixer_prompt",
        grid=(bsz, nt),
        in_specs=[pl.BlockSpec((1, tm, D_MODEL), lambda b, j: (b, j, 0))] + _mixer_weight_specs(),
        out_specs=[
            pl.BlockSpec((1, tm, D_MODEL), lambda b, j: (b, j, 0)),
            pl.BlockSpec((tm, D_MODEL // 2), lambda b, j: (b * nt + j, 0)),
            pl.BlockSpec((tm, LANES), lambda b, j: (b * nt + j, 0)),
            pl.BlockSpec((2 * TOP_K, tm), lambda b, j: (0, b * nt + j)),
            pl.BlockSpec((1, POOL_MAX, D_B), lambda b, j: (b, 0, 0)),
            pl.BlockSpec((N_EXPERTS, LANES), lambda b, j: (0, 0)),
        ],
        out_shape=[
            jax.ShapeDtypeStruct((bsz, seq, D_MODEL), F32),
            jax.ShapeDtypeStruct((bsz * seq, D_MODEL // 2), jnp.int32),
            jax.ShapeDtypeStruct((bsz * seq, LANES), F32),
            jax.ShapeDtypeStruct((2 * TOP_K, bsz * seq), jnp.int32),
            jax.ShapeDtypeStruct((bsz, POOL_MAX, D_B), F32),
            jax.ShapeDtypeStruct((N_EXPERTS, LANES), F32),
        ],
        scratch_shapes=[
            pltpu.VMEM((POOL_MAX, D_B), F32),
            pltpu.VMEM((N_EXPERTS, LANES), F32),
            pltpu.VMEM((tm, D_A), F32),
            pltpu.VMEM((tm, D_B), F32),
        ],
        compiler_params=pltpu.CompilerParams(
            dimension_semantics=("arbitrary", "arbitrary"), vmem_limit_bytes=VMEM_LIMIT),
    )(x, *weights)


def _mixer_sample(x2d, hist, cnt_in, route_p, weights, n_blocks):
    rows = x2d.shape[0]
    n_p = route_p.shape[1]
    n_streams, hist_rows, _ = hist.shape
    ext_rows = rows + n_streams * hist_rows
    tab_lanes = -(-n_blocks // LANES) * LANES
    full = lambda shape: pl.BlockSpec(shape, lambda i: (0,) * len(shape))
    i32 = jnp.int32
    out_shapes = [
        ((rows, D_MODEL), F32),
        ((rows, D_MODEL // 2), i32),
        ((rows, LANES), F32),
        ((2 * TOP_K, rows), i32),
        ((rows, D_A), F32),
        ((rows, D_B), F32),
        ((TOP_K, n_p), i32),
        ((TOP_K, rows), i32),
        ((N_EXPERTS, LANES), i32),
        ((N_EXPERTS, LANES), i32),
        ((SUBLANES, tab_lanes), i32),
    ]
    return pl.pallas_call(
        _mixer_sample_kernel,
        name="mixer_sample",
        grid=(1,),
        in_specs=[full((rows, D_MODEL)), full(hist.shape), full((N_EXPERTS, LANES)), full(route_p.shape)]
        + _mixer_weight_specs(),
        out_specs=[full(shape) for shape, _ in out_shapes],
        out_shape=[jax.ShapeDtypeStruct(shape, dtype) for shape, dtype in out_shapes],
        scratch_shapes=[
            pltpu.VMEM((N_EXPERTS, LANES), F32),
            pltpu.VMEM((rows, D_A), F32),
            pltpu.VMEM((rows, D_B), F32),
            pltpu.VMEM((ext_rows, D_B), F32),
        ],
        compiler_params=pltpu.CompilerParams(
            dimension_semantics=("arbitrary",), vmem_limit_bytes=VMEM_LIMIT),
    )(x2d, hist, cnt_in, route_p, *weights)


def _dispatch_kernel(n_prompt, n_sample, dest_ref, cnt_ref, region_ref, nused_ref,
                     hp_p_ref, hp_s_ref, xs_hbm, zbuf, sem, zsem):
    i = pl.program_id(0)
    tm_p = hp_p_ref.shape[0]
    tm_s = hp_s_ref.shape[0]
    n_prompt_tiles = n_prompt // tm_p
    n_blocks = xs_hbm.shape[0] // SLOT_BLOCK

    def pad_rows(e):
        cnt = cnt_ref[e]
        lo = region_ref[e] + cnt
        mid = region_ref[e] + (cnt + SUB_BLOCK - 1) // SUB_BLOCK * SUB_BLOCK
        hi = region_ref[e] + (cnt + SLOT_BLOCK - 1) // SLOT_BLOCK * SLOT_BLOCK
        return lo, mid, (hi - mid) // SUB_BLOCK

    def zero_row_copy(r):
        return pltpu.make_async_copy(zbuf.at[pl.ds(0, 1)], xs_hbm.at[pl.ds(r, 1)], zsem.at[0])

    def zero_sub_block_copy(r):
        return pltpu.make_async_copy(
            zbuf.at[pl.ds(0, SUB_BLOCK)], xs_hbm.at[pl.ds(pl.multiple_of(r, SUB_BLOCK), SUB_BLOCK)], zsem.at[1])

    def zero_block_copy(rb):
        return pltpu.make_async_copy(zbuf, xs_hbm.at[pl.ds(rb * SLOT_BLOCK, SLOT_BLOCK)], zsem.at[2])

    def for_each_fill(row_fn, sub_block_fn, block_fn):
        def per_expert(e, carry):
            lo, mid, n_sub = pad_rows(e)
            lax.fori_loop(lo, mid, lambda r, c: (row_fn(r), c)[1], 0)
            lax.fori_loop(0, n_sub, lambda j, c: (sub_block_fn(mid + j * SUB_BLOCK), c)[1], 0)
            return carry
        lax.fori_loop(0, N_EXPERTS, per_expert, 0)
        lax.fori_loop(nused_ref[0], n_blocks, lambda rb, c: (block_fn(rb), c)[1], 0)

    @pl.when(i == 0)
    def _():
        zbuf[...] = jnp.zeros_like(zbuf)
        for_each_fill(lambda r: zero_row_copy(r).start(), lambda r: zero_sub_block_copy(r).start(),
                      lambda rb: zero_block_copy(rb).start())

    def scatter(src_ref, first_index, k_stride):
        rows = src_ref.shape[0]

        def body(t, carry):
            for k in range(TOP_K):
                d = dest_ref[first_index + k * k_stride + t]
                pltpu.make_async_copy(
                    src_ref.at[pl.ds(t, 1)], xs_hbm.at[pl.ds(d, 1)], sem).start(priority=k % 2)
            return carry
        lax.fori_loop(0, rows, body, 0, unroll=8)
        for _ in range(TOP_K):
            pltpu.make_async_copy(src_ref, xs_hbm.at[pl.ds(0, rows)], sem).wait()

    @pl.when(i < n_prompt_tiles)
    def _():
        scatter(hp_p_ref, i * tm_p, n_prompt)

    @pl.when(i >= n_prompt_tiles)
    def _():
        scatter(hp_s_ref, TOP_K * n_prompt + (i - n_prompt_tiles) * tm_s, n_sample)

    @pl.when(i == 0)
    def _():
        for_each_fill(lambda r: zero_row_copy(r).wait(), lambda r: zero_sub_block_copy(r).wait(),
                      lambda rb: zero_block_copy(rb).wait())


def _dispatch(dest, counts, region, n_used, hp_p, hp_s, n_blocks):
    tm_p = min(TM_DISPATCH, hp_p.shape[0])
    tm_s = min(TM_DISPATCH, hp_s.shape[0])
    assert hp_p.shape[0] % tm_p == 0 and hp_s.shape[0] % tm_s == 0
    ntp = hp_p.shape[0] // tm_p
    nts = hp_s.shape[0] // tm_s
    half = D_MODEL // 2
    return pl.pallas_call(
        functools.partial(_dispatch_kernel, hp_p.shape[0], hp_s.shape[0]),
        name="dispatch",
        grid_spec=pltpu.PrefetchScalarGridSpec(
            num_scalar_prefetch=4,
            grid=(ntp + nts,),
            in_specs=[
                pl.BlockSpec((tm_p, half), lambda i, *_: (jnp.minimum(i, ntp - 1), 0)),
                pl.BlockSpec((tm_s, half), lambda i, *_: (jnp.maximum(i - ntp, 0), 0)),
            ],
            out_specs=pl.BlockSpec(memory_space=pl.ANY),
            scratch_shapes=[
                pltpu.VMEM((SLOT_BLOCK, half), jnp.int32),
                pltpu.SemaphoreType.DMA,
                pltpu.SemaphoreType.DMA((3,)),
            ],
        ),
        out_shape=jax.ShapeDtypeStruct((n_blocks * SLOT_BLOCK, half), jnp.int32),
        compiler_params=pltpu.CompilerParams(dimension_semantics=("arbitrary",)),
    )(dest, counts, region, n_used, hp_p, hp_s)


def _unpack_bf16_pairs(w):
    lo = lax.bitcast_convert_type(lax.shift_left(w, 16), F32).astype(BF16)
    hi = lax.bitcast_convert_type(w & jnp.int32(-65536), F32).astype(BF16)
    return jnp.concatenate([lo, hi], axis=1)


def _for_each_sub_block(nsub, compute, out_ref):
    n_sub_max = SLOT_BLOCK // SUB_BLOCK

    @pl.when(nsub == n_sub_max)
    def _():
        compute(pl.ds(0, SLOT_BLOCK))

    for s in range(n_sub_max):
        rows = pl.ds(s * SUB_BLOCK, SUB_BLOCK)

        @pl.when((s < nsub) & (nsub < n_sub_max))
        def _(rows=rows):
            compute(rows)

        @pl.when(s >= nsub)
        def _(rows=rows):
            out_ref[rows] = jnp.zeros((SUB_BLOCK,) + tuple(out_ref.shape[1:]), out_ref.dtype)


def _grouped_call(name, step, grid, in_specs_fn, out_spec, operands, out_shape, scratch_shapes):
    n_in = len(operands)

    def outer(be_ref, first_ref, nsub_ref, *refs):
        ins, out, scratches = refs[:n_in], refs[n_in], refs[n_in + 1:]

        def body(*block_refs):
            step(be_ref, first_ref, nsub_ref, *block_refs)

        pltpu.emit_pipeline(
            body, grid=grid, in_specs=in_specs_fn(be_ref), out_specs=[out_spec],
        )(*ins, out, scratches=scratches)

    any_spec = pl.BlockSpec(memory_space=pl.ANY)
    return lambda block_expert, first, nsub: pl.pallas_call(
        outer,
        name=name,
        grid_spec=pltpu.PrefetchScalarGridSpec(
            num_scalar_prefetch=3,
            grid=(),
            in_specs=[any_spec] * n_in,
            out_specs=any_spec,
            scratch_shapes=scratch_shapes,
        ),
        out_shape=out_shape,
        compiler_params=pltpu.CompilerParams(vmem_limit_bytes=VMEM_LIMIT),
    )(block_expert, first, nsub, *operands)


def _expert_weight_spec(be_ref, shape):
    return pl.BlockSpec(shape, lambda j, rb: (be_ref[rb], 0, j),
                        pipeline_mode=pl.Buffered(2, use_lookahead=True))


def _ffn_up_step(be_ref, first_ref, nsub_ref, x_ref, wg_ref, bg_ref, wu_ref, bu_ref, act_ref, wgb, wub):
    rb = pl.program_id(1)
    expert = pl.ds(be_ref[rb], 1)

    @pl.when(first_ref[rb] == 1)
    def _():
        wgb[...] = wg_ref[0].astype(BF16)
        wub[...] = wu_ref[0].astype(BF16)

    def compute(rows):
        x = _unpack_bf16_pairs(x_ref[rows, :])
        gate = jnp.minimum(_dot(x, wgb[...]) + bg_ref[expert, :], SWIGLU_LIMIT)
        up = jnp.clip(_dot(x, wub[...]) + bu_ref[expert, :], -SWIGLU_LIMIT, SWIGLU_LIMIT)
        glu = gate * jax.nn.sigmoid(SWIGLU_ALPHA * gate)
        act_ref[rows, :] = ((up + 1.0) * glu).astype(BF16)

    _for_each_sub_block(nsub_ref[rb], compute, act_ref)


def _ffn_up(block_expert, first, nsub, xs, w_gate, b_gate, w_up, b_up, n_blocks):
    def in_specs(be_ref):
        w_spec = _expert_weight_spec(be_ref, (1, D_MODEL, FF_TILE))
        b_spec = pl.BlockSpec((N_EXPERTS, FF_TILE), lambda f, rb: (0, f))
        return [pl.BlockSpec((SLOT_BLOCK, D_MODEL // 2), lambda f, rb: (rb, 0)), w_spec, b_spec, w_spec, b_spec]

    return _grouped_call(
        "ffn_up", _ffn_up_step, (D_FF // FF_TILE, n_blocks), in_specs,
        pl.BlockSpec((SLOT_BLOCK, FF_TILE), lambda f, rb: (rb, f)),
        (xs, w_gate, b_gate, w_up, b_up),
        jax.ShapeDtypeStruct((n_blocks * SLOT_BLOCK, D_FF), BF16),
        [pltpu.VMEM((D_MODEL, FF_TILE), BF16), pltpu.VMEM((D_MODEL, FF_TILE), BF16)],
    )(block_expert, first, nsub)


def _ffn_down_step(be_ref, first_ref, nsub_ref, act_ref, wd_ref, bd_ref, y_ref, wdb):
    rb = pl.program_id(1)
    expert = pl.ds(be_ref[rb], 1)

    @pl.when(first_ref[rb] == 1)
    def _():
        wdb[...] = wd_ref[0].astype(BF16)

    def compute(rows):
        y = _dot(act_ref[rows, :], wdb[...]) + bd_ref[expert, :]
        y_ref[rows, :, :] = _pack_bf16_pairs(y.astype(BF16)).reshape(-1, SUBLANES, LANES)

    _for_each_sub_block(nsub_ref[rb], compute, y_ref)


def _ffn_down(block_expert, first, nsub, act, w_down, b_down, n_blocks):
    def in_specs(be_ref):
        return [
            pl.BlockSpec((SLOT_BLOCK, D_FF), lambda n, rb: (rb, 0)),
            _expert_weight_spec(be_ref, (1, D_FF, D_MODEL)),
            pl.BlockSpec((N_EXPERTS, D_MODEL), lambda n, rb: (0, n)),
        ]

    return _grouped_call(
        "ffn_down", _ffn_down_step, (1, n_blocks), in_specs,
        pl.BlockSpec((SLOT_BLOCK, SUBLANES, LANES), lambda n, rb: (rb, 0, 0)),
        (act, w_down, b_down),
        jax.ShapeDtypeStruct((n_blocks * SLOT_BLOCK, SUBLANES, LANES), jnp.int32),
        [pltpu.VMEM((D_FF, D_MODEL), BF16)],
    )(block_expert, first, nsub)


def _combine_kernel(n_tokens, dest_ref, x1_ref, meta_ref, g_ref, ys_hbm, out_ref, ybuf0, ybuf1, sem0, sem1):
    i = pl.program_id(0)
    n_steps = pl.num_programs(0)
    tm = TM_COMBINE
    half = D_MODEL // 2

    def gather(tile, ybuf, sem):
        def body(grp, carry):
            for j in range(SUBLANES):
                token = tile * tm + grp * SUBLANES + j
                for k in range(TOP_K):
                    d = dest_ref[k * n_tokens + token]
                    pltpu.make_async_copy(
                        ys_hbm.at[pl.ds(d, 1)],
                        ybuf.at[pl.ds(k * tm + grp * SUBLANES + j, 1)], sem).start(priority=k % 2)
            return carry
        lax.fori_loop(0, tm // SUBLANES, body, 0)

    def finish(r0, ybuf, sem):
        pltpu.make_async_copy(ys_hbm.at[pl.ds(0, TOP_K * tm)], ybuf, sem).wait()
        lo = x1_ref[r0:r0 + tm, 0:half]
        hi = x1_ref[r0:r0 + tm, half:D_MODEL]
        for k in range(TOP_K):
            w = ybuf[k * tm:(k + 1) * tm].reshape(tm, half)
            gate = meta_ref[r0:r0 + tm, k:k + 1]
            lo = lo + lax.bitcast_convert_type(lax.shift_left(w, 16), F32) * gate
            hi = hi + lax.bitcast_convert_type(w & jnp.int32(-65536), F32) * gate
        ms = (jnp.sum(lo * lo, axis=-1, keepdims=True) + jnp.sum(hi * hi, axis=-1, keepdims=True)) / D_MODEL
        scale = lax.rsqrt(ms + EPS)
        out_ref[r0:r0 + tm, 0:half] = lo * scale * g_ref[:, 0:half]
        out_ref[r0:r0 + tm, half:D_MODEL] = hi * scale * g_ref[:, half:D_MODEL]

    @pl.when(i == 0)
    def _():
        gather(0, ybuf0, sem0)

    gather(2 * i + 1, ybuf1, sem1)
    finish(0, ybuf0, sem0)

    @pl.when(i + 1 < n_steps)
    def _():
        gather(2 * i + 2, ybuf0, sem0)

    finish(tm, ybuf1, sem1)


def _combine(dest_flat, x1, meta, g_final, ys):
    n = x1.shape[0]
    tm = 2 * TM_COMBINE
    assert n % tm == 0
    ybuf = pltpu.VMEM((TOP_K * TM_COMBINE, SUBLANES, LANES), jnp.int32)
    return pl.pallas_call(
        functools.partial(_combine_kernel, n),
        name="combine",
        grid_spec=pltpu.PrefetchScalarGridSpec(
            num_scalar_prefetch=1,
            grid=(n // tm,),
            in_specs=[
                pl.BlockSpec((tm, D_MODEL), lambda i, d: (i, 0)),
                pl.BlockSpec((tm, LANES), lambda i, d: (i, 0)),
                pl.BlockSpec((1, D_MODEL), lambda i, d: (0, 0)),
                pl.BlockSpec(memory_space=pl.ANY),
            ],
            out_specs=pl.BlockSpec((tm, D_MODEL), lambda i, d: (i, 0)),
            scratch_shapes=[ybuf, ybuf, pltpu.SemaphoreType.DMA, pltpu.SemaphoreType.DMA],
        ),
        out_shape=jax.ShapeDtypeStruct((n, D_MODEL), F32),
        compiler_params=pltpu.CompilerParams(
            dimension_semantics=("arbitrary",), vmem_limit_bytes=VMEM_LIMIT),
    )(dest_flat, x1, meta, g_final, ys)


def kernel(x_prompt, x_sample, cache_pool, norm_mix_g, w_in, ln_v_g, ln_v_b, w_spatial, b_spatial, w_pool,
           pool_scale, out_norm_a_g, out_norm_b_g, w_out, norm_ffn_g, w_router, b_router, w_gate, b_gate,
           w_up, b_up, w_down, b_down, final_norm_g):
    depth = norm_mix_g.shape[0]
    assert depth == 1
    bsz, seq, _ = x_prompt.shape
    dec_b, dec_t, _ = x_sample.shape
    assert seq % TM_PROMPT == 0 and TM_PROMPT % GMLP_CHUNK == 0
    assert GMLP_CHUNK % dec_t == 0 and (dec_b * dec_t) % GMLP_CHUNK == 0 and PAST_LEN % GMLP_CHUNK == 0
    assert dec_t >= POOL_STATE and PAST_LEN + 1 >= POOL_MAX
    n_p = bsz * seq
    n_s = dec_b * dec_t
    n = n_p + n_s
    assert n_p % TM_COMBINE == 0 and n_s % TM_COMBINE == 0
    l = 0

    row = lambda v: v.reshape(1, -1)
    shared = [
        row(norm_mix_g[l]), w_in[l].astype(BF16), row(ln_v_g[l]), row(ln_v_b[l]),
    ]
    tail = [
        w_pool[l].astype(BF16), row(pool_scale[l]), row(out_norm_a_g[l]), row(out_norm_b_g[l]),
        w_out[l].astype(BF16), row(norm_ffn_g[l]),
        jnp.pad(w_router[l].astype(BF16), ((0, 0), (0, LANES - N_EXPERTS))),
        jnp.pad(row(b_router[l]), ((0, 0), (0, LANES - N_EXPERTS))),
    ]
    weights_p = shared + [w_spatial[l], b_spatial[l].T] + tail
    reps = GMLP_CHUNK // dec_t
    weights_s = shared + [
        jnp.tile(w_spatial[l][:, :dec_t, :dec_t], (1, reps, reps)),
        jnp.tile(b_spatial[l][:, :dec_t].T, (reps, 1)),
    ] + tail

    x1_p, hp_p, meta_p, route_p, ptail, cnt_p = _mixer_prompt(x_prompt, weights_p)
    hist = jnp.pad(cache_pool[l], ((0, 0), (POOL_MAX - POOL_STATE, 0), (0, 0)))
    n_blocks = -(-(n * TOP_K + N_EXPERTS * (SLOT_BLOCK - 1)) // SLOT_BLOCK)
    x1_s, hp_s, meta_s, _, vn_s, p_s, dest_p, dest_s, cnt, region, tables = _mixer_sample(
        x_sample.reshape(n_s, D_MODEL), hist, cnt_p, route_p, weights_s, n_blocks)

    dest_p = dest_p.reshape(-1)
    dest_s = dest_s.reshape(-1)
    block_expert = tables[0, :n_blocks]
    first = tables[1, :n_blocks]
    nsub = tables[2, :n_blocks]
    n_used = tables[3, :1]

    xs = _dispatch(jnp.concatenate([dest_p, dest_s]), cnt[:, 0], region[:, 0], n_used, hp_p, hp_s, n_blocks)
    act = _ffn_up(block_expert, first, nsub, xs, w_gate[l], b_gate[l], w_up[l], b_up[l], n_blocks)
    ys = _ffn_down(block_expert, first, nsub, act, w_down[l], b_down[l], n_blocks)

    g_final = row(final_norm_g)
    y_p = _combine(dest_p, x1_p.reshape(n_p, D_MODEL), meta_p, g_final, ys)
    y_s = _combine(dest_s, x1_s, meta_s, g_final, ys)

    y_prompt = y_p.reshape(bsz, seq, D_MODEL)
    y_sample = y_s.reshape(dec_b, dec_t, D_MODEL)
    state_pool_prompt = ptail[:, POOL_MAX - POOL_STATE:][None]
    p_s3 = p_s.reshape(dec_b, dec_t, D_B)
    state_pool_sample = p_s3[:, dec_t - POOL_STATE:][None]
    state_chunk_v_sample = vn_s.reshape(dec_b, dec_t, D_A)[None]
    return (y_prompt, y_sample, state_pool_prompt, state_pool_sample, state_chunk_v_sample)
```

```python
import functools

import jax
import jax.numpy as jnp
from jax import lax
from jax.experimental import pallas as pl
from jax.experimental.pallas import tpu as pltpu

D_MODEL = 2048
D_A = 1024
D_B = 1024
GMLP_CHUNK = 128
N_HEADS_A = 8
HEAD_DIM_A = D_A // N_HEADS_A
POOL_WINDOWS = (2, 4, 8, 16)
N_POOL_GROUPS = len(POOL_WINDOWS)
POOL_GROUP_DIM = D_B // N_POOL_GROUPS
POOL_MAX = 16
POOL_STATE = POOL_MAX - 1
PAST_LEN = 1024
N_EXPERTS = 32
TOP_K = 4
D_FF = 2048
SWIGLU_LIMIT = 7.0
SWIGLU_ALPHA = 1.702
EPS = 1e-5

LANES = 128
SUBLANES = 8
TM_PROMPT = 256
SLOT_BLOCK = 512
SUB_BLOCK = 128
TM_DISPATCH = 2048
FF_TILE = 1024
TM_COMBINE = 128
VMEM_LIMIT = 56 * 1024 * 1024

BF16 = jnp.bfloat16
F32 = jnp.float32


def _dot(a, b):
    return jnp.dot(a, b, preferred_element_type=F32)


def _rms_norm(x, g):
    return x * lax.rsqrt(jnp.mean(x * x, axis=-1, keepdims=True) + EPS) * g


def _layer_norm(x, g, b):
    mu = jnp.mean(x, axis=-1, keepdims=True)
    xc = x - mu
    return xc * lax.rsqrt(jnp.mean(xc * xc, axis=-1, keepdims=True) + EPS) * g + b


def _gelu(x):
    return 0.5 * x * (1.0 + lax.erf(x * (2.0 ** -0.5)))


def _mixer_front(x, g_mix, w_in, ln_g, ln_b):
    h = _rms_norm(x, g_mix[...]).astype(BF16)
    u = _gelu(_dot(h, w_in[:, 0:D_A]))
    vn = _layer_norm(_dot(h, w_in[:, D_A:2 * D_A]), ln_g[...], ln_b[...])
    p = _dot(h, w_in[:, 2 * D_A:2 * D_A + D_B])
    return u, vn, p


def _spatial_gate(u, vn, ws, bs_t, chunk, a_ref):
    rows = u.shape[0]
    t = lax.broadcasted_iota(jnp.int32, (GMLP_CHUNK, GMLP_CHUNK), 0)
    s = lax.broadcasted_iota(jnp.int32, (GMLP_CHUNK, GMLP_CHUNK), 1)
    mask = (s <= t) & ((t // chunk) == (s // chunk))
    vb = vn.astype(BF16)
    for h in range(N_HEADS_A):
        w = jnp.where(mask, ws[h], 0.0).astype(BF16)
        bias = bs_t[:, h:h + 1]
        c0 = h * HEAD_DIM_A
        for r0 in range(0, rows, GMLP_CHUNK):
            mixed = _dot(w, vb[r0:r0 + GMLP_CHUNK, c0:c0 + HEAD_DIM_A]) + bias
            a_ref[r0:r0 + GMLP_CHUNK, c0:c0 + HEAD_DIM_A] = u[r0:r0 + GMLP_CHUNK, c0:c0 + HEAD_DIM_A] * mixed


def _window_sums(e):
    parts = []
    s = e
    for g, w in enumerate(POOL_WINDOWS):
        s = s + pltpu.roll(s, w // 2, 0)
        parts.append(s[:, 0:POOL_GROUP_DIM])
        if g + 1 < N_POOL_GROUPS:
            s = s[:, POOL_GROUP_DIM:]
    return parts


def _pool_project(diffs, w_pool, pool_scale, b_ref):
    for g in range(N_POOL_GROUPS):
        c0 = g * POOL_GROUP_DIM
        out = _dot(diffs[g].astype(BF16), w_pool[g])
        b_ref[:, c0:c0 + POOL_GROUP_DIM] = out * pool_scale[:, c0:c0 + POOL_GROUP_DIM]


def _mixer_back(x, a_ref, b_ref, g_a, g_b, w_out, g_ffn):
    na = _rms_norm(a_ref[...], g_a[...]).astype(BF16)
    nb = _rms_norm(b_ref[...], g_b[...]).astype(BF16)
    x1 = x + _dot(na, w_out[0:D_A, :]) + _dot(nb, w_out[D_A:D_A + D_B, :])
    hb = _rms_norm(x1, g_ffn[...]).astype(BF16)
    return x1, hb


def _pack_bf16_pairs(hb):
    half = D_MODEL // 2
    lo = lax.bitcast_convert_type(hb[:, 0:half].astype(F32), jnp.int32)
    hi = lax.bitcast_convert_type(hb[:, half:D_MODEL].astype(F32), jnp.int32)
    return lax.shift_right_logical(lo, 16) | hi


def _route(hb, w_router, b_router, run_ref):
    rows = hb.shape[0]
    logits = _dot(hb, w_router[...]) + b_router[...]
    l = logits.T[0:N_EXPERTS]
    sub = lax.broadcasted_iota(jnp.int32, (N_EXPERTS, rows), 0)
    vals, idxs = [], []
    for _ in range(TOP_K):
        m = jnp.max(l, axis=0, keepdims=True)
        i = jnp.min(jnp.where(l == m, sub, N_EXPERTS), axis=0, keepdims=True)
        vals.append(m)
        idxs.append(i)
        l = jnp.where(sub == i, -jnp.inf, l)
    exps = [jnp.exp(v - vals[0]) for v in vals]
    denom = exps[0] + exps[1] + exps[2] + exps[3]
    gates = [e / denom for e in exps]

    onehot = jnp.zeros((N_EXPERTS, rows), F32)
    for i in idxs:
        onehot = jnp.where(sub == i, 1.0, onehot)
    s = lax.broadcasted_iota(jnp.int32, (rows, rows), 0)
    t = lax.broadcasted_iota(jnp.int32, (rows, rows), 1)
    earlier = jnp.where(s < t, 1.0, 0.0).astype(BF16)
    base = _dot(onehot.astype(BF16), earlier) + run_ref[:, 0:1]
    ranks = [jnp.sum(jnp.where(sub == i, base, 0.0), axis=0, keepdims=True) for i in idxs]
    run_ref[...] = run_ref[...] + jnp.sum(onehot, axis=1, keepdims=True)

    route_t = jnp.concatenate(idxs + [r.astype(jnp.int32) for r in ranks], axis=0)
    sub_g = lax.broadcasted_iota(jnp.int32, (LANES, rows), 0)
    gates_t = jnp.zeros((LANES, rows), F32)
    for j, g in enumerate(gates):
        gates_t = jnp.where(sub_g == j, g, gates_t)
    return gates_t.T, route_t


def _mixer_prompt_kernel(x_ref, g_mix, w_in, ln_g, ln_b, ws, bs_t, w_pool, pool_scale, g_a, g_b, w_out,
                         g_ffn, w_router, b_router,
                         x1_ref, hp_ref, meta_ref, route_ref, ptail_ref, cnt_ref,
                         halo_ref, run_ref, a_ref, b_ref):
    tm = x_ref.shape[1]
    j = pl.program_id(1)

    @pl.when((pl.program_id(0) == 0) & (j == 0))
    def _():
        run_ref[...] = jnp.zeros_like(run_ref)

    @pl.when(j == 0)
    def _():
        halo_ref[...] = jnp.zeros_like(halo_ref)

    x = x_ref[0]
    u, vn, p = _mixer_front(x, g_mix, w_in, ln_g, ln_b)
    _spatial_gate(u, vn, ws, bs_t, GMLP_CHUNK, a_ref)

    e = jnp.concatenate([halo_ref[...], p], axis=0)
    halo_ref[...] = p[tm - POOL_MAX:tm]
    ptail_ref[0] = p[tm - POOL_MAX:tm]
    pos = j * tm + lax.broadcasted_iota(jnp.int32, (tm, 1), 0)
    sums = _window_sums(e)
    diffs = []
    for g, w in enumerate(POOL_WINDOWS):
        inv_cnt = 1.0 / jnp.minimum(pos + 1, w).astype(F32)
        c0 = g * POOL_GROUP_DIM
        diffs.append(sums[g][POOL_MAX:] * inv_cnt - p[:, c0:c0 + POOL_GROUP_DIM])
    _pool_project(diffs, w_pool, pool_scale, b_ref)

    x1, hb = _mixer_back(x, a_ref, b_ref, g_a, g_b, w_out, g_ffn)
    x1_ref[0] = x1
    hp_ref[...] = _pack_bf16_pairs(hb)
    meta_ref[...], route_ref[...] = _route(hb, w_router, b_router, run_ref)
    cnt_ref[...] = run_ref[...]


def _slot_layout(run_ref, route_refs, dest_refs, cnt_ref, region_ref, tab_ref):
    cnt = run_ref[...].astype(jnp.int32)
    sub = lax.broadcasted_iota(jnp.int32, cnt.shape, 0)
    blocks = (cnt + (SLOT_BLOCK - 1)) // SLOT_BLOCK
    blk_end = blocks
    shift = 1
    while shift < N_EXPERTS:
        blk_end = blk_end + jnp.where(sub >= shift, pltpu.roll(blk_end, shift, 0), 0)
        shift *= 2
    blk_start = blk_end - blocks
    region = blk_start * SLOT_BLOCK
    rows_last = cnt - (blocks - 1) * SLOT_BLOCK
    cnt_ref[...] = cnt
    region_ref[...] = region

    def of_expert(table, e):
        return table[e:e + 1, 0:1]

    for route_ref, dest_ref in zip(route_refs, dest_refs):
        idx = route_ref[0:TOP_K, :]
        first_slot = jnp.zeros_like(idx)
        for e in range(N_EXPERTS):
            first_slot = jnp.where(idx == e, of_expert(region, e), first_slot)
        dest_ref[...] = first_slot + route_ref[TOP_K:2 * TOP_K, :]

    n_used = of_expert(blk_end, N_EXPERTS - 1)
    blk = lax.broadcasted_iota(jnp.int32, (1, tab_ref.shape[1]), 1)
    blk_c = jnp.minimum(blk, n_used - 1)
    expert = jnp.zeros_like(blk)
    for e in range(N_EXPERTS - 1):
        expert = expert + jnp.where(blk_c >= of_expert(blk_end, e), 1, 0)
    start = jnp.zeros_like(blk)
    end = jnp.zeros_like(blk)
    last_rows = jnp.zeros_like(blk)
    for e in range(N_EXPERTS):
        hit = expert == e
        start = jnp.where(hit, of_expert(blk_start, e), start)
        end = jnp.where(hit, of_expert(blk_end, e), end)
        last_rows = jnp.where(hit, of_expert(rows_last, e), last_rows)
    used = blk < n_used
    first = jnp.where((blk_c == start) & used, 1, 0)
    sub_last = (last_rows + (SUB_BLOCK - 1)) // SUB_BLOCK
    nsub = jnp.where(used, jnp.where(blk_c == end - 1, sub_last, SLOT_BLOCK // SUB_BLOCK), 0)
    n_used_row = jnp.zeros_like(blk) + n_used
    zero = jnp.zeros_like(blk)
    tab_ref[...] = jnp.concatenate([expert, first, nsub, n_used_row, zero, zero, zero, zero], axis=0)


def _mixer_sample_kernel(x_ref, hist_ref, cnt_in_ref, route_p_ref,
                         g_mix, w_in, ln_g, ln_b, ws, bs_t, w_pool, pool_scale,
                         g_a, g_b, w_out, g_ffn, w_router, b_router,
                         x1_ref, hp_ref, meta_ref, route_ref, vn_ref, p_ref,
                         dest_p_ref, dest_s_ref, cnt_ref, region_ref, tab_ref,
                         run_ref, a_ref, b_ref, ext_ref):
    n_streams, hist_rows, _ = hist_ref.shape
    t_new = x_ref.shape[0] // n_streams
    group = hist_rows + t_new
    run_ref[...] = cnt_in_ref[...]

    x = x_ref[...]
    u, vn, p = _mixer_front(x, g_mix, w_in, ln_g, ln_b)
    vn_ref[...] = vn
    p_ref[...] = p
    _spatial_gate(u, vn, ws, bs_t, t_new, a_ref)

    for b in range(n_streams):
        ext_ref[b * group:b * group + hist_rows, :] = hist_ref[b]
        ext_ref[b * group + hist_rows:(b + 1) * group, :] = p[b * t_new:(b + 1) * t_new]
    e = ext_ref[...]
    sums = _window_sums(e)
    diffs = []
    for g, w in enumerate(POOL_WINDOWS):
        c0 = g * POOL_GROUP_DIM
        d = sums[g] * (1.0 / w) - e[:, c0:c0 + POOL_GROUP_DIM]
        diffs.append(jnp.concatenate(
            [d[b * group + hist_rows:(b + 1) * group] for b in range(n_streams)], axis=0))
    _pool_project(diffs, w_pool, pool_scale, b_ref)

    x1, hb = _mixer_back(x, a_ref, b_ref, g_a, g_b, w_out, g_ffn)
    x1_ref[...] = x1
    hp_ref[...] = _pack_bf16_pairs(hb)
    meta_ref[...], route_ref[...] = _route(hb, w_router, b_router, run_ref)
    _slot_layout(run_ref, (route_p_ref, route_ref), (dest_p_ref, dest_s_ref), cnt_ref, region_ref, tab_ref)


def _const_spec(shape):
    zeros = (0,) * len(shape)
    return pl.BlockSpec(shape, lambda *_: zeros, pipeline_mode=pl.Buffered(1))


def _mixer_weight_specs():
    return [
        _const_spec((1, D_MODEL)),
        _const_spec((D_MODEL, 2 * D_A + D_B)),
        _const_spec((1, D_A)), _const_spec((1, D_A)),
        _const_spec((N_HEADS_A, GMLP_CHUNK, GMLP_CHUNK)),
        _const_spec((GMLP_CHUNK, N_HEADS_A)),
        _const_spec((N_POOL_GROUPS, POOL_GROUP_DIM, POOL_GROUP_DIM)),
        _const_spec((1, D_B)),
        _const_spec((1, D_A)), _const_spec((1, D_B)),
        _const_spec((D_A + D_B, D_MODEL)),
        _const_spec((1, D_MODEL)),
        _const_spec((D_MODEL, LANES)),
        _const_spec((1, LANES)),
    ]


def _mixer_prompt(x, weights):
    bsz, seq, _ = x.shape
    tm = TM_PROMPT
    nt = seq // tm
    return pl.pallas_call(
        _mixer_prompt_kernel,
        name="mixer_prompt",
        grid=(bsz, nt),
        in_specs=[pl.BlockSpec((1, tm, D_MODEL), lambda b, j: (b, j, 0))] + _mixer_weight_specs(),
        out_specs=[
            pl.BlockSpec((1, tm, D_MODEL), lambda b, j: (b, j, 0)),
            pl.BlockSpec((tm, D_MODEL // 2), lambda b, j: (b * nt + j, 0)),
            pl.BlockSpec((tm, LANES), lambda b, j: (b * nt + j, 0)),
            pl.BlockSpec((2 * TOP_K, tm), lambda b, j: (0, b * nt + j)),
            pl.BlockSpec((1, POOL_MAX, D_B), lambda b, j: (b, 0, 0)),
            pl.BlockSpec((N_EXPERTS, LANES), lambda b, j: (0, 0)),
        ],
        out_shape=[
            jax.ShapeDtypeStruct((bsz, seq, D_MODEL), F32),
            jax.ShapeDtypeStruct((bsz * seq, D_MODEL // 2), jnp.int32),
            jax.ShapeDtypeStruct((bsz * seq, LANES), F32),
            jax.ShapeDtypeStruct((2 * TOP_K, bsz * seq), jnp.int32),
            jax.ShapeDtypeStruct((bsz, POOL_MAX, D_B), F32),
            jax.ShapeDtypeStruct((N_EXPERTS, LANES), F32),
        ],
        scratch_shapes=[
            pltpu.VMEM((POOL_MAX, D_B), F32),
            pltpu.VMEM((N_EXPERTS, LANES), F32),
            pltpu.VMEM((tm, D_A), F32),
            pltpu.VMEM((tm, D_B), F32),
        ],
        compiler_params=pltpu.CompilerParams(
            dimension_semantics=("arbitrary", "arbitrary"), vmem_limit_bytes=VMEM_LIMIT),
    )(x, *weights)


def _mixer_sample(x2d, hist, cnt_in, route_p, weights, n_blocks):
    rows = x2d.shape[0]
    n_p = route_p.shape[1]
    n_streams, hist_rows, _ = hist.shape
    ext_rows = rows + n_streams * hist_rows
    tab_lanes = -(-n_blocks // LANES) * LANES
    full = lambda shape: pl.BlockSpec(shape, lambda i: (0,) * len(shape))
    i32 = jnp.int32
    out_shapes = [
        ((rows, D_MODEL), F32),
        ((rows, D_MODEL // 2), i32),
        ((rows, LANES), F32),
        ((2 * TOP_K, rows), i32),
        ((rows, D_A), F32),
        ((rows, D_B), F32),
        ((TOP_K, n_p), i32),
        ((TOP_K, rows), i32),
        ((N_EXPERTS, LANES), i32),
        ((N_EXPERTS, LANES), i32),
        ((SUBLANES, tab_lanes), i32),
    ]
    return pl.pallas_call(
        _mixer_sample_kernel,
        name="mixer_sample",
        grid=(1,),
        in_specs=[full((rows, D_MODEL)), full(hist.shape), full((N_EXPERTS, LANES)), full(route_p.shape)]
        + _mixer_weight_specs(),
        out_specs=[full(shape) for shape, _ in out_shapes],
        out_shape=[jax.ShapeDtypeStruct(shape, dtype) for shape, dtype in out_shapes],
        scratch_shapes=[
            pltpu.VMEM((N_EXPERTS, LANES), F32),
            pltpu.VMEM((rows, D_A), F32),
            pltpu.VMEM((rows, D_B), F32),
            pltpu.VMEM((ext_rows, D_B), F32),
        ],
        compiler_params=pltpu.CompilerParams(
            dimension_semantics=("arbitrary",), vmem_limit_bytes=VMEM_LIMIT),
    )(x2d, hist, cnt_in, route_p, *weights)


def _dispatch_kernel(n_prompt, n_sample, dest_ref, cnt_ref, region_ref, nused_ref,
                     hp_p_ref, hp_s_ref, xs_hbm, zbuf, sem, zsem):
    i = pl.program_id(0)
    tm_p = hp_p_ref.shape[0]
    tm_s = hp_s_ref.shape[0]
    n_prompt_tiles = n_prompt // tm_p
    n_blocks = xs_hbm.shape[0] // SLOT_BLOCK

    def pad_rows(e):
        cnt = cnt_ref[e]
        lo = region_ref[e] + cnt
        mid = region_ref[e] + (cnt + SUB_BLOCK - 1) // SUB_BLOCK * SUB_BLOCK
        hi = region_ref[e] + (cnt + SLOT_BLOCK - 1) // SLOT_BLOCK * SLOT_BLOCK
        return lo, mid, (hi - mid) // SUB_BLOCK

    def zero_row_copy(r):
        return pltpu.make_async_copy(zbuf.at[pl.ds(0, 1)], xs_hbm.at[pl.ds(r, 1)], zsem.at[0])

    def zero_sub_block_copy(r):
        return pltpu.make_async_copy(
            zbuf.at[pl.ds(0, SUB_BLOCK)], xs_hbm.at[pl.ds(pl.multiple_of(r, SUB_BLOCK), SUB_BLOCK)], zsem.at[1])

    def zero_block_copy(rb):
        return pltpu.make_async_copy(zbuf, xs_hbm.at[pl.ds(rb * SLOT_BLOCK, SLOT_BLOCK)], zsem.at[2])

    def for_each_fill(row_fn, sub_block_fn, block_fn):
        def per_expert(e, carry):
            lo, mid, n_sub = pad_rows(e)
            lax.fori_loop(lo, mid, lambda r, c: (row_fn(r), c)[1], 0)
            lax.fori_loop(0, n_sub, lambda j, c: (sub_block_fn(mid + j * SUB_BLOCK), c)[1], 0)
            return carry
        lax.fori_loop(0, N_EXPERTS, per_expert, 0)
        lax.fori_loop(nused_ref[0], n_blocks, lambda rb, c: (block_fn(rb), c)[1], 0)

    @pl.when(i == 0)
    def _():
        zbuf[...] = jnp.zeros_like(zbuf)
        for_each_fill(lambda r: zero_row_copy(r).start(), lambda r: zero_sub_block_copy(r).start(),
                      lambda rb: zero_block_copy(rb).start())

    def scatter(src_ref, first_index, k_stride):
        rows = src_ref.shape[0]

        def body(t, carry):
            for k in range(TOP_K):
                d = dest_ref[first_index + k * k_stride + t]
                pltpu.make_async_copy(
                    src_ref.at[pl.ds(t, 1)], xs_hbm.at[pl.ds(d, 1)], sem).start(priority=k % 2)
            return carry
        lax.fori_loop(0, rows, body, 0, unroll=8)
        for _ in range(TOP_K):
            pltpu.make_async_copy(src_ref, xs_hbm.at[pl.ds(0, rows)], sem).wait()

    @pl.when(i < n_prompt_tiles)
    def _():
        scatter(hp_p_ref, i * tm_p, n_prompt)

    @pl.when(i >= n_prompt_tiles)
    def _():
        scatter(hp_s_ref, TOP_K * n_prompt + (i - n_prompt_tiles) * tm_s, n_sample)

    @pl.when(i == 0)
    def _():
        for_each_fill(lambda r: zero_row_copy(r).wait(), lambda r: zero_sub_block_copy(r).wait(),
                      lambda rb: zero_block_copy(rb).wait())


def _dispatch(dest, counts, region, n_used, hp_p, hp_s, n_blocks):
    tm_p = min(TM_DISPATCH, hp_p.shape[0])
    tm_s = min(TM_DISPATCH, hp_s.shape[0])
    assert hp_p.shape[0] % tm_p == 0 and hp_s.shape[0] % tm_s == 0
    ntp = hp_p.shape[0] // tm_p
    nts = hp_s.shape[0] // tm_s
    half = D_MODEL // 2
    return pl.pallas_call(
        functools.partial(_dispatch_kernel, hp_p.shape[0], hp_s.shape[0]),
        name="dispatch",
        grid_spec=pltpu.PrefetchScalarGridSpec(
            num_scalar_prefetch=4,
            grid=(ntp + nts,),
            in_specs=[
                pl.BlockSpec((tm_p, half), lambda i, *_: (jnp.minimum(i, ntp - 1), 0)),
                pl.BlockSpec((tm_s, half), lambda i, *_: (jnp.maximum(i - ntp, 0), 0)),
            ],
            out_specs=pl.BlockSpec(memory_space=pl.ANY),
            scratch_shapes=[
                pltpu.VMEM((SLOT_BLOCK, half), jnp.int32),
                pltpu.SemaphoreType.DMA,
                pltpu.SemaphoreType.DMA((3,)),
            ],
        ),
        out_shape=jax.ShapeDtypeStruct((n_blocks * SLOT_BLOCK, half), jnp.int32),
        compiler_params=pltpu.CompilerParams(dimension_semantics=("arbitrary",)),
    )(dest, counts, region, n_used, hp_p, hp_s)


def _unpack_bf16_pairs(w):
    lo = lax.bitcast_convert_type(lax.shift_left(w, 16), F32).astype(BF16)
    hi = lax.bitcast_convert_type(w & jnp.int32(-65536), F32).astype(BF16)
    return jnp.concatenate([lo, hi], axis=1)


def _for_each_sub_block(nsub, compute, out_ref):
    n_sub_max = SLOT_BLOCK // SUB_BLOCK

    @pl.when(nsub == n_sub_max)
    def _():
        compute(pl.ds(0, SLOT_BLOCK))

    for s in range(n_sub_max):
        rows = pl.ds(s * SUB_BLOCK, SUB_BLOCK)

        @pl.when((s < nsub) & (nsub < n_sub_max))
        def _(rows=rows):
            compute(rows)

        @pl.when(s >= nsub)
        def _(rows=rows):
            out_ref[rows] = jnp.zeros((SUB_BLOCK,) + tuple(out_ref.shape[1:]), out_ref.dtype)


def _grouped_call(name, step, grid, in_specs_fn, out_spec, operands, residents, out_shape, scratch_shapes):
    n_in = len(operands)
    n_res = len(residents)

    def outer(be_ref, first_ref, nsub_ref, *refs):
        ins, res_hbm, out = refs[:n_in], refs[n_in:n_in + n_res], refs[n_in + n_res]
        scratches = refs[n_in + n_res + 1:]
        res_vmem = scratches[len(scratches) - n_res:]
        for src, dst in zip(res_hbm, res_vmem):
            pltpu.sync_copy(src, dst)

        def body(*block_refs):
            step(be_ref, first_ref, nsub_ref, *block_refs)

        pltpu.emit_pipeline(
            body, grid=grid, in_specs=in_specs_fn(be_ref), out_specs=[out_spec],
        )(*ins, out, scratches=scratches)

    any_spec = pl.BlockSpec(memory_space=pl.ANY)
    return lambda block_expert, first, nsub: pl.pallas_call(
        outer,
        name=name,
        grid_spec=pltpu.PrefetchScalarGridSpec(
            num_scalar_prefetch=3,
            grid=(),
            in_specs=[any_spec] * (n_in + n_res),
            out_specs=any_spec,
            scratch_shapes=list(scratch_shapes) + [pltpu.VMEM(r.shape, r.dtype) for r in residents],
        ),
        out_shape=out_shape,
        compiler_params=pltpu.CompilerParams(vmem_limit_bytes=VMEM_LIMIT),
    )(block_expert, first, nsub, *operands, *residents)


def _expert_weight_spec(be_ref, shape):
    return pl.BlockSpec(shape, lambda j, rb: (be_ref[rb], 0, j),
                        pipeline_mode=pl.Buffered(2, use_lookahead=True))


def _ffn_up_step(be_ref, first_ref, nsub_ref, x_ref, wg_ref, wu_ref, act_ref, wgb, wub, bg_ref, bu_ref):
    rb = pl.program_id(1)
    expert = pl.ds(be_ref[rb], 1)
    cols = pl.ds(pl.multiple_of(pl.program_id(0) * FF_TILE, FF_TILE), FF_TILE)

    @pl.when(first_ref[rb] == 1)
    def _():
        wgb[...] = wg_ref[0].astype(BF16)
        wub[...] = wu_ref[0].astype(BF16)

    def compute(rows):
        x = _unpack_bf16_pairs(x_ref[rows, :])
        gate = jnp.minimum(_dot(x, wgb[...]) + bg_ref[expert, cols], SWIGLU_LIMIT)
        up = jnp.clip(_dot(x, wub[...]) + bu_ref[expert, cols], -SWIGLU_LIMIT, SWIGLU_LIMIT)
        glu = gate * jax.nn.sigmoid(SWIGLU_ALPHA * gate)
        act_ref[rows, :] = ((up + 1.0) * glu).astype(BF16)

    _for_each_sub_block(nsub_ref[rb], compute, act_ref)


def _ffn_up(block_expert, first, nsub, xs, w_gate, b_gate, w_up, b_up, n_blocks):
    def in_specs(be_ref):
        w_spec = _expert_weight_spec(be_ref, (1, D_MODEL, FF_TILE))
        return [pl.BlockSpec((SLOT_BLOCK, D_MODEL // 2), lambda f, rb: (rb, 0)), w_spec, w_spec]

    return _grouped_call(
        "ffn_up", _ffn_up_step, (D_FF // FF_TILE, n_blocks), in_specs,
        pl.BlockSpec((SLOT_BLOCK, FF_TILE), lambda f, rb: (rb, f)),
        (xs, w_gate, w_up), (b_gate, b_up),
        jax.ShapeDtypeStruct((n_blocks * SLOT_BLOCK, D_FF), BF16),
        [pltpu.VMEM((D_MODEL, FF_TILE), BF16), pltpu.VMEM((D_MODEL, FF_TILE), BF16)],
    )(block_expert, first, nsub)


def _ffn_down_step(be_ref, first_ref, nsub_ref, act_ref, wd_ref, y_ref, wdb, bd_ref):
    rb = pl.program_id(1)
    expert = pl.ds(be_ref[rb], 1)

    @pl.when(first_ref[rb] == 1)
    def _():
        wdb[...] = wd_ref[0].astype(BF16)

    def compute(rows):
        y = _dot(act_ref[rows, :], wdb[...]) + bd_ref[expert, :]
        y_ref[rows, :, :] = _pack_bf16_pairs(y.astype(BF16)).reshape(-1, SUBLANES, LANES)

    _for_each_sub_block(nsub_ref[rb], compute, y_ref)


def _ffn_down(block_expert, first, nsub, act, w_down, b_down, n_blocks):
    def in_specs(be_ref):
        return [
            pl.BlockSpec((SLOT_BLOCK, D_FF), lambda n, rb: (rb, 0)),
            _expert_weight_spec(be_ref, (1, D_FF, D_MODEL)),
        ]

    return _grouped_call(
        "ffn_down", _ffn_down_step, (1, n_blocks), in_specs,
        pl.BlockSpec((SLOT_BLOCK, SUBLANES, LANES), lambda n, rb: (rb, 0, 0)),
        (act, w_down), (b_down,),
        jax.ShapeDtypeStruct((n_blocks * SLOT_BLOCK, SUBLANES, LANES), jnp.int32),
        [pltpu.VMEM((D_FF, D_MODEL), BF16)],
    )(block_expert, first, nsub)


def _combine_kernel(n_tokens, dest_ref, x1_ref, meta_ref, g_ref, ys_hbm, out_ref, ybuf0, ybuf1, sem0, sem1):
    i = pl.program_id(0)
    n_steps = pl.num_programs(0)
    tm = TM_COMBINE
    half = D_MODEL // 2

    def gather(tile, ybuf, sem):
        def body(grp, carry):
            for j in range(SUBLANES):
                token = tile * tm + grp * SUBLANES + j
                for k in range(TOP_K):
                    d = dest_ref[k * n_tokens + token]
                    pltpu.make_async_copy(
                        ys_hbm.at[pl.ds(d, 1)],
                        ybuf.at[pl.ds(k * tm + grp * SUBLANES + j, 1)], sem).start(priority=k % 2)
            return carry
        lax.fori_loop(0, tm // SUBLANES, body, 0)

    def finish(r0, ybuf, sem):
        pltpu.make_async_copy(ys_hbm.at[pl.ds(0, TOP_K * tm)], ybuf, sem).wait()
        lo = x1_ref[r0:r0 + tm, 0:half]
        hi = x1_ref[r0:r0 + tm, half:D_MODEL]
        for k in range(TOP_K):
            w = ybuf[k * tm:(k + 1) * tm].reshape(tm, half)
            gate = meta_ref[r0:r0 + tm, k:k + 1]
            lo = lo + lax.bitcast_convert_type(lax.shift_left(w, 16), F32) * gate
            hi = hi + lax.bitcast_convert_type(w & jnp.int32(-65536), F32) * gate
        ms = (jnp.sum(lo * lo, axis=-1, keepdims=True) + jnp.sum(hi * hi, axis=-1, keepdims=True)) / D_MODEL
        scale = lax.rsqrt(ms + EPS)
        out_ref[r0:r0 + tm, 0:half] = lo * scale * g_ref[:, 0:half]
        out_ref[r0:r0 + tm, half:D_MODEL] = hi * scale * g_ref[:, half:D_MODEL]

    @pl.when(i == 0)
    def _():
        gather(0, ybuf0, sem0)

    gather(2 * i + 1, ybuf1, sem1)
    finish(0, ybuf0, sem0)

    @pl.when(i + 1 < n_steps)
    def _():
        gather(2 * i + 2, ybuf0, sem0)

    finish(tm, ybuf1, sem1)


def _combine(dest_flat, x1, meta, g_final, ys):
    n = x1.shape[0]
    tm = 2 * TM_COMBINE
    assert n % tm == 0
    ybuf = pltpu.VMEM((TOP_K * TM_COMBINE, SUBLANES, LANES), jnp.int32)
    return pl.pallas_call(
        functools.partial(_combine_kernel, n),
        name="combine",
        grid_spec=pltpu.PrefetchScalarGridSpec(
            num_scalar_prefetch=1,
            grid=(n // tm,),
            in_specs=[
                pl.BlockSpec((tm, D_MODEL), lambda i, d: (i, 0)),
                pl.BlockSpec((tm, LANES), lambda i, d: (i, 0)),
                pl.BlockSpec((1, D_MODEL), lambda i, d: (0, 0)),
                pl.BlockSpec(memory_space=pl.ANY),
            ],
            out_specs=pl.BlockSpec((tm, D_MODEL), lambda i, d: (i, 0)),
            scratch_shapes=[ybuf, ybuf, pltpu.SemaphoreType.DMA, pltpu.SemaphoreType.DMA],
        ),
        out_shape=jax.ShapeDtypeStruct((n, D_MODEL), F32),
        compiler_params=pltpu.CompilerParams(
            dimension_semantics=("arbitrary",), vmem_limit_bytes=VMEM_LIMIT),
    )(dest_flat, x1, meta, g_final, ys)


def kernel(x_prompt, x_sample, cache_pool, norm_mix_g, w_in, ln_v_g, ln_v_b, w_spatial, b_spatial, w_pool,
           pool_scale, out_norm_a_g, out_norm_b_g, w_out, norm_ffn_g, w_router, b_router, w_gate, b_gate,
           w_up, b_up, w_down, b_down, final_norm_g):
    depth = norm_mix_g.shape[0]
    assert depth == 1
    bsz, seq, _ = x_prompt.shape
    dec_b, dec_t, _ = x_sample.shape
    assert seq % TM_PROMPT == 0 and TM_PROMPT % GMLP_CHUNK == 0
    assert GMLP_CHUNK % dec_t == 0 and (dec_b * dec_t) % GMLP_CHUNK == 0 and PAST_LEN % GMLP_CHUNK == 0
    assert dec_t >= POOL_STATE and PAST_LEN + 1 >= POOL_MAX
    n_p = bsz * seq
    n_s = dec_b * dec_t
    n = n_p + n_s
    assert n_p % TM_COMBINE == 0 and n_s % TM_COMBINE == 0
    l = 0

    row = lambda v: v.reshape(1, -1)
    shared = [
        row(norm_mix_g[l]), w_in[l].astype(BF16), row(ln_v_g[l]), row(ln_v_b[l]),
    ]
    tail = [
        w_pool[l].astype(BF16), row(pool_scale[l]), row(out_norm_a_g[l]), row(out_norm_b_g[l]),
        w_out[l].astype(BF16), row(norm_ffn_g[l]),
        jnp.pad(w_router[l].astype(BF16), ((0, 0), (0, LANES - N_EXPERTS))),
        jnp.pad(row(b_router[l]), ((0, 0), (0, LANES - N_EXPERTS))),
    ]
    weights_p = shared + [w_spatial[l], b_spatial[l].T] + tail
    reps = GMLP_CHUNK // dec_t
    weights_s = shared + [
        jnp.tile(w_spatial[l][:, :dec_t, :dec_t], (1, reps, reps)),
        jnp.tile(b_spatial[l][:, :dec_t].T, (reps, 1)),
    ] + tail

    x1_p, hp_p, meta_p, route_p, ptail, cnt_p = _mixer_prompt(x_prompt, weights_p)
    hist = jnp.pad(cache_pool[l], ((0, 0), (POOL_MAX - POOL_STATE, 0), (0, 0)))
    n_blocks = -(-(n * TOP_K + N_EXPERTS * (SLOT_BLOCK - 1)) // SLOT_BLOCK)
    x1_s, hp_s, meta_s, _, vn_s, p_s, dest_p, dest_s, cnt, region, tables = _mixer_sample(
        x_sample.reshape(n_s, D_MODEL), hist, cnt_p, route_p, weights_s, n_blocks)

    dest_p = dest_p.reshape(-1)
    dest_s = dest_s.reshape(-1)
    block_expert = tables[0, :n_blocks]
    first = tables[1, :n_blocks]
    nsub = tables[2, :n_blocks]
    n_used = tables[3, :1]

    xs = _dispatch(jnp.concatenate([dest_p, dest_s]), cnt[:, 0], region[:, 0], n_used, hp_p, hp_s, n_blocks)
    act = _ffn_up(block_expert, first, nsub, xs, w_gate[l], b_gate[l], w_up[l], b_up[l], n_blocks)
    ys = _ffn_down(block_expert, first, nsub, act, w_down[l], b_down[l], n_blocks)

    g_final = row(final_norm_g)
    y_p = _combine(dest_p, x1_p.reshape(n_p, D_MODEL), meta_p, g_final, ys)
    y_s = _combine(dest_s, x1_s, meta_s, g_final, ys)

    y_prompt = y_p.reshape(bsz, seq, D_MODEL)
    y_sample = y_s.reshape(dec_b, dec_t, D_MODEL)
    state_pool_prompt = ptail[:, POOL_MAX - POOL_STATE:][None]
    p_s3 = p_s.reshape(dec_b, dec_t, D_B)
    state_pool_sample = p_s3[:, dec_t - POOL_STATE:][None]
    state_chunk_v_sample = vn_s.reshape(dec_b, dec_t, D_A)[None]
    return (y_prompt, y_sample, state_pool_prompt, state_pool_sample, state_chunk_v_sample)
```

```python
import functools

import jax
import jax.numpy as jnp
from jax import lax
from jax.experimental import pallas as pl
from jax.experimental.pallas import tpu as pltpu

D_MODEL = 2048
D_A = 1024
D_B = 1024
GMLP_CHUNK = 128
N_HEADS_A = 8
HEAD_DIM_A = D_A // N_HEADS_A
POOL_WINDOWS = (2, 4, 8, 16)
N_POOL_GROUPS = len(POOL_WINDOWS)
POOL_GROUP_DIM = D_B // N_POOL_GROUPS
POOL_MAX = 16
POOL_STATE = POOL_MAX - 1
PAST_LEN = 1024
N_EXPERTS = 32
TOP_K = 4
D_FF = 2048
SWIGLU_LIMIT = 7.0
SWIGLU_ALPHA = 1.702
EPS = 1e-5

LANES = 128
SUBLANES = 8
TM_PROMPT = 256
SLOT_BLOCK = 512
SUB_BLOCK = 128
TM_DISPATCH = 2048
FF_TILE = 1024
TM_COMBINE = 128
VMEM_LIMIT = 56 * 1024 * 1024

BF16 = jnp.bfloat16
F32 = jnp.float32


def _dot(a, b):
    return jnp.dot(a, b, preferred_element_type=F32)


def _rms_norm(x, g):
    return x * lax.rsqrt(jnp.mean(x * x, axis=-1, keepdims=True) + EPS) * g


def _layer_norm(x, g, b):
    mu = jnp.mean(x, axis=-1, keepdims=True)
    xc = x - mu
    return xc * lax.rsqrt(jnp.mean(xc * xc, axis=-1, keepdims=True) + EPS) * g + b


def _gelu(x):
    return 0.5 * x * (1.0 + lax.erf(x * (2.0 ** -0.5)))


def _mixer_front(x, g_mix, w_in, ln_g, ln_b):
    h = _rms_norm(x, g_mix[...]).astype(BF16)
    u = _gelu(_dot(h, w_in[:, 0:D_A]))
    vn = _layer_norm(_dot(h, w_in[:, D_A:2 * D_A]), ln_g[...], ln_b[...])
    p = _dot(h, w_in[:, 2 * D_A:2 * D_A + D_B])
    return u, vn, p


def _spatial_gate(u, vn, ws, bs_t, chunk, a_ref):
    rows = u.shape[0]
    t = lax.broadcasted_iota(jnp.int32, (GMLP_CHUNK, GMLP_CHUNK), 0)
    s = lax.broadcasted_iota(jnp.int32, (GMLP_CHUNK, GMLP_CHUNK), 1)
    mask = (s <= t) & ((t // chunk) == (s // chunk))
    vb = vn.astype(BF16)
    for h in range(N_HEADS_A):
        w = jnp.where(mask, ws[h], 0.0).astype(BF16)
        bias = bs_t[:, h:h + 1]
        c0 = h * HEAD_DIM_A
        for r0 in range(0, rows, GMLP_CHUNK):
            mixed = _dot(w, vb[r0:r0 + GMLP_CHUNK, c0:c0 + HEAD_DIM_A]) + bias
            a_ref[r0:r0 + GMLP_CHUNK, c0:c0 + HEAD_DIM_A] = u[r0:r0 + GMLP_CHUNK, c0:c0 + HEAD_DIM_A] * mixed


def _window_sums(e):
    parts = []
    s = e
    for g, w in enumerate(POOL_WINDOWS):
        s = s + pltpu.roll(s, w // 2, 0)
        parts.append(s[:, 0:POOL_GROUP_DIM])
        if g + 1 < N_POOL_GROUPS:
            s = s[:, POOL_GROUP_DIM:]
    return parts


def _pool_project(diffs, w_pool, pool_scale, b_ref):
    for g in range(N_POOL_GROUPS):
        c0 = g * POOL_GROUP_DIM
        out = _dot(diffs[g].astype(BF16), w_pool[g])
        b_ref[:, c0:c0 + POOL_GROUP_DIM] = out * pool_scale[:, c0:c0 + POOL_GROUP_DIM]


def _mixer_back(x, a_ref, b_ref, g_a, g_b, w_out, g_ffn):
    na = _rms_norm(a_ref[...], g_a[...]).astype(BF16)
    nb = _rms_norm(b_ref[...], g_b[...]).astype(BF16)
    x1 = x + _dot(na, w_out[0:D_A, :]) + _dot(nb, w_out[D_A:D_A + D_B, :])
    hb = _rms_norm(x1, g_ffn[...]).astype(BF16)
    return x1, hb


def _pack_bf16_pairs(hb):
    half = D_MODEL // 2
    lo = lax.bitcast_convert_type(hb[:, 0:half].astype(F32), jnp.int32)
    hi = lax.bitcast_convert_type(hb[:, half:D_MODEL].astype(F32), jnp.int32)
    return lax.shift_right_logical(lo, 16) | hi


def _route(hb, w_router, b_router, run_ref):
    rows = hb.shape[0]
    logits = _dot(hb, w_router[...]) + b_router[...]
    l = logits.T[0:N_EXPERTS]
    sub = lax.broadcasted_iota(jnp.int32, (N_EXPERTS, rows), 0)
    vals, idxs = [], []
    for _ in range(TOP_K):
        m = jnp.max(l, axis=0, keepdims=True)
        i = jnp.min(jnp.where(l == m, sub, N_EXPERTS), axis=0, keepdims=True)
        vals.append(m)
        idxs.append(i)
        l = jnp.where(sub == i, -jnp.inf, l)
    exps = [jnp.exp(v - vals[0]) for v in vals]
    denom = exps[0] + exps[1] + exps[2] + exps[3]
    gates = [e / denom for e in exps]

    onehot = jnp.zeros((N_EXPERTS, rows), F32)
    for i in idxs:
        onehot = jnp.where(sub == i, 1.0, onehot)
    s = lax.broadcasted_iota(jnp.int32, (rows, rows), 0)
    t = lax.broadcasted_iota(jnp.int32, (rows, rows), 1)
    earlier = jnp.where(s < t, 1.0, 0.0).astype(BF16)
    base = _dot(onehot.astype(BF16), earlier) + run_ref[:, 0:1]
    ranks = [jnp.sum(jnp.where(sub == i, base, 0.0), axis=0, keepdims=True) for i in idxs]
    run_ref[...] = run_ref[...] + jnp.sum(onehot, axis=1, keepdims=True)

    route_t = jnp.concatenate(idxs + [r.astype(jnp.int32) for r in ranks], axis=0)
    sub_g = lax.broadcasted_iota(jnp.int32, (LANES, rows), 0)
    gates_t = jnp.zeros((LANES, rows), F32)
    for j, g in enumerate(gates):
        gates_t = jnp.where(sub_g == j, g, gates_t)
    return gates_t.T, route_t


def _mixer_prompt_kernel(x_ref, g_mix, w_in, ln_g, ln_b, ws, bs_t, w_pool, pool_scale, g_a, g_b, w_out,
                         g_ffn, w_router, b_router,
                         x1_ref, hp_ref, meta_ref, route_ref, ptail_ref, cnt_ref,
                         halo_ref, run_ref, a_ref, b_ref):
    tm = x_ref.shape[1]
    j = pl.program_id(1)

    @pl.when((pl.program_id(0) == 0) & (j == 0))
    def _():
        run_ref[...] = jnp.zeros_like(run_ref)

    @pl.when(j == 0)
    def _():
        halo_ref[...] = jnp.zeros_like(halo_ref)

    x = x_ref[0]
    u, vn, p = _mixer_front(x, g_mix, w_in, ln_g, ln_b)
    _spatial_gate(u, vn, ws, bs_t, GMLP_CHUNK, a_ref)

    e = jnp.concatenate([halo_ref[...], p], axis=0)
    halo_ref[...] = p[tm - POOL_MAX:tm]
    ptail_ref[0] = p[tm - POOL_MAX:tm]
    pos = j * tm + lax.broadcasted_iota(jnp.int32, (tm, 1), 0)
    sums = _window_sums(e)
    diffs = []
    for g, w in enumerate(POOL_WINDOWS):
        inv_cnt = 1.0 / jnp.minimum(pos + 1, w).astype(F32)
        c0 = g * POOL_GROUP_DIM
        diffs.append(sums[g][POOL_MAX:] * inv_cnt - p[:, c0:c0 + POOL_GROUP_DIM])
    _pool_project(diffs, w_pool, pool_scale, b_ref)

    x1, hb = _mixer_back(x, a_ref, b_ref, g_a, g_b, w_out, g_ffn)
    x1_ref[0] = x1
    hp_ref[...] = _pack_bf16_pairs(hb)
    meta_ref[...], route_ref[...] = _route(hb, w_router, b_router, run_ref)
    cnt_ref[...] = run_ref[...]


def _slot_layout(run_ref, route_refs, dest_refs, cnt_ref, region_ref, tab_ref):
    cnt = run_ref[...].astype(jnp.int32)
    sub = lax.broadcasted_iota(jnp.int32, cnt.shape, 0)
    blocks = (cnt + (SLOT_BLOCK - 1)) // SLOT_BLOCK
    blk_end = blocks
    shift = 1
    while shift < N_EXPERTS:
        blk_end = blk_end + jnp.where(sub >= shift, pltpu.roll(blk_end, shift, 0), 0)
        shift *= 2
    blk_start = blk_end - blocks
    region = blk_start * SLOT_BLOCK
    rows_last = cnt - (blocks - 1) * SLOT_BLOCK
    cnt_ref[...] = cnt
    region_ref[...] = region

    def of_expert(table, e):
        return table[e:e + 1, 0:1]

    for route_ref, dest_ref in zip(route_refs, dest_refs):
        idx = route_ref[0:TOP_K, :]
        first_slot = jnp.zeros_like(idx)
        for e in range(N_EXPERTS):
            first_slot = jnp.where(idx == e, of_expert(region, e), first_slot)
        dest_ref[...] = first_slot + route_ref[TOP_K:2 * TOP_K, :]

    n_used = of_expert(blk_end, N_EXPERTS - 1)
    blk = lax.broadcasted_iota(jnp.int32, (1, tab_ref.shape[1]), 1)
    blk_c = jnp.minimum(blk, n_used - 1)
    expert = jnp.zeros_like(blk)
    for e in range(N_EXPERTS - 1):
        expert = expert + jnp.where(blk_c >= of_expert(blk_end, e), 1, 0)
    start = jnp.zeros_like(blk)
    end = jnp.zeros_like(blk)
    last_rows = jnp.zeros_like(blk)
    for e in range(N_EXPERTS):
        hit = expert == e
        start = jnp.where(hit, of_expert(blk_start, e), start)
        end = jnp.where(hit, of_expert(blk_end, e), end)
        last_rows = jnp.where(hit, of_expert(rows_last, e), last_rows)
    used = blk < n_used
    first = jnp.where((blk_c == start) & used, 1, 0)
    sub_last = (last_rows + (SUB_BLOCK - 1)) // SUB_BLOCK
    nsub = jnp.where(used, jnp.where(blk_c == end - 1, sub_last, SLOT_BLOCK // SUB_BLOCK), 0)
    n_used_row = jnp.zeros_like(blk) + n_used
    zero = jnp.zeros_like(blk)
    tab_ref[...] = jnp.concatenate([expert, first, nsub, n_used_row, zero, zero, zero, zero], axis=0)


def _mixer_sample_kernel(x_ref, hist_ref, cnt_in_ref, route_p_ref,
                         g_mix, w_in, ln_g, ln_b, ws, bs_t, w_pool, pool_scale,
                         g_a, g_b, w_out, g_ffn, w_router, b_router,
                         x1_ref, hp_ref, meta_ref, route_ref, vn_ref, p_ref,
                         dest_p_ref, dest_s_ref, cnt_ref, region_ref, tab_ref,
                         run_ref, a_ref, b_ref, ext_ref):
    n_streams, hist_rows, _ = hist_ref.shape
    t_new = x_ref.shape[0] // n_streams
    group = hist_rows + t_new
    run_ref[...] = cnt_in_ref[...]

    x = x_ref[...]
    u, vn, p = _mixer_front(x, g_mix, w_in, ln_g, ln_b)
    vn_ref[...] = vn
    p_ref[...] = p
    _spatial_gate(u, vn, ws, bs_t, t_new, a_ref)

    for b in range(n_streams):
        ext_ref[b * group:b * group + hist_rows, :] = hist_ref[b]
        ext_ref[b * group + hist_rows:(b + 1) * group, :] = p[b * t_new:(b + 1) * t_new]
    e = ext_ref[...]
    sums = _window_sums(e)
    diffs = []
    for g, w in enumerate(POOL_WINDOWS):
        c0 = g * POOL_GROUP_DIM
        d = sums[g] * (1.0 / w) - e[:, c0:c0 + POOL_GROUP_DIM]
        diffs.append(jnp.concatenate(
            [d[b * group + hist_rows:(b + 1) * group] for b in range(n_streams)], axis=0))
    _pool_project(diffs, w_pool, pool_scale, b_ref)

    x1, hb = _mixer_back(x, a_ref, b_ref, g_a, g_b, w_out, g_ffn)
    x1_ref[...] = x1
    hp_ref[...] = _pack_bf16_pairs(hb)
    meta_ref[...], route_ref[...] = _route(hb, w_router, b_router, run_ref)
    _slot_layout(run_ref, (route_p_ref, route_ref), (dest_p_ref, dest_s_ref), cnt_ref, region_ref, tab_ref)


def _const_spec(shape):
    zeros = (0,) * len(shape)
    return pl.BlockSpec(shape, lambda *_: zeros, pipeline_mode=pl.Buffered(1))


def _mixer_weight_specs():
    return [
        _const_spec((1, D_MODEL)),
        _const_spec((D_MODEL, 2 * D_A + D_B)),
        _const_spec((1, D_A)), _const_spec((1, D_A)),
        _const_spec((N_HEADS_A, GMLP_CHUNK, GMLP_CHUNK)),
        _const_spec((GMLP_CHUNK, N_HEADS_A)),
        _const_spec((N_POOL_GROUPS, POOL_GROUP_DIM, POOL_GROUP_DIM)),
        _const_spec((1, D_B)),
        _const_spec((1, D_A)), _const_spec((1, D_B)),
        _const_spec((D_A + D_B, D_MODEL)),
        _const_spec((1, D_MODEL)),
        _const_spec((D_MODEL, LANES)),
        _const_spec((1, LANES)),
    ]


def _mixer_prompt(x, weights):
    bsz, seq, _ = x.shape
    tm = TM_PROMPT
    nt = seq // tm
    return pl.pallas_call(
        _mixer_prompt_kernel,
        name="mixer_prompt",
        grid=(bsz, nt),
        in_specs=[pl.BlockSpec((1, tm, D_MODEL), lambda b, j: (b, j, 0))] + _mixer_weight_specs(),
        out_specs=[
            pl.BlockSpec((1, tm, D_MODEL), lambda b, j: (b, j, 0)),
            pl.BlockSpec((tm, D_MODEL // 2), lambda b, j: (b * nt + j, 0)),
            pl.BlockSpec((tm, LANES), lambda b, j: (b * nt + j, 0)),
            pl.BlockSpec((2 * TOP_K, tm), lambda b, j: (0, b * nt + j)),
            pl.BlockSpec((1, POOL_MAX, D_B), lambda b, j: (b, 0, 0)),
            pl.BlockSpec((N_EXPERTS, LANES), lambda b, j: (0, 0)),
        ],
        out_shape=[
            jax.ShapeDtypeStruct((bsz, seq, D_MODEL), F32),
            jax.ShapeDtypeStruct((bsz * seq, D_MODEL // 2), jnp.int32),
            jax.ShapeDtypeStruct((bsz * seq, LANES), F32),
            jax.ShapeDtypeStruct((2 * TOP_K, bsz * seq), jnp.int32),
            jax.ShapeDtypeStruct((bsz, POOL_MAX, D_B), F32),
            jax.ShapeDtypeStruct((N_EXPERTS, LANES), F32),
        ],
        scratch_shapes=[
            pltpu.VMEM((POOL_MAX, D_B), F32),
            pltpu.VMEM((N_EXPERTS, LANES), F32),
            pltpu.VMEM((tm, D_A), F32),
            pltpu.VMEM((tm, D_B), F32),
        ],
        compiler_params=pltpu.CompilerParams(
            dimension_semantics=("arbitrary", "arbitrary"), vmem_limit_bytes=VMEM_LIMIT),
    )(x, *weights)


def _mixer_sample(x2d, hist, cnt_in, route_p, weights, n_blocks):
    rows = x2d.shape[0]
    n_p = route_p.shape[1]
    n_streams, hist_rows, _ = hist.shape
    ext_rows = rows + n_streams * hist_rows
    tab_lanes = -(-n_blocks // LANES) * LANES
    full = lambda shape: pl.BlockSpec(shape, lambda i: (0,) * len(shape))
    i32 = jnp.int32
    out_shapes = [
        ((rows, D_MODEL), F32),
        ((rows, D_MODEL // 2), i32),
        ((rows, LANES), F32),
        ((2 * TOP_K, rows), i32),
        ((rows, D_A), F32),
        ((rows, D_B), F32),
        ((TOP_K, n_p), i32),
        ((TOP_K, rows), i32),
        ((N_EXPERTS, LANES), i32),
        ((N_EXPERTS, LANES), i32),
        ((SUBLANES, tab_lanes), i32),
    ]
    return pl.pallas_call(
        _mixer_sample_kernel,
        name="mixer_sample",
        grid=(1,),
        in_specs=[full((rows, D_MODEL)), full(hist.shape), full((N_EXPERTS, LANES)), full(route_p.shape)]
        + _mixer_weight_specs(),
        out_specs=[full(shape) for shape, _ in out_shapes],
        out_shape=[jax.ShapeDtypeStruct(shape, dtype) for shape, dtype in out_shapes],
        scratch_shapes=[
            pltpu.VMEM((N_EXPERTS, LANES), F32),
            pltpu.VMEM((rows, D_A), F32),
            pltpu.VMEM((rows, D_B), F32),
            pltpu.VMEM((ext_rows, D_B), F32),
        ],
        compiler_params=pltpu.CompilerParams(
            dimension_semantics=("arbitrary",), vmem_limit_bytes=VMEM_LIMIT),
    )(x2d, hist, cnt_in, route_p, *weights)


def _dispatch_kernel(n_prompt, n_sample, dest_ref, cnt_ref, region_ref, nused_ref,
                     hp_p_ref, hp_s_ref, xs_hbm, zbuf, sem, zsem):
    i = pl.program_id(0)
    tm_p = hp_p_ref.shape[0]
    tm_s = hp_s_ref.shape[0]
    n_prompt_tiles = n_prompt // tm_p
    n_blocks = xs_hbm.shape[0] // SLOT_BLOCK

    def pad_rows(e):
        cnt = cnt_ref[e]
        lo = region_ref[e] + cnt
        mid = region_ref[e] + (cnt + SUB_BLOCK - 1) // SUB_BLOCK * SUB_BLOCK
        hi = region_ref[e] + (cnt + SLOT_BLOCK - 1) // SLOT_BLOCK * SLOT_BLOCK
        return lo, mid, (hi - mid) // SUB_BLOCK

    def zero_row_copy(r):
        return pltpu.make_async_copy(zbuf.at[pl.ds(0, 1)], xs_hbm.at[pl.ds(r, 1)], zsem.at[0])

    def zero_sub_block_copy(r):
        return pltpu.make_async_copy(
            zbuf.at[pl.ds(0, SUB_BLOCK)], xs_hbm.at[pl.ds(pl.multiple_of(r, SUB_BLOCK), SUB_BLOCK)], zsem.at[1])

    def zero_block_copy(rb):
        return pltpu.make_async_copy(zbuf, xs_hbm.at[pl.ds(rb * SLOT_BLOCK, SLOT_BLOCK)], zsem.at[2])

    def for_each_fill(row_fn, sub_block_fn, block_fn):
        def per_expert(e, carry):
            lo, mid, n_sub = pad_rows(e)
            lax.fori_loop(lo, mid, lambda r, c: (row_fn(r), c)[1], 0)
            lax.fori_loop(0, n_sub, lambda j, c: (sub_block_fn(mid + j * SUB_BLOCK), c)[1], 0)
            return carry
        lax.fori_loop(0, N_EXPERTS, per_expert, 0)
        lax.fori_loop(nused_ref[0], n_blocks, lambda rb, c: (block_fn(rb), c)[1], 0)

    @pl.when(i == 0)
    def _():
        zbuf[...] = jnp.zeros_like(zbuf)
        for_each_fill(lambda r: zero_row_copy(r).start(), lambda r: zero_sub_block_copy(r).start(),
                      lambda rb: zero_block_copy(rb).start())

    def scatter(src_ref, first_index, k_stride):
        rows = src_ref.shape[0]

        def body(t, carry):
            for k in range(TOP_K):
                d = dest_ref[first_index + k * k_stride + t]
                pltpu.make_async_copy(
                    src_ref.at[pl.ds(t, 1)], xs_hbm.at[pl.ds(d, 1)], sem).start(priority=k % 2)
            return carry
        lax.fori_loop(0, rows, body, 0, unroll=8)
        for _ in range(TOP_K):
            pltpu.make_async_copy(src_ref, xs_hbm.at[pl.ds(0, rows)], sem).wait()

    @pl.when(i < n_prompt_tiles)
    def _():
        scatter(hp_p_ref, i * tm_p, n_prompt)

    @pl.when(i >= n_prompt_tiles)
    def _():
        scatter(hp_s_ref, TOP_K * n_prompt + (i - n_prompt_tiles) * tm_s, n_sample)

    @pl.when(i == 0)
    def _():
        for_each_fill(lambda r: zero_row_copy(r).wait(), lambda r: zero_sub_block_copy(r).wait(),
                      lambda rb: zero_block_copy(rb).wait())


def _dispatch(dest, counts, region, n_used, hp_p, hp_s, n_blocks):
    tm_p = min(TM_DISPATCH, hp_p.shape[0])
    tm_s = min(TM_DISPATCH, hp_s.shape[0])
    assert hp_p.shape[0] % tm_p == 0 and hp_s.shape[0] % tm_s == 0
    ntp = hp_p.shape[0] // tm_p
    nts = hp_s.shape[0] // tm_s
    half = D_MODEL // 2
    return pl.pallas_call(
        functools.partial(_dispatch_kernel, hp_p.shape[0], hp_s.shape[0]),
        name="dispatch",
        grid_spec=pltpu.PrefetchScalarGridSpec(
            num_scalar_prefetch=4,
            grid=(ntp + nts,),
            in_specs=[
                pl.BlockSpec((tm_p, half), lambda i, *_: (jnp.minimum(i, ntp - 1), 0)),
                pl.BlockSpec((tm_s, half), lambda i, *_: (jnp.maximum(i - ntp, 0), 0)),
            ],
            out_specs=pl.BlockSpec(memory_space=pl.ANY),
            scratch_shapes=[
                pltpu.VMEM((SLOT_BLOCK, half), jnp.int32),
                pltpu.SemaphoreType.DMA,
                pltpu.SemaphoreType.DMA((3,)),
            ],
        ),
        out_shape=jax.ShapeDtypeStruct((n_blocks * SLOT_BLOCK, half), jnp.int32),
        compiler_params=pltpu.CompilerParams(dimension_semantics=("arbitrary",)),
    )(dest, counts, region, n_used, hp_p, hp_s)


def _unpack_bf16_pairs(w):
    lo = lax.bitcast_convert_type(lax.shift_left(w, 16), F32).astype(BF16)
    hi = lax.bitcast_convert_type(w & jnp.int32(-65536), F32).astype(BF16)
    return jnp.concatenate([lo, hi], axis=1)


def _for_each_sub_block(nsub, compute, out_ref):
    n_sub_max = SLOT_BLOCK // SUB_BLOCK

    for n in range(1, n_sub_max + 1):
        @pl.when(nsub == n)
        def _(n=n):
            compute(pl.ds(0, n * SUB_BLOCK))

    for s in range(n_sub_max):
        rows = pl.ds(s * SUB_BLOCK, SUB_BLOCK)

        @pl.when(s >= nsub)
        def _(rows=rows):
            out_ref[rows] = jnp.zeros((SUB_BLOCK,) + tuple(out_ref.shape[1:]), out_ref.dtype)


def _grouped_call(name, step, grid, in_specs_fn, out_spec, operands, residents, out_shape, scratch_shapes):
    n_in = len(operands)
    n_res = len(residents)

    def outer(be_ref, first_ref, nsub_ref, *refs):
        ins, res_hbm, out = refs[:n_in], refs[n_in:n_in + n_res], refs[n_in + n_res]
        scratches = refs[n_in + n_res + 1:]
        res_vmem = scratches[len(scratches) - n_res:]
        for src, dst in zip(res_hbm, res_vmem):
            pltpu.sync_copy(src, dst)

        def body(*block_refs):
            step(be_ref, first_ref, nsub_ref, *block_refs)

        pltpu.emit_pipeline(
            body, grid=grid, in_specs=in_specs_fn(be_ref), out_specs=[out_spec],
        )(*ins, out, scratches=scratches)

    any_spec = pl.BlockSpec(memory_space=pl.ANY)
    return lambda block_expert, first, nsub: pl.pallas_call(
        outer,
        name=name,
        grid_spec=pltpu.PrefetchScalarGridSpec(
            num_scalar_prefetch=3,
            grid=(),
            in_specs=[any_spec] * (n_in + n_res),
            out_specs=any_spec,
            scratch_shapes=list(scratch_shapes) + [pltpu.VMEM(r.shape, r.dtype) for r in residents],
        ),
        out_shape=out_shape,
        compiler_params=pltpu.CompilerParams(vmem_limit_bytes=VMEM_LIMIT),
    )(block_expert, first, nsub, *operands, *residents)


def _expert_weight_spec(be_ref, shape):
    return pl.BlockSpec(shape, lambda j, rb: (be_ref[rb], 0, j),
                        pipeline_mode=pl.Buffered(2, use_lookahead=True))


def _ffn_up_step(be_ref, first_ref, nsub_ref, x_ref, wg_ref, wu_ref, act_ref, wgb, wub, bg_ref, bu_ref):
    rb = pl.program_id(1)
    expert = pl.ds(be_ref[rb], 1)
    cols = pl.ds(pl.multiple_of(pl.program_id(0) * FF_TILE, FF_TILE), FF_TILE)

    @pl.when(first_ref[rb] == 1)
    def _():
        wgb[...] = wg_ref[0].astype(BF16)
        wub[...] = wu_ref[0].astype(BF16)

    def compute(rows):
        x = _unpack_bf16_pairs(x_ref[rows, :])
        gate = jnp.minimum(_dot(x, wgb[...]) + bg_ref[expert, cols], SWIGLU_LIMIT)
        up = jnp.clip(_dot(x, wub[...]) + bu_ref[expert, cols], -SWIGLU_LIMIT, SWIGLU_LIMIT)
        glu = gate * jax.nn.sigmoid(SWIGLU_ALPHA * gate)
        act_ref[rows, :] = ((up + 1.0) * glu).astype(BF16)

    _for_each_sub_block(nsub_ref[rb], compute, act_ref)


def _ffn_up(block_expert, first, nsub, xs, w_gate, b_gate, w_up, b_up, n_blocks):
    def in_specs(be_ref):
        w_spec = _expert_weight_spec(be_ref, (1, D_MODEL, FF_TILE))
        return [pl.BlockSpec((SLOT_BLOCK, D_MODEL // 2), lambda f, rb: (rb, 0)), w_spec, w_spec]

    return _grouped_call(
        "ffn_up", _ffn_up_step, (D_FF // FF_TILE, n_blocks), in_specs,
        pl.BlockSpec((SLOT_BLOCK, FF_TILE), lambda f, rb: (rb, f)),
        (xs, w_gate, w_up), (b_gate, b_up),
        jax.ShapeDtypeStruct((n_blocks * SLOT_BLOCK, D_FF), BF16),
        [pltpu.VMEM((D_MODEL, FF_TILE), BF16), pltpu.VMEM((D_MODEL, FF_TILE), BF16)],
    )(block_expert, first, nsub)


def _ffn_down_step(be_ref, first_ref, nsub_ref, act_ref, wd_ref, y_ref, wdb, bd_ref):
    rb = pl.program_id(1)
    expert = pl.ds(be_ref[rb], 1)

    @pl.when(first_ref[rb] == 1)
    def _():
        wdb[...] = wd_ref[0].astype(BF16)

    def compute(rows):
        y = _dot(act_ref[rows, :], wdb[...]) + bd_ref[expert, :]
        y_ref[rows, :, :] = _pack_bf16_pairs(y.astype(BF16)).reshape(-1, SUBLANES, LANES)

    _for_each_sub_block(nsub_ref[rb], compute, y_ref)


def _ffn_down(block_expert, first, nsub, act, w_down, b_down, n_blocks):
    def in_specs(be_ref):
        return [
            pl.BlockSpec((SLOT_BLOCK, D_FF), lambda n, rb: (rb, 0)),
            _expert_weight_spec(be_ref, (1, D_FF, D_MODEL)),
        ]

    return _grouped_call(
        "ffn_down", _ffn_down_step, (1, n_blocks), in_specs,
        pl.BlockSpec((SLOT_BLOCK, SUBLANES, LANES), lambda n, rb: (rb, 0, 0)),
        (act, w_down), (b_down,),
        jax.ShapeDtypeStruct((n_blocks * SLOT_BLOCK, SUBLANES, LANES), jnp.int32),
        [pltpu.VMEM((D_FF, D_MODEL), BF16)],
    )(block_expert, first, nsub)


def _combine_kernel(n_tokens, dest_ref, x1_ref, meta_ref, g_ref, ys_hbm, out_ref, ybuf0, ybuf1, sem0, sem1):
    i = pl.program_id(0)
    n_steps = pl.num_programs(0)
    tm = TM_COMBINE
    half = D_MODEL // 2

    def gather(tile, ybuf, sem):
        def body(grp, carry):
            for j in range(SUBLANES):
                token = tile * tm + grp * SUBLANES + j
                for k in range(TOP_K):
                    d = dest_ref[k * n_tokens + token]
                    pltpu.make_async_copy(
                        ys_hbm.at[pl.ds(d, 1)],
                        ybuf.at[pl.ds(k * tm + grp * SUBLANES + j, 1)], sem).start(priority=k % 2)
            return carry
        lax.fori_loop(0, tm // SUBLANES, body, 0)

    def finish(r0, ybuf, sem):
        pltpu.make_async_copy(ys_hbm.at[pl.ds(0, TOP_K * tm)], ybuf, sem).wait()
        lo = x1_ref[r0:r0 + tm, 0:half]
        hi = x1_ref[r0:r0 + tm, half:D_MODEL]
        for k in range(TOP_K):
            w = ybuf[k * tm:(k + 1) * tm].reshape(tm, half)
            gate = meta_ref[r0:r0 + tm, k:k + 1]
            lo = lo + lax.bitcast_convert_type(lax.shift_left(w, 16), F32) * gate
            hi = hi + lax.bitcast_convert_type(w & jnp.int32(-65536), F32) * gate
        ms = (jnp.sum(lo * lo, axis=-1, keepdims=True) + jnp.sum(hi * hi, axis=-1, keepdims=True)) / D_MODEL
        scale = lax.rsqrt(ms + EPS)
        out_ref[r0:r0 + tm, 0:half] = lo * scale * g_ref[:, 0:half]
        out_ref[r0:r0 + tm, half:D_MODEL] = hi * scale * g_ref[:, half:D_MODEL]

    @pl.when(i == 0)
    def _():
        gather(0, ybuf0, sem0)

    gather(2 * i + 1, ybuf1, sem1)
    finish(0, ybuf0, sem0)

    @pl.when(i + 1 < n_steps)
    def _():
        gather(2 * i + 2, ybuf0, sem0)

    finish(tm, ybuf1, sem1)


def _combine(dest_flat, x1, meta, g_final, ys):
    n = x1.shape[0]
    tm = 2 * TM_COMBINE
    assert n % tm == 0
    ybuf = pltpu.VMEM((TOP_K * TM_COMBINE, SUBLANES, LANES), jnp.int32)
    return pl.pallas_call(
        functools.partial(_combine_kernel, n),
        name="combine",
        grid_spec=pltpu.PrefetchScalarGridSpec(
            num_scalar_prefetch=1,
            grid=(n // tm,),
            in_specs=[
                pl.BlockSpec((tm, D_MODEL), lambda i, d: (i, 0)),
                pl.BlockSpec((tm, LANES), lambda i, d: (i, 0)),
                pl.BlockSpec((1, D_MODEL), lambda i, d: (0, 0)),
                pl.BlockSpec(memory_space=pl.ANY),
            ],
            out_specs=pl.BlockSpec((tm, D_MODEL), lambda i, d: (i, 0)),
            scratch_shapes=[ybuf, ybuf, pltpu.SemaphoreType.DMA, pltpu.SemaphoreType.DMA],
        ),
        out_shape=jax.ShapeDtypeStruct((n, D_MODEL), F32),
        compiler_params=pltpu.CompilerParams(
            dimension_semantics=("arbitrary",), vmem_limit_bytes=VMEM_LIMIT),
    )(dest_flat, x1, meta, g_final, ys)


def kernel(x_prompt, x_sample, cache_pool, norm_mix_g, w_in, ln_v_g, ln_v_b, w_spatial, b_spatial, w_pool,
           pool_scale, out_norm_a_g, out_norm_b_g, w_out, norm_ffn_g, w_router, b_router, w_gate, b_gate,
           w_up, b_up, w_down, b_down, final_norm_g):
    depth = norm_mix_g.shape[0]
    assert depth == 1
    bsz, seq, _ = x_prompt.shape
    dec_b, dec_t, _ = x_sample.shape
    assert seq % TM_PROMPT == 0 and TM_PROMPT % GMLP_CHUNK == 0
    assert GMLP_CHUNK % dec_t == 0 and (dec_b * dec_t) % GMLP_CHUNK == 0 and PAST_LEN % GMLP_CHUNK == 0
    assert dec_t >= POOL_STATE and PAST_LEN + 1 >= POOL_MAX
    n_p = bsz * seq
    n_s = dec_b * dec_t
    n = n_p + n_s
    assert n_p % TM_COMBINE == 0 and n_s % TM_COMBINE == 0
    l = 0

    row = lambda v: v.reshape(1, -1)
    shared = [
        row(norm_mix_g[l]), w_in[l].astype(BF16), row(ln_v_g[l]), row(ln_v_b[l]),
    ]
    tail = [
        w_pool[l].astype(BF16), row(pool_scale[l]), row(out_norm_a_g[l]), row(out_norm_b_g[l]),
        w_out[l].astype(BF16), row(norm_ffn_g[l]),
        jnp.pad(w_router[l].astype(BF16), ((0, 0), (0, LANES - N_EXPERTS))),
        jnp.pad(row(b_router[l]), ((0, 0), (0, LANES - N_EXPERTS))),
    ]
    weights_p = shared + [w_spatial[l], b_spatial[l].T] + tail
    reps = GMLP_CHUNK // dec_t
    weights_s = shared + [
        jnp.tile(w_spatial[l][:, :dec_t, :dec_t], (1, reps, reps)),
        jnp.tile(b_spatial[l][:, :dec_t].T, (reps, 1)),
    ] + tail

    x1_p, hp_p, meta_p, route_p, ptail, cnt_p = _mixer_prompt(x_prompt, weights_p)
    hist = jnp.pad(cache_pool[l], ((0, 0), (POOL_MAX - POOL_STATE, 0), (0, 0)))
    n_blocks = -(-(n * TOP_K + N_EXPERTS * (SLOT_BLOCK - 1)) // SLOT_BLOCK)
    x1_s, hp_s, meta_s, _, vn_s, p_s, dest_p, dest_s, cnt, region, tables = _mixer_sample(
        x_sample.reshape(n_s, D_MODEL), hist, cnt_p, route_p, weights_s, n_blocks)

    dest_p = dest_p.reshape(-1)
    dest_s = dest_s.reshape(-1)
    block_expert = tables[0, :n_blocks]
    first = tables[1, :n_blocks]
    nsub = tables[2, :n_blocks]
    n_used = tables[3, :1]

    xs = _dispatch(jnp.concatenate([dest_p, dest_s]), cnt[:, 0], region[:, 0], n_used, hp_p, hp_s, n_blocks)
    act = _ffn_up(block_expert, first, nsub, xs, w_gate[l], b_gate[l], w_up[l], b_up[l], n_blocks)
    ys = _ffn_down(block_expert, first, nsub, act, w_down[l], b_down[l], n_blocks)

    g_final = row(final_norm_g)
    y_p = _combine(dest_p, x1_p.reshape(n_p, D_MODEL), meta_p, g_final, ys)
    y_s = _combine(dest_s, x1_s, meta_s, g_final, ys)

    y_prompt = y_p.reshape(bsz, seq, D_MODEL)
    y_sample = y_s.reshape(dec_b, dec_t, D_MODEL)
    state_pool_prompt = ptail[:, POOL_MAX - POOL_STATE:][None]
    p_s3 = p_s.reshape(dec_b, dec_t, D_B)
    state_pool_sample = p_s3[:, dec_t - POOL_STATE:][None]
    state_chunk_v_sample = vn_s.reshape(dec_b, dec_t, D_A)[None]
    return (y_prompt, y_sample, state_pool_prompt, state_pool_sample, state_chunk_v_sample)
```

```python
import functools

import jax
import jax.numpy as jnp
from jax import lax
from jax.experimental import pallas as pl
from jax.experimental.pallas import tpu as pltpu

D_MODEL = 2048
D_A = 1024
D_B = 1024
GMLP_CHUNK = 128
N_HEADS_A = 8
HEAD_DIM_A = D_A // N_HEADS_A
POOL_WINDOWS = (2, 4, 8, 16)
N_POOL_GROUPS = len(POOL_WINDOWS)
POOL_GROUP_DIM = D_B // N_POOL_GROUPS
POOL_MAX = 16
POOL_STATE = POOL_MAX - 1
PAST_LEN = 1024
N_EXPERTS = 32
TOP_K = 4
D_FF = 2048
SWIGLU_LIMIT = 7.0
SWIGLU_ALPHA = 1.702
EPS = 1e-5

LANES = 128
SUBLANES = 8
TM_PROMPT = 256
SLOT_BLOCK = 512
SUB_BLOCK = 64
TM_DISPATCH = 2048
FF_TILE = 1024
TM_COMBINE = 128
VMEM_LIMIT = 56 * 1024 * 1024

BF16 = jnp.bfloat16
F32 = jnp.float32


def _dot(a, b):
    return jnp.dot(a, b, preferred_element_type=F32)


def _rms_norm(x, g):
    return x * lax.rsqrt(jnp.mean(x * x, axis=-1, keepdims=True) + EPS) * g


def _layer_norm(x, g, b):
    mu = jnp.mean(x, axis=-1, keepdims=True)
    xc = x - mu
    return xc * lax.rsqrt(jnp.mean(xc * xc, axis=-1, keepdims=True) + EPS) * g + b


def _gelu(x):
    return 0.5 * x * (1.0 + lax.erf(x * (2.0 ** -0.5)))


def _mixer_front(x, g_mix, w_in, ln_g, ln_b):
    h = _rms_norm(x, g_mix[...]).astype(BF16)
    u = _gelu(_dot(h, w_in[:, 0:D_A]))
    vn = _layer_norm(_dot(h, w_in[:, D_A:2 * D_A]), ln_g[...], ln_b[...])
    p = _dot(h, w_in[:, 2 * D_A:2 * D_A + D_B])
    return u, vn, p


def _spatial_gate(u, vn, ws, bs_t, chunk, a_ref):
    rows = u.shape[0]
    t = lax.broadcasted_iota(jnp.int32, (GMLP_CHUNK, GMLP_CHUNK), 0)
    s = lax.broadcasted_iota(jnp.int32, (GMLP_CHUNK, GMLP_CHUNK), 1)
    mask = (s <= t) & ((t // chunk) == (s // chunk))
    vb = vn.astype(BF16)
    for h in range(N_HEADS_A):
        w = jnp.where(mask, ws[h], 0.0).astype(BF16)
        bias = bs_t[:, h:h + 1]
        c0 = h * HEAD_DIM_A
        for r0 in range(0, rows, GMLP_CHUNK):
            mixed = _dot(w, vb[r0:r0 + GMLP_CHUNK, c0:c0 + HEAD_DIM_A]) + bias
            a_ref[r0:r0 + GMLP_CHUNK, c0:c0 + HEAD_DIM_A] = u[r0:r0 + GMLP_CHUNK, c0:c0 + HEAD_DIM_A] * mixed


def _window_sums(e):
    parts = []
    s = e
    for g, w in enumerate(POOL_WINDOWS):
        s = s + pltpu.roll(s, w // 2, 0)
        parts.append(s[:, 0:POOL_GROUP_DIM])
        if g + 1 < N_POOL_GROUPS:
            s = s[:, POOL_GROUP_DIM:]
    return parts


def _pool_project(diffs, w_pool, pool_scale, b_ref):
    for g in range(N_POOL_GROUPS):
        c0 = g * POOL_GROUP_DIM
        out = _dot(diffs[g].astype(BF16), w_pool[g])
        b_ref[:, c0:c0 + POOL_GROUP_DIM] = out * pool_scale[:, c0:c0 + POOL_GROUP_DIM]


def _mixer_back(x, a_ref, b_ref, g_a, g_b, w_out, g_ffn):
    na = _rms_norm(a_ref[...], g_a[...]).astype(BF16)
    nb = _rms_norm(b_ref[...], g_b[...]).astype(BF16)
    x1 = x + _dot(na, w_out[0:D_A, :]) + _dot(nb, w_out[D_A:D_A + D_B, :])
    hb = _rms_norm(x1, g_ffn[...]).astype(BF16)
    return x1, hb


def _pack_bf16_pairs(hb):
    half = D_MODEL // 2
    lo = lax.bitcast_convert_type(hb[:, 0:half].astype(F32), jnp.int32)
    hi = lax.bitcast_convert_type(hb[:, half:D_MODEL].astype(F32), jnp.int32)
    return lax.shift_right_logical(lo, 16) | hi


def _route(hb, w_router, b_router, run_ref):
    rows = hb.shape[0]
    logits = _dot(hb, w_router[...]) + b_router[...]
    l = logits.T[0:N_EXPERTS]
    sub = lax.broadcasted_iota(jnp.int32, (N_EXPERTS, rows), 0)
    vals, idxs = [], []
    for _ in range(TOP_K):
        m = jnp.max(l, axis=0, keepdims=True)
        i = jnp.min(jnp.where(l == m, sub, N_EXPERTS), axis=0, keepdims=True)
        vals.append(m)
        idxs.append(i)
        l = jnp.where(sub == i, -jnp.inf, l)
    exps = [jnp.exp(v - vals[0]) for v in vals]
    denom = exps[0] + exps[1] + exps[2] + exps[3]
    gates = [e / denom for e in exps]

    onehot = jnp.zeros((N_EXPERTS, rows), F32)
    for i in idxs:
        onehot = jnp.where(sub == i, 1.0, onehot)
    s = lax.broadcasted_iota(jnp.int32, (rows, rows), 0)
    t = lax.broadcasted_iota(jnp.int32, (rows, rows), 1)
    earlier = jnp.where(s < t, 1.0, 0.0).astype(BF16)
    base = _dot(onehot.astype(BF16), earlier) + run_ref[:, 0:1]
    ranks = [jnp.sum(jnp.where(sub == i, base, 0.0), axis=0, keepdims=True) for i in idxs]
    run_ref[...] = run_ref[...] + jnp.sum(onehot, axis=1, keepdims=True)

    route_t = jnp.concatenate(idxs + [r.astype(jnp.int32) for r in ranks], axis=0)
    sub_g = lax.broadcasted_iota(jnp.int32, (LANES, rows), 0)
    gates_t = jnp.zeros((LANES, rows), F32)
    for j, g in enumerate(gates):
        gates_t = jnp.where(sub_g == j, g, gates_t)
    return gates_t.T, route_t


def _mixer_prompt_kernel(x_ref, g_mix, w_in, ln_g, ln_b, ws, bs_t, w_pool, pool_scale, g_a, g_b, w_out,
                         g_ffn, w_router, b_router,
                         x1_ref, hp_ref, meta_ref, route_ref, ptail_ref, cnt_ref,
                         halo_ref, run_ref, a_ref, b_ref):
    tm = x_ref.shape[1]
    j = pl.program_id(1)

    @pl.when((pl.program_id(0) == 0) & (j == 0))
    def _():
        run_ref[...] = jnp.zeros_like(run_ref)

    @pl.when(j == 0)
    def _():
        halo_ref[...] = jnp.zeros_like(halo_ref)

    x = x_ref[0]
    u, vn, p = _mixer_front(x, g_mix, w_in, ln_g, ln_b)
    _spatial_gate(u, vn, ws, bs_t, GMLP_CHUNK, a_ref)

    e = jnp.concatenate([halo_ref[...], p], axis=0)
    halo_ref[...] = p[tm - POOL_MAX:tm]
    ptail_ref[0] = p[tm - POOL_MAX:tm]
    pos = j * tm + lax.broadcasted_iota(jnp.int32, (tm, 1), 0)
    sums = _window_sums(e)
    diffs = []
    for g, w in enumerate(POOL_WINDOWS):
        inv_cnt = 1.0 / jnp.minimum(pos + 1, w).astype(F32)
        c0 = g * POOL_GROUP_DIM
        diffs.append(sums[g][POOL_MAX:] * inv_cnt - p[:, c0:c0 + POOL_GROUP_DIM])
    _pool_project(diffs, w_pool, pool_scale, b_ref)

    x1, hb = _mixer_back(x, a_ref, b_ref, g_a, g_b, w_out, g_ffn)
    x1_ref[0] = x1
    hp_ref[...] = _pack_bf16_pairs(hb)
    meta_ref[...], route_ref[...] = _route(hb, w_router, b_router, run_ref)
    cnt_ref[...] = run_ref[...]


def _slot_layout(run_ref, route_refs, dest_refs, cnt_ref, region_ref, tab_ref):
    cnt = run_ref[...].astype(jnp.int32)
    sub = lax.broadcasted_iota(jnp.int32, cnt.shape, 0)
    blocks = (cnt + (SLOT_BLOCK - 1)) // SLOT_BLOCK
    blk_end = blocks
    shift = 1
    while shift < N_EXPERTS:
        blk_end = blk_end + jnp.where(sub >= shift, pltpu.roll(blk_end, shift, 0), 0)
        shift *= 2
    blk_start = blk_end - blocks
    region = blk_start * SLOT_BLOCK
    rows_last = cnt - (blocks - 1) * SLOT_BLOCK
    cnt_ref[...] = cnt
    region_ref[...] = region

    def of_expert(table, e):
        return table[e:e + 1, 0:1]

    for route_ref, dest_ref in zip(route_refs, dest_refs):
        idx = route_ref[0:TOP_K, :]
        first_slot = jnp.zeros_like(idx)
        for e in range(N_EXPERTS):
            first_slot = jnp.where(idx == e, of_expert(region, e), first_slot)
        dest_ref[...] = first_slot + route_ref[TOP_K:2 * TOP_K, :]

    n_used = of_expert(blk_end, N_EXPERTS - 1)
    blk = lax.broadcasted_iota(jnp.int32, (1, tab_ref.shape[1]), 1)
    blk_c = jnp.minimum(blk, n_used - 1)
    expert = jnp.zeros_like(blk)
    for e in range(N_EXPERTS - 1):
        expert = expert + jnp.where(blk_c >= of_expert(blk_end, e), 1, 0)
    start = jnp.zeros_like(blk)
    end = jnp.zeros_like(blk)
    last_rows = jnp.zeros_like(blk)
    for e in range(N_EXPERTS):
        hit = expert == e
        start = jnp.where(hit, of_expert(blk_start, e), start)
        end = jnp.where(hit, of_expert(blk_end, e), end)
        last_rows = jnp.where(hit, of_expert(rows_last, e), last_rows)
    used = blk < n_used
    first = jnp.where((blk_c == start) & used, 1, 0)
    sub_last = (last_rows + (SUB_BLOCK - 1)) // SUB_BLOCK
    nsub = jnp.where(used, jnp.where(blk_c == end - 1, sub_last, SLOT_BLOCK // SUB_BLOCK), 0)
    n_used_row = jnp.zeros_like(blk) + n_used
    zero = jnp.zeros_like(blk)
    tab_ref[...] = jnp.concatenate([expert, first, nsub, n_used_row, zero, zero, zero, zero], axis=0)


def _mixer_sample_kernel(x_ref, hist_ref, cnt_in_ref, route_p_ref,
                         g_mix, w_in, ln_g, ln_b, ws, bs_t, w_pool, pool_scale,
                         g_a, g_b, w_out, g_ffn, w_router, b_router,
                         x1_ref, hp_ref, meta_ref, route_ref, vn_ref, p_ref,
                         dest_p_ref, dest_s_ref, cnt_ref, region_ref, tab_ref,
                         run_ref, a_ref, b_ref, ext_ref):
    n_streams, hist_rows, _ = hist_ref.shape
    t_new = x_ref.shape[0] // n_streams
    group = hist_rows + t_new
    run_ref[...] = cnt_in_ref[...]

    x = x_ref[...]
    u, vn, p = _mixer_front(x, g_mix, w_in, ln_g, ln_b)
    vn_ref[...] = vn
    p_ref[...] = p
    _spatial_gate(u, vn, ws, bs_t, t_new, a_ref)

    for b in range(n_streams):
        ext_ref[b * group:b * group + hist_rows, :] = hist_ref[b]
        ext_ref[b * group + hist_rows:(b + 1) * group, :] = p[b * t_new:(b + 1) * t_new]
    e = ext_ref[...]
    sums = _window_sums(e)
    diffs = []
    for g, w in enumerate(POOL_WINDOWS):
        c0 = g * POOL_GROUP_DIM
        d = sums[g] * (1.0 / w) - e[:, c0:c0 + POOL_GROUP_DIM]
        diffs.append(jnp.concatenate(
            [d[b * group + hist_rows:(b + 1) * group] for b in range(n_streams)], axis=0))
    _pool_project(diffs, w_pool, pool_scale, b_ref)

    x1, hb = _mixer_back(x, a_ref, b_ref, g_a, g_b, w_out, g_ffn)
    x1_ref[...] = x1
    hp_ref[...] = _pack_bf16_pairs(hb)
    meta_ref[...], route_ref[...] = _route(hb, w_router, b_router, run_ref)
    _slot_layout(run_ref, (route_p_ref, route_ref), (dest_p_ref, dest_s_ref), cnt_ref, region_ref, tab_ref)


def _const_spec(shape):
    zeros = (0,) * len(shape)
    return pl.BlockSpec(shape, lambda *_: zeros, pipeline_mode=pl.Buffered(1))


def _mixer_weight_specs():
    return [
        _const_spec((1, D_MODEL)),
        _const_spec((D_MODEL, 2 * D_A + D_B)),
        _const_spec((1, D_A)), _const_spec((1, D_A)),
        _const_spec((N_HEADS_A, GMLP_CHUNK, GMLP_CHUNK)),
        _const_spec((GMLP_CHUNK, N_HEADS_A)),
        _const_spec((N_POOL_GROUPS, POOL_GROUP_DIM, POOL_GROUP_DIM)),
        _const_spec((1, D_B)),
        _const_spec((1, D_A)), _const_spec((1, D_B)),
        _const_spec((D_A + D_B, D_MODEL)),
        _const_spec((1, D_MODEL)),
        _const_spec((D_MODEL, LANES)),
        _const_spec((1, LANES)),
    ]


def _mixer_prompt(x, weights):
    bsz, seq, _ = x.shape
    tm = TM_PROMPT
    nt = seq // tm
    return pl.pallas_call(
        _mixer_prompt_kernel,
        name="mixer_prompt",
        grid=(bsz, nt),
        in_specs=[pl.BlockSpec((1, tm, D_MODEL), lambda b, j: (b, j, 0))] + _mixer_weight_specs(),
        out_specs=[
            pl.BlockSpec((1, tm, D_MODEL), lambda b, j: (b, j, 0)),
            pl.BlockSpec((tm, D_MODEL // 2), lambda b, j: (b * nt + j, 0)),
            pl.BlockSpec((tm, LANES), lambda b, j: (b * nt + j, 0)),
            pl.BlockSpec((2 * TOP_K, tm), lambda b, j: (0, b * nt + j)),
            pl.BlockSpec((1, POOL_MAX, D_B), lambda b, j: (b, 0, 0)),
            pl.BlockSpec((N_EXPERTS, LANES), lambda b, j: (0, 0)),
        ],
        out_shape=[
            jax.ShapeDtypeStruct((bsz, seq, D_MODEL), F32),
            jax.ShapeDtypeStruct((bsz * seq, D_MODEL // 2), jnp.int32),
            jax.ShapeDtypeStruct((bsz * seq, LANES), F32),
            jax.ShapeDtypeStruct((2 * TOP_K, bsz * seq), jnp.int32),
            jax.ShapeDtypeStruct((bsz, POOL_MAX, D_B), F32),
            jax.ShapeDtypeStruct((N_EXPERTS, LANES), F32),
        ],
        scratch_shapes=[
            pltpu.VMEM((POOL_MAX, D_B), F32),
            pltpu.VMEM((N_EXPERTS, LANES), F32),
            pltpu.VMEM((tm, D_A), F32),
            pltpu.VMEM((tm, D_B), F32),
        ],
        compiler_params=pltpu.CompilerParams(
            dimension_semantics=("arbitrary", "arbitrary"), vmem_limit_bytes=VMEM_LIMIT),
    )(x, *weights)


def _mixer_sample(x2d, hist, cnt_in, route_p, weights, n_blocks):
    rows = x2d.shape[0]
    n_p = route_p.shape[1]
    n_streams, hist_rows, _ = hist.shape
    ext_rows = rows + n_streams * hist_rows
    tab_lanes = -(-n_blocks // LANES) * LANES
    full = lambda shape: pl.BlockSpec(shape, lambda i: (0,) * len(shape))
    i32 = jnp.int32
    out_shapes = [
        ((rows, D_MODEL), F32),
        ((rows, D_MODEL // 2), i32),
        ((rows, LANES), F32),
        ((2 * TOP_K, rows), i32),
        ((rows, D_A), F32),
        ((rows, D_B), F32),
        ((TOP_K, n_p), i32),
        ((TOP_K, rows), i32),
        ((N_EXPERTS, LANES), i32),
        ((N_EXPERTS, LANES), i32),
        ((SUBLANES, tab_lanes), i32),
    ]
    return pl.pallas_call(
        _mixer_sample_kernel,
        name="mixer_sample",
        grid=(1,),
        in_specs=[full((rows, D_MODEL)), full(hist.shape), full((N_EXPERTS, LANES)), full(route_p.shape)]
        + _mixer_weight_specs(),
        out_specs=[full(shape) for shape, _ in out_shapes],
        out_shape=[jax.ShapeDtypeStruct(shape, dtype) for shape, dtype in out_shapes],
        scratch_shapes=[
            pltpu.VMEM((N_EXPERTS, LANES), F32),
            pltpu.VMEM((rows, D_A), F32),
            pltpu.VMEM((rows, D_B), F32),
            pltpu.VMEM((ext_rows, D_B), F32),
        ],
        compiler_params=pltpu.CompilerParams(
            dimension_semantics=("arbitrary",), vmem_limit_bytes=VMEM_LIMIT),
    )(x2d, hist, cnt_in, route_p, *weights)


def _dispatch_kernel(n_prompt, n_sample, dest_ref, cnt_ref, region_ref, nused_ref,
                     hp_p_ref, hp_s_ref, xs_hbm, zbuf, sem, zsem):
    i = pl.program_id(0)
    tm_p = hp_p_ref.shape[0]
    tm_s = hp_s_ref.shape[0]
    n_prompt_tiles = n_prompt // tm_p
    n_blocks = xs_hbm.shape[0] // SLOT_BLOCK

    def pad_rows(e):
        cnt = cnt_ref[e]
        lo = region_ref[e] + cnt
        mid = region_ref[e] + (cnt + SUB_BLOCK - 1) // SUB_BLOCK * SUB_BLOCK
        hi = region_ref[e] + (cnt + SLOT_BLOCK - 1) // SLOT_BLOCK * SLOT_BLOCK
        return lo, mid, (hi - mid) // SUB_BLOCK

    def zero_row_copy(r):
        return pltpu.make_async_copy(zbuf.at[pl.ds(0, 1)], xs_hbm.at[pl.ds(r, 1)], zsem.at[0])

    def zero_sub_block_copy(r):
        return pltpu.make_async_copy(
            zbuf.at[pl.ds(0, SUB_BLOCK)], xs_hbm.at[pl.ds(pl.multiple_of(r, SUB_BLOCK), SUB_BLOCK)], zsem.at[1])

    def zero_block_copy(rb):
        return pltpu.make_async_copy(zbuf, xs_hbm.at[pl.ds(rb * SLOT_BLOCK, SLOT_BLOCK)], zsem.at[2])

    def for_each_fill(row_fn, sub_block_fn, block_fn):
        def per_expert(e, carry):
            lo, mid, n_sub = pad_rows(e)
            lax.fori_loop(lo, mid, lambda r, c: (row_fn(r), c)[1], 0)
            lax.fori_loop(0, n_sub, lambda j, c: (sub_block_fn(mid + j * SUB_BLOCK), c)[1], 0)
            return carry
        lax.fori_loop(0, N_EXPERTS, per_expert, 0)
        lax.fori_loop(nused_ref[0], n_blocks, lambda rb, c: (block_fn(rb), c)[1], 0)

    @pl.when(i == 0)
    def _():
        zbuf[...] = jnp.zeros_like(zbuf)
        for_each_fill(lambda r: zero_row_copy(r).start(), lambda r: zero_sub_block_copy(r).start(),
                      lambda rb: zero_block_copy(rb).start())

    def scatter(src_ref, first_index, k_stride):
        rows = src_ref.shape[0]

        def body(t, carry):
            for k in range(TOP_K):
                d = dest_ref[first_index + k * k_stride + t]
                pltpu.make_async_copy(
                    src_ref.at[pl.ds(t, 1)], xs_hbm.at[pl.ds(d, 1)], sem).start(priority=k % 2)
            return carry
        lax.fori_loop(0, rows, body, 0, unroll=8)
        for _ in range(TOP_K):
            pltpu.make_async_copy(src_ref, xs_hbm.at[pl.ds(0, rows)], sem).wait()

    @pl.when(i < n_prompt_tiles)
    def _():
        scatter(hp_p_ref, i * tm_p, n_prompt)

    @pl.when(i >= n_prompt_tiles)
    def _():
        scatter(hp_s_ref, TOP_K * n_prompt + (i - n_prompt_tiles) * tm_s, n_sample)

    @pl.when(i == 0)
    def _():
        for_each_fill(lambda r: zero_row_copy(r).wait(), lambda r: zero_sub_block_copy(r).wait(),
                      lambda rb: zero_block_copy(rb).wait())


def _dispatch(dest, counts, region, n_used, hp_p, hp_s, n_blocks):
    tm_p = min(TM_DISPATCH, hp_p.shape[0])
    tm_s = min(TM_DISPATCH, hp_s.shape[0])
    assert hp_p.shape[0] % tm_p == 0 and hp_s.shape[0] % tm_s == 0
    ntp = hp_p.shape[0] // tm_p
    nts = hp_s.shape[0] // tm_s
    half = D_MODEL // 2
    return pl.pallas_call(
        functools.partial(_dispatch_kernel, hp_p.shape[0], hp_s.shape[0]),
        name="dispatch",
        grid_spec=pltpu.PrefetchScalarGridSpec(
            num_scalar_prefetch=4,
            grid=(ntp + nts,),
            in_specs=[
                pl.BlockSpec((tm_p, half), lambda i, *_: (jnp.minimum(i, ntp - 1), 0)),
                pl.BlockSpec((tm_s, half), lambda i, *_: (jnp.maximum(i - ntp, 0), 0)),
            ],
            out_specs=pl.BlockSpec(memory_space=pl.ANY),
            scratch_shapes=[
                pltpu.VMEM((SLOT_BLOCK, half), jnp.int32),
                pltpu.SemaphoreType.DMA,
                pltpu.SemaphoreType.DMA((3,)),
            ],
        ),
        out_shape=jax.ShapeDtypeStruct((n_blocks * SLOT_BLOCK, half), jnp.int32),
        compiler_params=pltpu.CompilerParams(dimension_semantics=("arbitrary",)),
    )(dest, counts, region, n_used, hp_p, hp_s)


def _unpack_bf16_pairs(w):
    lo = lax.bitcast_convert_type(lax.shift_left(w, 16), F32).astype(BF16)
    hi = lax.bitcast_convert_type(w & jnp.int32(-65536), F32).astype(BF16)
    return jnp.concatenate([lo, hi], axis=1)


def _for_each_sub_block(nsub, compute, out_ref):
    n_sub_max = SLOT_BLOCK // SUB_BLOCK

    for n in range(1, n_sub_max + 1):
        @pl.when(nsub == n)
        def _(n=n):
            compute(pl.ds(0, n * SUB_BLOCK))

    for s in range(n_sub_max):
        rows = pl.ds(s * SUB_BLOCK, SUB_BLOCK)

        @pl.when(s >= nsub)
        def _(rows=rows):
            out_ref[rows] = jnp.zeros((SUB_BLOCK,) + tuple(out_ref.shape[1:]), out_ref.dtype)


def _grouped_call(name, step, grid, in_specs_fn, out_spec, operands, residents, out_shape, scratch_shapes):
    n_in = len(operands)
    n_res = len(residents)

    def outer(be_ref, first_ref, nsub_ref, *refs):
        ins, res_hbm, out = refs[:n_in], refs[n_in:n_in + n_res], refs[n_in + n_res]
        scratches = refs[n_in + n_res + 1:]
        res_vmem = scratches[len(scratches) - n_res:]
        for src, dst in zip(res_hbm, res_vmem):
            pltpu.sync_copy(src, dst)

        def body(*block_refs):
            step(be_ref, first_ref, nsub_ref, *block_refs)

        pltpu.emit_pipeline(
            body, grid=grid, in_specs=in_specs_fn(be_ref), out_specs=[out_spec],
        )(*ins, out, scratches=scratches)

    any_spec = pl.BlockSpec(memory_space=pl.ANY)
    return lambda block_expert, first, nsub: pl.pallas_call(
        outer,
        name=name,
        grid_spec=pltpu.PrefetchScalarGridSpec(
            num_scalar_prefetch=3,
            grid=(),
            in_specs=[any_spec] * (n_in + n_res),
            out_specs=any_spec,
            scratch_shapes=list(scratch_shapes) + [pltpu.VMEM(r.shape, r.dtype) for r in residents],
        ),
        out_shape=out_shape,
        compiler_params=pltpu.CompilerParams(vmem_limit_bytes=VMEM_LIMIT),
    )(block_expert, first, nsub, *operands, *residents)


def _expert_weight_spec(be_ref, shape):
    return pl.BlockSpec(shape, lambda j, rb: (be_ref[rb], 0, j),
                        pipeline_mode=pl.Buffered(2, use_lookahead=True))


def _ffn_up_step(be_ref, first_ref, nsub_ref, x_ref, wg_ref, wu_ref, act_ref, wgb, wub, bg_ref, bu_ref):
    rb = pl.program_id(1)
    expert = pl.ds(be_ref[rb], 1)
    cols = pl.ds(pl.multiple_of(pl.program_id(0) * FF_TILE, FF_TILE), FF_TILE)

    @pl.when(first_ref[rb] == 1)
    def _():
        wgb[...] = wg_ref[0].astype(BF16)
        wub[...] = wu_ref[0].astype(BF16)

    def compute(rows):
        x = _unpack_bf16_pairs(x_ref[rows, :])
        gate = jnp.minimum(_dot(x, wgb[...]) + bg_ref[expert, cols], SWIGLU_LIMIT)
        up = jnp.clip(_dot(x, wub[...]) + bu_ref[expert, cols], -SWIGLU_LIMIT, SWIGLU_LIMIT)
        glu = gate * jax.nn.sigmoid(SWIGLU_ALPHA * gate)
        act_ref[rows, :] = ((up + 1.0) * glu).astype(BF16)

    _for_each_sub_block(nsub_ref[rb], compute, act_ref)


def _ffn_up(block_expert, first, nsub, xs, w_gate, b_gate, w_up, b_up, n_blocks):
    def in_specs(be_ref):
        w_spec = _expert_weight_spec(be_ref, (1, D_MODEL, FF_TILE))
        return [pl.BlockSpec((SLOT_BLOCK, D_MODEL // 2), lambda f, rb: (rb, 0)), w_spec, w_spec]

    return _grouped_call(
        "ffn_up", _ffn_up_step, (D_FF // FF_TILE, n_blocks), in_specs,
        pl.BlockSpec((SLOT_BLOCK, FF_TILE), lambda f, rb: (rb, f)),
        (xs, w_gate, w_up), (b_gate, b_up),
        jax.ShapeDtypeStruct((n_blocks * SLOT_BLOCK, D_FF), BF16),
        [pltpu.VMEM((D_MODEL, FF_TILE), BF16), pltpu.VMEM((D_MODEL, FF_TILE), BF16)],
    )(block_expert, first, nsub)


def _ffn_down_step(be_ref, first_ref, nsub_ref, act_ref, wd_ref, y_ref, wdb, bd_ref):
    rb = pl.program_id(1)
    expert = pl.ds(be_ref[rb], 1)

    @pl.when(first_ref[rb] == 1)
    def _():
        wdb[...] = wd_ref[0].astype(BF16)

    def compute(rows):
        y = _dot(act_ref[rows, :], wdb[...]) + bd_ref[expert, :]
        y_ref[rows, :, :] = _pack_bf16_pairs(y.astype(BF16)).reshape(-1, SUBLANES, LANES)

    _for_each_sub_block(nsub_ref[rb], compute, y_ref)


def _ffn_down(block_expert, first, nsub, act, w_down, b_down, n_blocks):
    def in_specs(be_ref):
        return [
            pl.BlockSpec((SLOT_BLOCK, D_FF), lambda n, rb: (rb, 0)),
            _expert_weight_spec(be_ref, (1, D_FF, D_MODEL)),
        ]

    return _grouped_call(
        "ffn_down", _ffn_down_step, (1, n_blocks), in_specs,
        pl.BlockSpec((SLOT_BLOCK, SUBLANES, LANES), lambda n, rb: (rb, 0, 0)),
        (act, w_down), (b_down,),
        jax.ShapeDtypeStruct((n_blocks * SLOT_BLOCK, SUBLANES, LANES), jnp.int32),
        [pltpu.VMEM((D_FF, D_MODEL), BF16)],
    )(block_expert, first, nsub)


def _combine_kernel(n_tokens, dest_ref, x1_ref, meta_ref, g_ref, ys_hbm, out_ref, ybuf0, ybuf1, sem0, sem1):
    i = pl.program_id(0)
    n_steps = pl.num_programs(0)
    tm = TM_COMBINE
    half = D_MODEL // 2

    def gather(tile, ybuf, sem):
        def body(grp, carry):
            for j in range(SUBLANES):
                token = tile * tm + grp * SUBLANES + j
                for k in range(TOP_K):
                    d = dest_ref[k * n_tokens + token]
                    pltpu.make_async_copy(
                        ys_hbm.at[pl.ds(d, 1)],
                        ybuf.at[pl.ds(k * tm + grp * SUBLANES + j, 1)], sem).start(priority=k % 2)
            return carry
        lax.fori_loop(0, tm // SUBLANES, body, 0)

    def finish(r0, ybuf, sem):
        pltpu.make_async_copy(ys_hbm.at[pl.ds(0, TOP_K * tm)], ybuf, sem).wait()
        lo = x1_ref[r0:r0 + tm, 0:half]
        hi = x1_ref[r0:r0 + tm, half:D_MODEL]
        for k in range(TOP_K):
            w = ybuf[k * tm:(k + 1) * tm].reshape(tm, half)
            gate = meta_ref[r0:r0 + tm, k:k + 1]
            lo = lo + lax.bitcast_convert_type(lax.shift_left(w, 16), F32) * gate
            hi = hi + lax.bitcast_convert_type(w & jnp.int32(-65536), F32) * gate
        ms = (jnp.sum(lo * lo, axis=-1, keepdims=True) + jnp.sum(hi * hi, axis=-1, keepdims=True)) / D_MODEL
        scale = lax.rsqrt(ms + EPS)
        out_ref[r0:r0 + tm, 0:half] = lo * scale * g_ref[:, 0:half]
        out_ref[r0:r0 + tm, half:D_MODEL] = hi * scale * g_ref[:, half:D_MODEL]

    @pl.when(i == 0)
    def _():
        gather(0, ybuf0, sem0)

    gather(2 * i + 1, ybuf1, sem1)
    finish(0, ybuf0, sem0)

    @pl.when(i + 1 < n_steps)
    def _():
        gather(2 * i + 2, ybuf0, sem0)

    finish(tm, ybuf1, sem1)


def _combine(dest_flat, x1, meta, g_final, ys):
    n = x1.shape[0]
    tm = 2 * TM_COMBINE
    assert n % tm == 0
    ybuf = pltpu.VMEM((TOP_K * TM_COMBINE, SUBLANES, LANES), jnp.int32)
    return pl.pallas_call(
        functools.partial(_combine_kernel, n),
        name="combine",
        grid_spec=pltpu.PrefetchScalarGridSpec(
            num_scalar_prefetch=1,
            grid=(n // tm,),
            in_specs=[
                pl.BlockSpec((tm, D_MODEL), lambda i, d: (i, 0)),
                pl.BlockSpec((tm, LANES), lambda i, d: (i, 0)),
                pl.BlockSpec((1, D_MODEL), lambda i, d: (0, 0)),
                pl.BlockSpec(memory_space=pl.ANY),
            ],
            out_specs=pl.BlockSpec((tm, D_MODEL), lambda i, d: (i, 0)),
            scratch_shapes=[ybuf, ybuf, pltpu.SemaphoreType.DMA, pltpu.SemaphoreType.DMA],
        ),
        out_shape=jax.ShapeDtypeStruct((n, D_MODEL), F32),
        compiler_params=pltpu.CompilerParams(
            dimension_semantics=("arbitrary",), vmem_limit_bytes=VMEM_LIMIT),
    )(dest_flat, x1, meta, g_final, ys)


def kernel(x_prompt, x_sample, cache_pool, norm_mix_g, w_in, ln_v_g, ln_v_b, w_spatial, b_spatial, w_pool,
           pool_scale, out_norm_a_g, out_norm_b_g, w_out, norm_ffn_g, w_router, b_router, w_gate, b_gate,
           w_up, b_up, w_down, b_down, final_norm_g):
    depth = norm_mix_g.shape[0]
    assert depth == 1
    bsz, seq, _ = x_prompt.shape
    dec_b, dec_t, _ = x_sample.shape
    assert seq % TM_PROMPT == 0 and TM_PROMPT % GMLP_CHUNK == 0
    assert GMLP_CHUNK % dec_t == 0 and (dec_b * dec_t) % GMLP_CHUNK == 0 and PAST_LEN % GMLP_CHUNK == 0
    assert dec_t >= POOL_STATE and PAST_LEN + 1 >= POOL_MAX
    n_p = bsz * seq
    n_s = dec_b * dec_t
    n = n_p + n_s
    assert n_p % TM_COMBINE == 0 and n_s % TM_COMBINE == 0
    l = 0

    row = lambda v: v.reshape(1, -1)
    shared = [
        row(norm_mix_g[l]), w_in[l].astype(BF16), row(ln_v_g[l]), row(ln_v_b[l]),
    ]
    tail = [
        w_pool[l].astype(BF16), row(pool_scale[l]), row(out_norm_a_g[l]), row(out_norm_b_g[l]),
        w_out[l].astype(BF16), row(norm_ffn_g[l]),
        jnp.pad(w_router[l].astype(BF16), ((0, 0), (0, LANES - N_EXPERTS))),
        jnp.pad(row(b_router[l]), ((0, 0), (0, LANES - N_EXPERTS))),
    ]
    weights_p = shared + [w_spatial[l], b_spatial[l].T] + tail
    reps = GMLP_CHUNK // dec_t
    weights_s = shared + [
        jnp.tile(w_spatial[l][:, :dec_t, :dec_t], (1, reps, reps)),
        jnp.tile(b_spatial[l][:, :dec_t].T, (reps, 1)),
    ] + tail

    x1_p, hp_p, meta_p, route_p, ptail, cnt_p = _mixer_prompt(x_prompt, weights_p)
    hist = jnp.pad(cache_pool[l], ((0, 0), (POOL_MAX - POOL_STATE, 0), (0, 0)))
    n_blocks = -(-(n * TOP_K + N_EXPERTS * (SLOT_BLOCK - 1)) // SLOT_BLOCK)
    x1_s, hp_s, meta_s, _, vn_s, p_s, dest_p, dest_s, cnt, region, tables = _mixer_sample(
        x_sample.reshape(n_s, D_MODEL), hist, cnt_p, route_p, weights_s, n_blocks)

    dest_p = dest_p.reshape(-1)
    dest_s = dest_s.reshape(-1)
    block_expert = tables[0, :n_blocks]
    first = tables[1, :n_blocks]
    nsub = tables[2, :n_blocks]
    n_used = tables[3, :1]

    xs = _dispatch(jnp.concatenate([dest_p, dest_s]), cnt[:, 0], region[:, 0], n_used, hp_p, hp_s, n_blocks)
    act = _ffn_up(block_expert, first, nsub, xs, w_gate[l], b_gate[l], w_up[l], b_up[l], n_blocks)
    ys = _ffn_down(block_expert, first, nsub, act, w_down[l], b_down[l], n_blocks)

    g_final = row(final_norm_g)
    y_p = _combine(dest_p, x1_p.reshape(n_p, D_MODEL), meta_p, g_final, ys)
    y_s = _combine(dest_s, x1_s, meta_s, g_final, ys)

    y_prompt = y_p.reshape(bsz, seq, D_MODEL)
    y_sample = y_s.reshape(dec_b, dec_t, D_MODEL)
    state_pool_prompt = ptail[:, POOL_MAX - POOL_STATE:][None]
    p_s3 = p_s.reshape(dec_b, dec_t, D_B)
    state_pool_sample = p_s3[:, dec_t - POOL_STATE:][None]
    state_chunk_v_sample = vn_s.reshape(dec_b, dec_t, D_A)[None]
    return (y_prompt, y_sample, state_pool_prompt, state_pool_sample, state_chunk_v_sample)
```
